```python
import jax, jax.numpy as jnp
from jax import lax
import numpy as np

D_MODEL = 1024
BATCH = 1
SEQ = 16384
DEPTH = 2

HEAD_DIM = 64
D_FF = 2816
D_PLE = 256
ATTN_Q_HEADS = 8
ATTN_KV_HEADS = 2
ATTN_GROUP = ATTN_Q_HEADS // ATTN_KV_HEADS
WINDOW = 128
ROPE_THETA = 500000.0
ROPE_DIM = HEAD_DIM // 4
MLSTM_HEADS = 4
MLSTM_CHUNK = 64
MLSTM_CONV = 4
GATE_CAP = 15.0
RWKV_HEADS = 4
RWKV_W_RANK = 64
RWKV_A_RANK = 64
RWKV_G_RANK = 128
RWKV_GN_EPS = 64e-5
NORM_EPS = 1e-6
NEG_INF = -1e30

ATTN_W = ATTN_Q_HEADS * HEAD_DIM
KV_W = ATTN_KV_HEADS * HEAD_DIM
MLSTM_W = MLSTM_HEADS * HEAD_DIM
RWKV_W = RWKV_HEADS * HEAD_DIM
D_MIX = ATTN_W + MLSTM_W + RWKV_W
ATTN_COLS = ATTN_W + 2 * KV_W
MLSTM_COLS = 4 * MLSTM_W + 2 * MLSTM_HEADS
RWKV_COLS = 3 * RWKV_W + RWKV_W_RANK + RWKV_A_RANK + RWKV_G_RANK
D_IN = ATTN_COLS + MLSTM_COLS + RWKV_COLS

kernel_name = "hybrid_mlstm_swa_rwkv7_block"


def rms_norm(x, gain):
    xf = x.astype(jnp.float32)
    y = xf * lax.rsqrt(jnp.mean(xf * xf, axis=-1, keepdims=True) + NORM_EPS)
    return (y * gain.astype(jnp.float32)).astype(x.dtype)


def swiglu(x, w_in, w_out):
    gate, up = jnp.split(x @ w_in, 2, axis=-1)
    return (jax.nn.silu(gate) * up) @ w_out


def rope_tables(positions):
    inv_freq = ROPE_THETA ** (-jnp.arange(0, ROPE_DIM, 2, dtype=jnp.float32) / ROPE_DIM)
    ang = positions.astype(jnp.float32)[..., None] * inv_freq
    return jnp.cos(ang)[:, :, None, :], jnp.sin(ang)[:, :, None, :]


def partial_rope(x, cos, sin):
    half = ROPE_DIM // 2
    x1 = x[..., :half].astype(jnp.float32)
    x2 = x[..., half:ROPE_DIM].astype(jnp.float32)
    rot = jnp.concatenate([x1 * cos - x2 * sin, x2 * cos + x1 * sin], axis=-1).astype(x.dtype)
    return jnp.concatenate([rot, x[..., ROPE_DIM:]], axis=-1)


def sliding_window_gqa(q, k, v, sinks):
    B, S = q.shape[:2]
    nb = S // WINDOW
    qb = q.reshape(B, nb, WINDOW, ATTN_KV_HEADS, ATTN_GROUP, HEAD_DIM)

    def band_keys(t):
        tb = t.reshape(B, nb, WINDOW, ATTN_KV_HEADS, HEAD_DIM)
        prev = jnp.pad(tb[:, :-1], ((0, 0), (1, 0), (0, 0), (0, 0), (0, 0)))
        return jnp.concatenate([prev, tb], axis=2)

    kb, vb = band_keys(k), band_keys(v)
    scores = jnp.einsum('bnqhgd,bnkhd->bnhgqk', qb, kb).astype(jnp.float32) * (HEAD_DIM ** -0.5)
    qi = jnp.arange(WINDOW)[:, None]
    kj = jnp.arange(2 * WINDOW)[None, :]
    lag = qi + WINDOW - kj
    in_band = (lag >= 0) & (lag < WINDOW)
    blk = jnp.arange(nb)[:, None, None]
    valid = in_band[None] & ((blk > 0) | (kj >= WINDOW)[None])
    scores = jnp.where(valid[None, :, None, None], scores, NEG_INF)
    sink = jnp.broadcast_to(
        sinks.astype(jnp.float32).reshape(1, 1, ATTN_KV_HEADS, ATTN_GROUP, 1, 1),
        scores.shape[:-1] + (1,))
    probs = jax.nn.softmax(jnp.concatenate([scores, sink], axis=-1), axis=-1)[..., :-1]
    out = jnp.einsum('bnhgqk,bnkhd->bnqhgd', probs.astype(v.dtype), vb)
    return out.reshape(B, S, ATTN_W)


def causal_depthwise_conv(x, w):
    return lax.conv_general_dilated(
        x, w[:, None, :].astype(x.dtype), window_strides=(1,),
        padding=((w.shape[0] - 1, 0),), dimension_numbers=('NWC', 'WIO', 'NWC'),
        feature_group_count=x.shape[-1])


def mlstm_chunkwise(q, k, v, i_pre, f_pre):
    B, S, H, D = q.shape
    L = MLSTM_CHUNK
    nc = S // L

    def chunks(t):
        return jnp.moveaxis(t.reshape((B, nc, L, H) + t.shape[3:]), 3, 1)

    q = chunks(q * (D ** -0.5))
    k = chunks(k)
    v = chunks(v)
    ig = chunks(i_pre)
    g = jnp.cumsum(chunks(jax.nn.log_sigmoid(f_pre)), axis=-1)
    g_tot = g[..., -1]

    a = g_tot[..., None] - g + ig
    m_loc = jnp.max(a, axis=-1)
    w_loc = jnp.exp(a - m_loc[..., None])
    c_loc = jnp.einsum('bhcl,bhclv,bhclk->bhcvk', w_loc, v, k)
    n_loc = jnp.einsum('bhcl,bhclk->bhck', w_loc, k)

    def carry_state(state, xs):
        c, n, m = state
        gt, ml, cl, nl = xs
        m_new = jnp.maximum(gt + m, ml)
        s_old = jnp.exp(gt + m - m_new)
        s_loc = jnp.exp(ml - m_new)
        c_new = s_old[..., None, None] * c + s_loc[..., None, None] * cl
        n_new = s_old[..., None] * n + s_loc[..., None] * nl
        return (c_new, n_new, m_new), (c, n, m)

    init = (jnp.zeros((B, H, D, D), jnp.float32), jnp.zeros((B, H, D), jnp.float32),
            jnp.full((B, H), NEG_INF, jnp.float32))
    xs = tuple(jnp.moveaxis(t, 2, 0) for t in (g_tot, m_loc, c_loc, n_loc))
    _, (c_prev, n_prev, m_prev) = lax.scan(carry_state, init, xs)
    c_prev = jnp.moveaxis(c_prev, 0, 2)
    n_prev = jnp.moveaxis(n_prev, 0, 2)
    m_prev = jnp.moveaxis(m_prev, 0, 2)

    causal = jnp.tril(jnp.ones((L, L), dtype=bool))
    d_intra = jnp.where(causal, g[..., :, None] - g[..., None, :] + ig[..., None, :], NEG_INF)
    inter = g + m_prev[..., None]
    m_row = jnp.maximum(inter, jnp.max(d_intra, axis=-1))
    w_intra = jnp.exp(d_intra - m_row[..., None])
    w_inter = jnp.exp(inter - m_row)
    s = jnp.einsum('bhcld,bhcsd->bhcls', q, k) * w_intra
    num = (jnp.einsum('bhcls,bhcsv->bhclv', s, v)
           + w_inter[..., None] * jnp.einsum('bhcvk,bhclk->bhclv', c_prev, q))
    den = jnp.sum(s, axis=-1) + w_inter * jnp.einsum('bhck,bhclk->bhcl', n_prev, q)
    h = num / jnp.maximum(jnp.abs(den), jnp.exp(-m_row))[..., None]
    return jnp.moveaxis(h, 1, 3).reshape(B, S, H, D)


def rwkv7_scan(r, decay, k, v, a_vec, b_vec):
    B, S, H, D = r.shape

    def step(state, xs):
        r_t, w_t, k_t, v_t, a_t, b_t = xs
        sa = jnp.einsum('bhvk,bhk->bhv', state, a_t)
        state = (state * w_t[:, :, None, :] + sa[..., None] * b_t[:, :, None, :]
                 + v_t[..., None] * k_t[:, :, None, :])
        return state, jnp.einsum('bhvk,bhk->bhv', state, r_t)

    xs = tuple(jnp.moveaxis(t, 1, 0) for t in (r, decay, k, v, a_vec, b_vec))
    _, y = lax.scan(step, jnp.zeros((B, H, D, D), jnp.float32), xs)
    return jnp.moveaxis(y, 0, 1)


def rwkv7_time_mix(u, mu, w0, w_up, a0, a_up, g_up, k_k, k_a, r_k, ln_w, ln_b):
    f32 = jnp.float32
    B, S, _ = u.shape
    u = u.astype(f32)
    prev = jnp.pad(u[:, :-1], ((0, 0), (1, 0), (0, 0)))
    u = u + (prev - u) * mu.astype(f32)
    r, k, v, xw, xa, xg = jnp.split(
        u, [RWKV_W, 2 * RWKV_W, 3 * RWKV_W, 3 * RWKV_W + RWKV_W_RANK,
            3 * RWKV_W + RWKV_W_RANK + RWKV_A_RANK], axis=-1)
    log_w = -jax.nn.softplus(-(w0.astype(f32) + jnp.tanh(xw) @ w_up.astype(f32))) - 0.5
    decay = jnp.exp(-jnp.exp(log_w))
    a = jax.nn.sigmoid(a0.astype(f32) + xa @ a_up.astype(f32))
    g = jax.nn.sigmoid(xg) @ g_up.astype(f32)

    def heads(t):
        return t.reshape(B, S, RWKV_HEADS, HEAD_DIM)

    kk = heads(k * k_k.astype(f32))
    kk = kk * lax.rsqrt(jnp.maximum(jnp.sum(kk * kk, axis=-1, keepdims=True), 1e-24))
    k = k * (1.0 + (a - 1.0) * k_a.astype(f32))
    r_h, k_h, v_h, a_h = heads(r), heads(k), heads(v), heads(a)
    y = rwkv7_scan(r_h, heads(decay), k_h, v_h, -kk, kk * a_h)
    mean = jnp.mean(y, axis=-1, keepdims=True)
    var = jnp.mean(jnp.square(y - mean), axis=-1, keepdims=True)
    y = ((y - mean) * lax.rsqrt(var + RWKV_GN_EPS) * ln_w.astype(f32).reshape(RWKV_HEADS, HEAD_DIM)
         + ln_b.astype(f32).reshape(RWKV_HEADS, HEAD_DIM))
    y = y + jnp.sum(r_h * k_h * r_k.astype(f32), axis=-1, keepdims=True) * v_h
    return y.reshape(B, S, RWKV_W) * g


def token_mixing(h, cos, sin, w_in, attn_sinks, mlstm_conv, mlstm_i_bias, mlstm_f_bias, mlstm_norm,
                 rwkv_mu, rwkv_w0, rwkv_w_up, rwkv_a0, rwkv_a_up, rwkv_g_up, rwkv_k_k, rwkv_k_a,
                 rwkv_r_k, rwkv_ln_w, rwkv_ln_b, w_out):
    f32 = jnp.float32
    B, S, _ = h.shape
    proj = h @ w_in
    attn_p, mlstm_p, rwkv_p = jnp.split(proj, [ATTN_COLS, ATTN_COLS + MLSTM_COLS], axis=-1)

    q_at, k_at, v_at = jnp.split(attn_p, [ATTN_W, ATTN_W + KV_W], axis=-1)
    q_at = partial_rope(q_at.reshape(B, S, ATTN_Q_HEADS, HEAD_DIM), cos, sin)
    k_at = partial_rope(k_at.reshape(B, S, ATTN_KV_HEADS, HEAD_DIM), cos, sin)
    v_at = v_at.reshape(B, S, ATTN_KV_HEADS, HEAD_DIM)
    y_attn = sliding_window_gqa(q_at, k_at, v_at, attn_sinks)

    qk_m, v_m, o_m, i_m, f_m = jnp.split(
        mlstm_p, [2 * MLSTM_W, 3 * MLSTM_W, 4 * MLSTM_W, 4 * MLSTM_W + MLSTM_HEADS], axis=-1)
    qk_m = jax.nn.silu(causal_depthwise_conv(qk_m, mlstm_conv))
    q_m, k_m = jnp.split(qk_m, 2, axis=-1)

    def m_heads(t):
        return t.astype(f32).reshape(B, S, MLSTM_HEADS, HEAD_DIM)

    i_pre = GATE_CAP * jnp.tanh((i_m.astype(f32) + mlstm_i_bias.astype(f32)) / GATE_CAP)
    f_pre = GATE_CAP * jnp.tanh((f_m.astype(f32) + mlstm_f_bias.astype(f32)) / GATE_CAP)
    h_m = mlstm_chunkwise(m_heads(q_m), m_heads(k_m), m_heads(v_m), i_pre, f_pre)
    h_m = (h_m * lax.rsqrt(jnp.mean(h_m * h_m, axis=-1, keepdims=True) + NORM_EPS)
           * mlstm_norm.astype(f32).reshape(MLSTM_HEADS, HEAD_DIM))
    y_mlstm = jax.nn.sigmoid(o_m.astype(f32)) * h_m.reshape(B, S, MLSTM_W)

    y_rwkv = rwkv7_time_mix(rwkv_p, rwkv_mu, rwkv_w0, rwkv_w_up, rwkv_a0, rwkv_a_up, rwkv_g_up,
                            rwkv_k_k, rwkv_k_a, rwkv_r_k, rwkv_ln_w, rwkv_ln_b)

    y = jnp.concatenate([y_attn, y_mlstm.astype(h.dtype), y_rwkv.astype(h.dtype)], axis=-1)
    return y @ w_out


def setup_inputs(seed: int = 0) -> dict:
    key = jax.random.key(seed)
    ks = iter(jax.random.split(key, 48))
    f32 = jnp.float32

    def normal(shape, scale):
        return scale * jax.random.normal(next(ks), shape, f32)

    def uniform(shape, lo, hi):
        return jax.random.uniform(next(ks), shape, f32, lo, hi)

    def gain(width):
        return 1.0 + normal((DEPTH, width), 0.05)

    x = normal((BATCH, SEQ, D_MODEL), 1.0)
    p = normal((DEPTH, BATCH, SEQ, D_PLE), 1.0)
    start = jax.random.randint(next(ks), (BATCH, 1), 0, 4096, jnp.int32)
    positions = start + jnp.arange(SEQ, dtype=jnp.int32)[None, :]
    return {
        'x': x,
        'p': p,
        'positions': positions,
        'ln_ffn1_pre': gain(D_MODEL),
        'ln_ffn1_post': gain(D_MODEL),
        'w_ffn1_in': normal((DEPTH, D_MODEL, 2 * D_FF), D_MODEL ** -0.5),
        'w_ffn1_out': normal((DEPTH, D_FF, D_MODEL), D_FF ** -0.5),
        'ln_mix_pre': gain(D_MODEL),
        'w_in': normal((DEPTH, D_MODEL, D_IN), D_MODEL ** -0.5),
        'attn_sinks': normal((DEPTH, ATTN_Q_HEADS), 0.5),
        'mlstm_conv': normal((DEPTH, MLSTM_CONV, 2 * MLSTM_W), MLSTM_CONV ** -0.5),
        'mlstm_i_bias': normal((DEPTH, MLSTM_HEADS), 0.1),
        'mlstm_f_bias': uniform((DEPTH, MLSTM_HEADS), 3.0, 6.0),
        'mlstm_norm': gain(MLSTM_W),
        'rwkv_mu': uniform((DEPTH, RWKV_COLS), 0.0, 1.0),
        'rwkv_w0': uniform((DEPTH, RWKV_W), -1.5, 0.5),
        'rwkv_w_up': normal((DEPTH, RWKV_W_RANK, RWKV_W), 0.1),
        'rwkv_a0': normal((DEPTH, RWKV_W), 0.1),
        'rwkv_a_up': normal((DEPTH, RWKV_A_RANK, RWKV_W), 0.5 * RWKV_A_RANK ** -0.5),
        'rwkv_g_up': normal((DEPTH, RWKV_G_RANK, RWKV_W), RWKV_G_RANK ** -0.5),
        'rwkv_k_k': 0.85 + normal((DEPTH, RWKV_W), 0.05),
        'rwkv_k_a': 1.0 + normal((DEPTH, RWKV_W), 0.05),
        'rwkv_r_k': normal((DEPTH, RWKV_HEADS, HEAD_DIM), 0.1),
        'rwkv_ln_w': gain(RWKV_W),
        'rwkv_ln_b': normal((DEPTH, RWKV_W), 0.01),
        'w_out': normal((DEPTH, D_MIX, D_MODEL), D_MIX ** -0.5),
        'ln_mix_post': gain(D_MODEL),
        'ln_ffn2_pre': gain(D_MODEL),
        'ln_ffn2_post': gain(D_MODEL),
        'w_ffn2_in': normal((DEPTH, D_MODEL, 2 * D_FF), D_MODEL ** -0.5),
        'w_ffn2_out': normal((DEPTH, D_FF, D_MODEL), D_FF ** -0.5),
        'ln_ple_pre': gain(D_MODEL),
        'w_ple_gate': normal((DEPTH, D_MODEL, D_MODEL), D_MODEL ** -0.5),
        'w_ple_proj': normal((DEPTH, D_PLE, D_MODEL), D_PLE ** -0.5),
        'ln_ple_post': gain(D_MODEL),
    }


def reference(x, p, positions, ln_ffn1_pre, ln_ffn1_post, w_ffn1_in, w_ffn1_out, ln_mix_pre, w_in,
              attn_sinks, mlstm_conv, mlstm_i_bias, mlstm_f_bias, mlstm_norm, rwkv_mu, rwkv_w0,
              rwkv_w_up, rwkv_a0, rwkv_a_up, rwkv_g_up, rwkv_k_k, rwkv_k_a, rwkv_r_k, rwkv_ln_w,
              rwkv_ln_b, w_out, ln_mix_post, ln_ffn2_pre, ln_ffn2_post, w_ffn2_in, w_ffn2_out,
              ln_ple_pre, w_ple_gate, w_ple_proj, ln_ple_post):
    cos, sin = rope_tables(positions)
    for l in range(DEPTH):
        x = x + 0.5 * rms_norm(swiglu(rms_norm(x, ln_ffn1_pre[l]), w_ffn1_in[l], w_ffn1_out[l]),
                               ln_ffn1_post[l])
        mix = token_mixing(rms_norm(x, ln_mix_pre[l]), cos, sin, w_in[l], attn_sinks[l],
                           mlstm_conv[l], mlstm_i_bias[l], mlstm_f_bias[l], mlstm_norm[l],
                           rwkv_mu[l], rwkv_w0[l], rwkv_w_up[l], rwkv_a0[l], rwkv_a_up[l],
                           rwkv_g_up[l], rwkv_k_k[l], rwkv_k_a[l], rwkv_r_k[l], rwkv_ln_w[l],
                           rwkv_ln_b[l], w_out[l])
        x = x + rms_norm(mix, ln_mix_post[l])
        x = x + 0.5 * rms_norm(swiglu(rms_norm(x, ln_ffn2_pre[l]), w_ffn2_in[l], w_ffn2_out[l]),
                               ln_ffn2_post[l])
        gate = jax.nn.sigmoid(rms_norm(x, ln_ple_pre[l]) @ w_ple_gate[l])
        x = x + rms_norm(gate * (p[l] @ w_ple_proj[l]), ln_ple_post[l])
    return x
```

```python
import functools

import jax
import jax.numpy as jnp
from jax import lax
from jax.experimental import pallas as pl
from jax.experimental.pallas import tpu as pltpu

F32 = jnp.float32
BF16 = jnp.bfloat16
HIGHEST = lax.Precision.HIGHEST

D_MODEL = 1024
HEAD_DIM = 64
D_FF = 2816
D_PLE = 256
ATTN_Q_HEADS = 8
ATTN_KV_HEADS = 2
ATTN_GROUP = ATTN_Q_HEADS // ATTN_KV_HEADS
WINDOW = 128
ROPE_THETA = 500000.0
ROPE_DIM = HEAD_DIM // 4
ROPE_HALF = ROPE_DIM // 2
MLSTM_HEADS = 4
MLSTM_CONV = 4
GATE_CAP = 15.0
RWKV_HEADS = 4
RWKV_W_RANK = 64
RWKV_A_RANK = 64
RWKV_G_RANK = 128
RWKV_GN_EPS = 64e-5
NORM_EPS = 1e-6
NEG_INF = -1e30

ATTN_W = ATTN_Q_HEADS * HEAD_DIM
KV_W = ATTN_KV_HEADS * HEAD_DIM
MLSTM_W = MLSTM_HEADS * HEAD_DIM
RWKV_W = RWKV_HEADS * HEAD_DIM
ATTN_COLS = ATTN_W + 2 * KV_W
MLSTM_MAIN = 4 * MLSTM_W
MLSTM_GATES = 2 * MLSTM_HEADS
RWKV_COLS = 3 * RWKV_W + RWKV_W_RANK + RWKV_A_RANK + RWKV_G_RANK

CHUNK = 64
NEUMANN_STEPS = 5
FF_CHUNK = 1408
SUBLANES = 8
V7X_VMEM_LIMIT = 56 * 1024 * 1024


def _params(semantics):
    return pltpu.CompilerParams(dimension_semantics=semantics, vmem_limit_bytes=V7X_VMEM_LIMIT)


def _resident(block_shape, index_map):
    return pl.BlockSpec(block_shape, index_map, pipeline_mode=pl.Buffered(1))


def _layer_vec(width, layer):
    return pl.BlockSpec((None, 1, width), lambda i: (layer, 0, 0))


def _rms(x, gain):
    return x * lax.rsqrt(jnp.mean(x * x, axis=-1, keepdims=True) + NORM_EPS) * gain


def _mm(a, b):
    return jnp.dot(a.astype(BF16), b.astype(BF16), preferred_element_type=F32)


def _mm_nt(a, b):
    return lax.dot_general(a.astype(BF16), b.astype(BF16), (((1,), (1,)), ((), ())),
                           preferred_element_type=F32)


def _mm_tn(a, b):
    return lax.dot_general(a.astype(BF16), b.astype(BF16), (((0,), (0,)), ((), ())),
                           preferred_element_type=F32)


def _cumsum_rows(tri, x):
    return jnp.dot(tri, x, precision=HIGHEST, preferred_element_type=F32)


def _tri_masks(n):
    row = lax.broadcasted_iota(jnp.int32, (n, n), 0)
    col = lax.broadcasted_iota(jnp.int32, (n, n), 1)
    return row, col


def _ffn_body(x_ref, gpre_ref, gpost_ref, win_ref, wout_ref, o_ref):
    x = x_ref[...]
    xn = _rms(x, gpre_ref[...]).astype(BF16)
    acc = None
    for lo in range(0, D_FF, FF_CHUNK):
        gate = jnp.dot(xn, win_ref[:, lo:lo + FF_CHUNK], preferred_element_type=F32)
        up = jnp.dot(xn, win_ref[:, D_FF + lo:D_FF + lo + FF_CHUNK], preferred_element_type=F32)
        act = (gate * jax.nn.sigmoid(gate) * up).astype(BF16)
        part = jnp.dot(act, wout_ref[lo:lo + FF_CHUNK, :], preferred_element_type=F32)
        acc = part if acc is None else acc + part
    o_ref[...] = x + 0.5 * _rms(acc, gpost_ref[...])


def _ffn(x, gpre, gpost, w_in, w_out, layer, tm):
    s = x.shape[0]
    row = lambda i: (i, 0)
    lay3 = lambda i: (layer, 0, 0)
    return pl.pallas_call(
        _ffn_body,
        out_shape=jax.ShapeDtypeStruct((s, D_MODEL), F32),
        grid=(s // tm,),
        in_specs=[
            pl.BlockSpec((tm, D_MODEL), row),
            _layer_vec(D_MODEL, layer),
            _layer_vec(D_MODEL, layer),
            _resident((None, D_MODEL, 2 * D_FF), lay3),
            _resident((None, D_FF, D_MODEL), lay3),
        ],
        out_specs=pl.BlockSpec((tm, D_MODEL), row),
        compiler_params=_params(("parallel",)),
        name="ffn",
    )(x, gpre, gpost, w_in, w_out)


def _proj_body(x_ref, g_ref, wa_ref, wm_ref, wg_ref, wr_ref, oa_ref, om_ref, og_ref, or_ref):
    h = _rms(x_ref[...], g_ref[...]).astype(BF16)
    oa_ref[...] = jnp.dot(h, wa_ref[...], preferred_element_type=F32)
    om_ref[...] = jnp.dot(h, wm_ref[...], preferred_element_type=F32)
    og_ref[...] = jnp.dot(h, wg_ref[...], preferred_element_type=F32)
    or_ref[...] = jnp.dot(h, wr_ref[...], preferred_element_type=F32)


def _mix_proj(x, gain, wa, wm, wg, wr, layer, tm):
    s = x.shape[0]
    row = lambda i: (i, 0)
    lay3 = lambda i: (layer, 0, 0)
    widths = (ATTN_COLS, MLSTM_MAIN, MLSTM_GATES, RWKV_COLS)
    return pl.pallas_call(
        _proj_body,
        out_shape=[jax.ShapeDtypeStruct((s, w), F32) for w in widths],
        grid=(s // tm,),
        in_specs=[pl.BlockSpec((tm, D_MODEL), row), _layer_vec(D_MODEL, layer)]
        + [_resident((None, D_MODEL, w), lay3) for w in widths],
        out_specs=[pl.BlockSpec((tm, w), row) for w in widths],
        compiler_params=_params(("parallel",)),
        name="mix_proj",
    )(x, gain, wa, wm, wg, wr)


def _out_body(ya_ref, ym_ref, yr_ref, x_ref, w_ref, g_ref, o_ref):
    mix = jnp.dot(ya_ref[...], w_ref[0:ATTN_W, :], preferred_element_type=F32)
    mix += jnp.dot(ym_ref[...], w_ref[ATTN_W:ATTN_W + MLSTM_W, :], preferred_element_type=F32)
    mix += jnp.dot(yr_ref[...], w_ref[ATTN_W + MLSTM_W:, :], preferred_element_type=F32)
    o_ref[...] = x_ref[...] + _rms(mix, g_ref[...])


def _out_proj(ya, ym, yr, x, w_out, gain, layer, tm):
    s = x.shape[0]
    row = lambda i: (i, 0)
    return pl.pallas_call(
        _out_body,
        out_shape=jax.ShapeDtypeStruct((s, D_MODEL), F32),
        grid=(s // tm,),
        in_specs=[
            pl.BlockSpec((tm, ATTN_W), row),
            pl.BlockSpec((tm, MLSTM_W), row),
            pl.BlockSpec((tm, RWKV_W), row),
            pl.BlockSpec((tm, D_MODEL), row),
            _resident((None, D_MODEL, D_MODEL), lambda i: (layer, 0, 0)),
            _layer_vec(D_MODEL, layer),
        ],
        out_specs=pl.BlockSpec((tm, D_MODEL), row),
        compiler_params=_params(("parallel",)),
        name="out_proj",
    )(ya, ym, yr, x, w_out, gain)


def _ple_body(x_ref, p_ref, gpre_ref, gpost_ref, wg_ref, wp_ref, o_ref):
    x = x_ref[...]
    gate = jax.nn.sigmoid(jnp.dot(_rms(x, gpre_ref[...]).astype(BF16), wg_ref[...],
                                  preferred_element_type=F32))
    emb = jnp.dot(p_ref[...].astype(BF16), wp_ref[...], preferred_element_type=F32)
    o_ref[...] = x + _rms(gate * emb, gpost_ref[...])


def _ple(x, p, gpre, gpost, w_gate, w_proj, layer, tm):
    s = x.shape[0]
    row = lambda i: (i, 0)
    lay3 = lambda i: (layer, 0, 0)
    return pl.pallas_call(
        _ple_body,
        out_shape=jax.ShapeDtypeStruct((s, D_MODEL), F32),
        grid=(s // tm,),
        in_specs=[
            pl.BlockSpec((tm, D_MODEL), row),
            pl.BlockSpec((None, tm, D_PLE), lambda i: (layer, i, 0)),
            _layer_vec(D_MODEL, layer),
            _layer_vec(D_MODEL, layer),
            _resident((None, D_MODEL, D_MODEL), lay3),
            _resident((None, D_PLE, D_MODEL), lay3),
        ],
        out_specs=pl.BlockSpec((tm, D_MODEL), row),
        compiler_params=_params(("parallel",)),
        name="ple",
    )(x, p, gpre, gpost, w_gate, w_proj)


def _rope_body(pos_ref, invf_ref, cos_ref, sa_ref, sb_ref):
    ang = pos_ref[...].astype(F32) * invf_ref[...]
    sin = jnp.sin(ang)
    dim = lax.broadcasted_iota(jnp.int32, ang.shape, 1) & (HEAD_DIM - 1)
    cos_ref[...] = jnp.cos(ang)
    sa_ref[...] = jnp.where(dim < ROPE_HALF, -sin, 0.0)
    sb_ref[...] = jnp.where((dim >= ROPE_HALF) & (dim < ROPE_DIM), sin, 0.0)


def _rope_tables(positions, tm):
    s = positions.shape[0]
    lane = jnp.arange(2 * HEAD_DIM) % HEAD_DIM
    freq = ROPE_THETA ** (-jnp.arange(0, ROPE_DIM, 2, dtype=F32) / ROPE_DIM)
    invf = jnp.where(lane < ROPE_DIM, freq[lane % ROPE_HALF], 0.0).astype(F32)[None, :]
    row = lambda i: (i, 0)
    out = jax.ShapeDtypeStruct((s, 2 * HEAD_DIM), F32)
    return pl.pallas_call(
        _rope_body,
        out_shape=[out, out, out],
        grid=(s // tm,),
        in_specs=[pl.BlockSpec((tm, 1), row), pl.BlockSpec((1, 2 * HEAD_DIM), lambda i: (0, 0))],
        out_specs=[pl.BlockSpec((tm, 2 * HEAD_DIM), row)] * 3,
        compiler_params=_params(("parallel",)),
        name="rope_tables",
    )(positions, invf)


def _attn_body(sinks_ref, p_ref, cos_ref, sa_ref, sb_ref, o_ref, kprev_ref, vprev_ref):
    step = pl.program_id(0)

    @pl.when(step == 0)
    def _():
        kprev_ref[...] = jnp.zeros_like(kprev_ref)
        vprev_ref[...] = jnp.zeros_like(vprev_ref)

    cos, sin_a, sin_b = cos_ref[...], sa_ref[...], sb_ref[...]

    def rope(x):
        width = x.shape[1]
        reps = width // cos.shape[1]
        tile = (lambda t: jnp.concatenate([t] * reps, axis=1)) if reps > 1 else (lambda t: t)
        return (x * tile(cos) + pltpu.roll(x, width - ROPE_HALF, 1) * tile(sin_a)
                + pltpu.roll(x, ROPE_HALF, 1) * tile(sin_b))

    q = rope(p_ref[:, 0:ATTN_W])
    k_cur = rope(p_ref[:, ATTN_W:ATTN_W + KV_W]).astype(BF16)
    v_cur = p_ref[:, ATTN_W + KV_W:ATTN_COLS].astype(BF16)
    k_prev, v_prev = kprev_ref[...], vprev_ref[...]

    rows = ATTN_GROUP * WINDOW
    t = lax.broadcasted_iota(jnp.int32, (rows, WINDOW), 0) & (WINDOW - 1)
    j = lax.broadcasted_iota(jnp.int32, (rows, WINDOW), 1)
    head_of_row = lax.broadcasted_iota(jnp.int32, (rows, 1), 0) >> (WINDOW.bit_length() - 1)
    cur_ok = j <= t
    prev_ok = (j > t) & (step > 0)

    outs = []
    for g in range(ATTN_KV_HEADS):
        heads = range(g * ATTN_GROUP, (g + 1) * ATTN_GROUP)
        qs = jnp.concatenate([q[:, h * HEAD_DIM:(h + 1) * HEAD_DIM] for h in heads], axis=0)
        lanes = slice(g * HEAD_DIM, (g + 1) * HEAD_DIM)
        s_cur = jnp.where(cur_ok, _mm_nt(qs, k_cur[:, lanes]) * (HEAD_DIM ** -0.5), NEG_INF)
        s_prev = jnp.where(prev_ok, _mm_nt(qs, k_prev[:, lanes]) * (HEAD_DIM ** -0.5), NEG_INF)
        sink = jnp.zeros((rows, 1), F32)
        for n, h in enumerate(heads):
            sink = jnp.where(head_of_row == n, sinks_ref[h], sink)
        m = jnp.maximum(jnp.maximum(jnp.max(s_cur, axis=-1, keepdims=True),
                                    jnp.max(s_prev, axis=-1, keepdims=True)), sink)
        p_cur = jnp.exp(s_cur - m)
        p_prev = jnp.exp(s_prev - m)
        den = (jnp.sum(p_cur, axis=-1, keepdims=True) + jnp.sum(p_prev, axis=-1, keepdims=True)
               + jnp.exp(sink - m))
        o = (_mm(p_cur, v_cur[:, lanes]) + _mm(p_prev, v_prev[:, lanes])) / den
        outs += [o[n * WINDOW:(n + 1) * WINDOW, :] for n in range(ATTN_GROUP)]
    o_ref[...] = jnp.concatenate(outs, axis=1).astype(o_ref.dtype)
    kprev_ref[...] = k_cur
    vprev_ref[...] = v_cur


def _attention(sinks, attn_p, cos, sin_a, sin_b, layer):
    s = attn_p.shape[0]
    row = lambda i: (i, 0)
    return pl.pallas_call(
        _attn_body,
        out_shape=jax.ShapeDtypeStruct((s, ATTN_W), BF16),
        grid=(s // WINDOW,),
        in_specs=[
            pl.BlockSpec(memory_space=pltpu.SMEM),
            pl.BlockSpec((WINDOW, ATTN_COLS), row),
            pl.BlockSpec((WINDOW, 2 * HEAD_DIM), row),
            pl.BlockSpec((WINDOW, 2 * HEAD_DIM), row),
            pl.BlockSpec((WINDOW, 2 * HEAD_DIM), row),
        ],
        out_specs=pl.BlockSpec((WINDOW, ATTN_W), row),
        scratch_shapes=[pltpu.VMEM((WINDOW, KV_W), BF16), pltpu.VMEM((WINDOW, KV_W), BF16)],
        compiler_params=_params(("arbitrary",)),
        name="swa_attention",
    )(sinks[layer], attn_p, cos, sin_a, sin_b)


def _mlstm_body(p_ref, gates_ref, conv_ref, bias_ref, norm_ref, o_ref, xpad_ref, c_ref, n_ref, *, tb):
    step = pl.program_id(0)

    @pl.when(step == 0)
    def _():
        xpad_ref[0:SUBLANES, :] = jnp.zeros((SUBLANES, 2 * MLSTM_W), F32)
        c_ref[...] = jnp.zeros_like(c_ref)
        n_ref[...] = jnp.zeros_like(n_ref)

    xpad_ref[SUBLANES:SUBLANES + tb, :] = p_ref[:, 0:2 * MLSTM_W]
    conv = None
    for tap in range(MLSTM_CONV):
        shifted = xpad_ref[pl.ds(SUBLANES - (MLSTM_CONV - 1) + tap, tb), :]
        term = shifted * conv_ref[tap:tap + 1, :]
        conv = term if conv is None else conv + term
    xpad_ref[0:SUBLANES, :] = xpad_ref[tb:tb + SUBLANES, :]
    qk = conv * jax.nn.sigmoid(conv)
    q_all = qk[:, 0:MLSTM_W] * (HEAD_DIM ** -0.5)
    k_all = qk[:, MLSTM_W:2 * MLSTM_W]

    pre = GATE_CAP * jnp.tanh((gates_ref[...] + bias_ref[...]) / GATE_CAP)
    logsig = jnp.minimum(pre, 0.0) - jnp.log(1.0 + jnp.exp(-jnp.abs(pre)))

    row, col = _tri_masks(CHUNK)
    tri = (col <= row).astype(F32)
    causal = col <= row
    eye = col == row

    def to_row(column):
        return jnp.sum(jnp.where(eye, column, 0.0), axis=0, keepdims=True)

    c_state = [c_ref[h] for h in range(MLSTM_HEADS)]
    n_state = [n_ref[h:h + 1, :] for h in range(MLSTM_HEADS)]
    for c0 in range(0, tb, CHUNK):
        rows = slice(c0, c0 + CHUNK)
        g_cum = _cumsum_rows(tri, logsig[rows, :])
        outs = []
        for h in range(MLSTM_HEADS):
            lanes = slice(h * HEAD_DIM, (h + 1) * HEAD_DIM)
            g_col = g_cum[:, MLSTM_HEADS + h:MLSTM_HEADS + h + 1]
            i_col = pre[rows, h:h + 1]
            g_tot = g_col[CHUNK - 1:CHUNK, :]
            q, k = q_all[rows, lanes], k_all[rows, lanes]
            v = p_ref[rows, 2 * MLSTM_W + h * HEAD_DIM:2 * MLSTM_W + (h + 1) * HEAD_DIM]
            o_gate = p_ref[rows, 3 * MLSTM_W + h * HEAD_DIM:3 * MLSTM_W + (h + 1) * HEAD_DIM]
            decay = jnp.exp(jnp.where(causal, g_col - to_row(g_col) + to_row(i_col), NEG_INF))
            s_mat = _mm_nt(q, k) * decay
            e_g = jnp.exp(g_col)
            num = _mm(s_mat, v) + e_g * _mm_nt(q, c_state[h])
            den = (jnp.sum(s_mat, axis=-1, keepdims=True)
                   + e_g * jnp.sum(q * n_state[h], axis=-1, keepdims=True))
            hid = num / jnp.maximum(jnp.abs(den), 1.0)
            hid = hid * lax.rsqrt(jnp.mean(hid * hid, axis=-1, keepdims=True) + NORM_EPS) * norm_ref[:, lanes]
            outs.append(jax.nn.sigmoid(o_gate) * hid)
            w_in = jnp.exp(g_tot - g_col + i_col)
            e_tot = jnp.exp(g_tot)
            c_state[h] = e_tot * c_state[h] + _mm_tn(w_in * v, k)
            n_state[h] = e_tot * n_state[h] + jnp.sum(w_in * k, axis=0, keepdims=True)
        o_ref[rows, :] = jnp.concatenate(outs, axis=1).astype(o_ref.dtype)
    for h in range(MLSTM_HEADS):
        c_ref[h] = c_state[h]
        n_ref[h:h + 1, :] = n_state[h]


def _mlstm(mlstm_p, gates, conv_w, bias, norm, layer, tb):
    s = mlstm_p.shape[0]
    row = lambda i: (i, 0)
    return pl.pallas_call(
        functools.partial(_mlstm_body, tb=tb),
        out_shape=jax.ShapeDtypeStruct((s, MLSTM_W), BF16),
        grid=(s // tb,),
        in_specs=[
            pl.BlockSpec((tb, MLSTM_MAIN), row),
            pl.BlockSpec((tb, MLSTM_GATES), row),
            pl.BlockSpec((None, MLSTM_CONV, 2 * MLSTM_W), lambda i: (layer, 0, 0)),
            _layer_vec(MLSTM_GATES, layer),
            _layer_vec(MLSTM_W, layer),
        ],
        out_specs=pl.BlockSpec((tb, MLSTM_W), row),
        scratch_shapes=[
            pltpu.VMEM((tb + SUBLANES, 2 * MLSTM_W), F32),
            pltpu.VMEM((MLSTM_HEADS, HEAD_DIM, HEAD_DIM), F32),
            pltpu.VMEM((SUBLANES, HEAD_DIM), F32),
        ],
        compiler_params=_params(("arbitrary",)),
        name="mlstm",
    )(mlstm_p, gates, conv_w, bias, norm)


def _rwkv_chunk(r, ld, k, v, a_vec, b_vec, state, tri, strict, incl):
    lp = _cumsum_rows(tri, ld)
    lp_last = lp[CHUNK - 1:CHUNK, :]
    grow = jnp.exp(-lp)
    a_t = a_vec * jnp.exp(lp - ld)
    b_t = b_vec * grow
    k_t = k * grow
    r_t = r * jnp.exp(lp)
    to_end = jnp.exp(lp_last - lp)
    b_end = b_vec * to_end
    k_end = k * to_end

    n_mat = jnp.where(strict, _mm_nt(a_t, b_t), 0.0)
    a_ak = jnp.where(strict, _mm_nt(a_t, k_t), 0.0)
    c_rb = jnp.where(incl, _mm_nt(r_t, b_t), 0.0)
    c_rk = jnp.where(incl, _mm_nt(r_t, k_t), 0.0)

    inv = jnp.where(incl & ~strict, 1.0, 0.0) + n_mat
    power = n_mat
    for _ in range(NEUMANN_STEPS):
        power = _mm(power, power)
        inv = inv + _mm(inv, power)

    w1 = _mm(inv, a_t)
    w2 = _mm(inv, _mm(a_ak, v))
    q_eff = r_t + _mm(c_rb, w1)
    y = _mm_nt(q_eff, state) + _mm(c_rb, w2) + _mm(c_rk, v)
    new_state = (state * jnp.exp(lp_last) + _mm(state, _mm_tn(w1, b_end))
                 + _mm_tn(w2, b_end) + _mm_tn(v, k_end))
    return y, new_state


def _rwkv_body(p_ref, mu_ref, w0_ref, wup_ref, a0_ref, aup_ref, gup_ref, kk_ref, ka_ref, rk_ref,
               lnw_ref, lnb_ref, o_ref, upad_ref, s_ref, *, tb):
    step = pl.program_id(0)

    @pl.when(step == 0)
    def _():
        upad_ref[0:SUBLANES, :] = jnp.zeros((SUBLANES, RWKV_COLS), F32)
        s_ref[...] = jnp.zeros_like(s_ref)

    u = p_ref[...]
    upad_ref[SUBLANES:SUBLANES + tb, :] = u
    prev = upad_ref[pl.ds(SUBLANES - 1, tb), :]
    upad_ref[0:SUBLANES, :] = upad_ref[tb:tb + SUBLANES, :]
    u = u + (prev - u) * mu_ref[...]

    r_all = u[:, 0:RWKV_W]
    k_raw = u[:, RWKV_W:2 * RWKV_W]
    v_all = u[:, 2 * RWKV_W:3 * RWKV_W]
    x_w = u[:, 3 * RWKV_W:3 * RWKV_W + RWKV_W_RANK]
    x_a = u[:, 3 * RWKV_W + RWKV_W_RANK:3 * RWKV_W + RWKV_W_RANK + RWKV_A_RANK]
    x_g = u[:, 3 * RWKV_W + RWKV_W_RANK + RWKV_A_RANK:RWKV_COLS]

    z = w0_ref[...] + _mm(jnp.tanh(x_w), wup_ref[...])
    ld_all = -jnp.exp(-0.5) * jax.nn.sigmoid(z)
    a_all = jax.nn.sigmoid(a0_ref[...] + _mm(x_a, aup_ref[...]))
    g_all = _mm(jax.nn.sigmoid(x_g), gup_ref[...])
    kk_all = k_raw * kk_ref[...]
    k_all = k_raw * (1.0 + (a_all - 1.0) * ka_ref[...])

    row, col = _tri_masks(CHUNK)
    incl = col <= row
    strict = col < row
    tri = incl.astype(F32)

    state = [s_ref[h] for h in range(RWKV_HEADS)]
    for c0 in range(0, tb, CHUNK):
        rows = slice(c0, c0 + CHUNK)
        outs = []
        for h in range(RWKV_HEADS):
            lanes = slice(h * HEAD_DIM, (h + 1) * HEAD_DIM)
            r, k, v = r_all[rows, lanes], k_all[rows, lanes], v_all[rows, lanes]
            kk = kk_all[rows, lanes]
            kk = kk * lax.rsqrt(jnp.maximum(jnp.sum(kk * kk, axis=-1, keepdims=True), 1e-24))
            y, state[h] = _rwkv_chunk(r, ld_all[rows, lanes], k, v, -kk, kk * a_all[rows, lanes],
                                      state[h], tri, strict, incl)
            mean = jnp.mean(y, axis=-1, keepdims=True)
            var = jnp.mean(jnp.square(y - mean), axis=-1, keepdims=True)
            y = (y - mean) * lax.rsqrt(var + RWKV_GN_EPS) * lnw_ref[:, lanes] + lnb_ref[:, lanes]
            y = y + jnp.sum(r * k * rk_ref[:, lanes], axis=-1, keepdims=True) * v
            outs.append(y * g_all[rows, lanes])
        o_ref[rows, :] = jnp.concatenate(outs, axis=1).astype(o_ref.dtype)
    for h in range(RWKV_HEADS):
        s_ref[h] = state[h]


def _rwkv(rwkv_p, mu, w0, w_up, a0, a_up, g_up, k_k, k_a, r_k, ln_w, ln_b, layer, tb):
    s = rwkv_p.shape[0]
    row = lambda i: (i, 0)
    lay3 = lambda i: (layer, 0, 0)
    vec = lambda width: _layer_vec(width, layer)
    return pl.pallas_call(
        functools.partial(_rwkv_body, tb=tb),
        out_shape=jax.ShapeDtypeStruct((s, RWKV_W), BF16),
        grid=(s // tb,),
        in_specs=[
            pl.BlockSpec((tb, RWKV_COLS), row),
            vec(RWKV_COLS), vec(RWKV_W),
            pl.BlockSpec((None, RWKV_W_RANK, RWKV_W), lay3),
            vec(RWKV_W),
            pl.BlockSpec((None, RWKV_A_RANK, RWKV_W), lay3),
            pl.BlockSpec((None, RWKV_G_RANK, RWKV_W), lay3),
            vec(RWKV_W), vec(RWKV_W), vec(RWKV_W), vec(RWKV_W), vec(RWKV_W),
        ],
        out_specs=pl.BlockSpec((tb, RWKV_W), row),
        scratch_shapes=[
            pltpu.VMEM((tb + SUBLANES, RWKV_COLS), F32),
            pltpu.VMEM((RWKV_HEADS, HEAD_DIM, HEAD_DIM), F32),
        ],
        compiler_params=_params(("arbitrary",)),
        name="rwkv7",
    )(rwkv_p, mu, w0, w_up, a0, a_up, g_up, k_k, k_a, r_k, ln_w, ln_b)


def _tiles(s):
    return min(512, s), min(256, s)


def kernel(x, p, positions, ln_ffn1_pre, ln_ffn1_post, w_ffn1_in, w_ffn1_out, ln_mix_pre, w_in, attn_sinks, mlstm_conv, mlstm_i_bias, mlstm_f_bias, mlstm_norm, rwkv_mu, rwkv_w0, rwkv_w_up, rwkv_a0, rwkv_a_up, rwkv_g_up, rwkv_k_k, rwkv_k_a, rwkv_r_k, rwkv_ln_w, rwkv_ln_b, w_out, ln_mix_post, ln_ffn2_pre, ln_ffn2_post, w_ffn2_in, w_ffn2_out, ln_ple_pre, w_ple_gate, w_ple_proj, ln_ple_post):
    batch, seq, _ = x.shape
    assert batch == 1 and seq % WINDOW == 0
    depth = w_in.shape[0]
    tm, tb = _tiles(seq)

    bf = lambda w: w.astype(BF16)
    w_ffn1_in, w_ffn1_out, w_ffn2_in, w_ffn2_out = map(bf, (w_ffn1_in, w_ffn1_out, w_ffn2_in, w_ffn2_out))
    w_out, w_ple_gate, w_ple_proj = map(bf, (w_out, w_ple_gate, w_ple_proj))
    rwkv_w_up, rwkv_a_up, rwkv_g_up = map(bf, (rwkv_w_up, rwkv_a_up, rwkv_g_up))
    w_in = bf(w_in)
    m0 = ATTN_COLS
    g0 = m0 + MLSTM_MAIN
    r0 = g0 + MLSTM_GATES
    w_attn, w_mlstm, w_gates, w_rwkv = w_in[:, :, :m0], w_in[:, :, m0:g0], w_in[:, :, g0:r0], w_in[:, :, r0:]
    vec = lambda a: a.reshape(depth, 1, -1)
    gate_bias = vec(jnp.concatenate([mlstm_i_bias, mlstm_f_bias], axis=-1))
    (ln_ffn1_pre, ln_ffn1_post, ln_mix_pre, ln_mix_post, ln_ffn2_pre, ln_ffn2_post, ln_ple_pre,
     ln_ple_post, mlstm_norm, rwkv_mu, rwkv_w0, rwkv_a0, rwkv_k_k, rwkv_k_a, rwkv_r_k, rwkv_ln_w,
     rwkv_ln_b) = map(vec, (
         ln_ffn1_pre, ln_ffn1_post, ln_mix_pre, ln_mix_post, ln_ffn2_pre, ln_ffn2_post, ln_ple_pre,
         ln_ple_post, mlstm_norm, rwkv_mu, rwkv_w0, rwkv_a0, rwkv_k_k, rwkv_k_a, rwkv_r_k, rwkv_ln_w,
         rwkv_ln_b))

    cos, sin_a, sin_b = _rope_tables(positions.reshape(seq, 1), tm)
    xs = x.reshape(seq, D_MODEL)
    for l in range(depth):
        xs = _ffn(xs, ln_ffn1_pre, ln_ffn1_post, w_ffn1_in, w_ffn1_out, l, tm)
        attn_p, mlstm_p, gates, rwkv_p = _mix_proj(xs, ln_mix_pre, w_attn, w_mlstm, w_gates, w_rwkv, l, tm)
        y_attn = _attention(attn_sinks, attn_p, cos, sin_a, sin_b, l)
        y_mlstm = _mlstm(mlstm_p, gates, mlstm_conv, gate_bias, mlstm_norm, l, tb)
        y_rwkv = _rwkv(rwkv_p, rwkv_mu, rwkv_w0, rwkv_w_up, rwkv_a0, rwkv_a_up, rwkv_g_up,
                       rwkv_k_k, rwkv_k_a, rwkv_r_k, rwkv_ln_w, rwkv_ln_b, l, tb)
        xs = _out_proj(y_attn, y_mlstm, y_rwkv, xs, w_out, ln_mix_post, l, tm)
        xs = _ffn(xs, ln_ffn2_pre, ln_ffn2_post, w_ffn2_in, w_ffn2_out, l, tm)
        xs = _ple(xs, p.reshape(depth, seq, D_PLE), ln_ple_pre, ln_ple_post, w_ple_gate, w_ple_proj, l, tm)
    return xs.reshape(batch, seq, D_MODEL)
```

```python
import functools

import jax
import jax.numpy as jnp
from jax import lax
from jax.experimental import pallas as pl
from jax.experimental.pallas import tpu as pltpu

F32 = jnp.float32
BF16 = jnp.bfloat16

D_MODEL = 1024
HEAD_DIM = 64
D_FF = 2816
D_PLE = 256
ATTN_Q_HEADS = 8
ATTN_KV_HEADS = 2
ATTN_GROUP = ATTN_Q_HEADS // ATTN_KV_HEADS
WINDOW = 128
ROPE_THETA = 500000.0
ROPE_DIM = HEAD_DIM // 4
ROPE_HALF = ROPE_DIM // 2
MLSTM_HEADS = 4
MLSTM_CONV = 4
GATE_CAP = 15.0
RWKV_HEADS = 4
RWKV_W_RANK = 64
RWKV_A_RANK = 64
RWKV_G_RANK = 128
RWKV_GN_EPS = 64e-5
NORM_EPS = 1e-6
NEG_INF = -1e30

ATTN_W = ATTN_Q_HEADS * HEAD_DIM
KV_W = ATTN_KV_HEADS * HEAD_DIM
MLSTM_W = MLSTM_HEADS * HEAD_DIM
RWKV_W = RWKV_HEADS * HEAD_DIM
ATTN_COLS = ATTN_W + 2 * KV_W
MLSTM_MAIN = 4 * MLSTM_W
MLSTM_GATES = 2 * MLSTM_HEADS
RWKV_COLS = 3 * RWKV_W + RWKV_W_RANK + RWKV_A_RANK + RWKV_G_RANK

CHUNK = 64
NEUMANN_STEPS = 5
FF_CHUNK = 1408
SUBLANES = 8
V7X_VMEM_LIMIT = 56 * 1024 * 1024


def _params(semantics):
    return pltpu.CompilerParams(dimension_semantics=semantics, vmem_limit_bytes=V7X_VMEM_LIMIT)


def _resident(block_shape, index_map):
    return pl.BlockSpec(block_shape, index_map, pipeline_mode=pl.Buffered(1))


def _layer_vec(width, layer):
    return pl.BlockSpec((None, 1, width), lambda i: (layer, 0, 0))


def _rms(x, gain):
    return x * lax.rsqrt(jnp.mean(x * x, axis=-1, keepdims=True) + NORM_EPS) * gain


def _mm(a, b):
    return jnp.dot(a.astype(BF16), b.astype(BF16), preferred_element_type=F32)


def _mm_nt(a, b):
    return lax.dot_general(a.astype(BF16), b.astype(BF16), (((1,), (1,)), ((), ())),
                           preferred_element_type=F32)


def _mm_tn(a, b):
    return lax.dot_general(a.astype(BF16), b.astype(BF16), (((0,), (0,)), ((), ())),
                           preferred_element_type=F32)


def _bf16_terms(x, terms):
    out, rest = [], x
    for n in range(terms):
        part = rest.astype(BF16)
        out.append(part)
        if n + 1 < terms:
            rest = rest - part.astype(F32)
    return out


def _cumsum_rows(tri, x):
    return sum(jnp.dot(tri, term, preferred_element_type=F32) for term in _bf16_terms(x, 3))


def _split_dot(x, ones, terms=3):
    return sum(jnp.dot(term, ones, preferred_element_type=F32) for term in _bf16_terms(x, terms))


def _head_ones(width):
    row, col = _tri_masks(width)
    shift = HEAD_DIM.bit_length() - 1
    return ((row >> shift) == (col >> shift)).astype(BF16)


def _gate_spread():
    lanes = MLSTM_GATES * HEAD_DIM
    row = lax.broadcasted_iota(jnp.int32, (MLSTM_GATES, lanes), 0)
    col = lax.broadcasted_iota(jnp.int32, (MLSTM_GATES, lanes), 1)
    return (row == (col >> (HEAD_DIM.bit_length() - 1))).astype(BF16)


def _tri_masks(n):
    row = lax.broadcasted_iota(jnp.int32, (n, n), 0)
    col = lax.broadcasted_iota(jnp.int32, (n, n), 1)
    return row, col


def _chunk_tri(n):
    row, col = _tri_masks(n)
    shift = CHUNK.bit_length() - 1
    return (((row >> shift) == (col >> shift)) & (col <= row)).astype(BF16)


def _ffn_body(x_ref, gpre_ref, gpost_ref, win_ref, wout_ref, o_ref):
    x = x_ref[...]
    xn = _rms(x, gpre_ref[...]).astype(BF16)
    acc = None
    for lo in range(0, D_FF, FF_CHUNK):
        gate = jnp.dot(xn, win_ref[:, lo:lo + FF_CHUNK], preferred_element_type=F32)
        up = jnp.dot(xn, win_ref[:, D_FF + lo:D_FF + lo + FF_CHUNK], preferred_element_type=F32)
        act = (gate * jax.nn.sigmoid(gate) * up).astype(BF16)
        part = jnp.dot(act, wout_ref[lo:lo + FF_CHUNK, :], preferred_element_type=F32)
        acc = part if acc is None else acc + part
    o_ref[...] = x + 0.5 * _rms(acc, gpost_ref[...])


def _ffn(x, gpre, gpost, w_in, w_out, layer, tm):
    s = x.shape[0]
    row = lambda i: (i, 0)
    lay3 = lambda i: (layer, 0, 0)
    return pl.pallas_call(
        _ffn_body,
        out_shape=jax.ShapeDtypeStruct((s, D_MODEL), F32),
        grid=(s // tm,),
        in_specs=[
            pl.BlockSpec((tm, D_MODEL), row),
            _layer_vec(D_MODEL, layer),
            _layer_vec(D_MODEL, layer),
            _resident((None, D_MODEL, 2 * D_FF), lay3),
            _resident((None, D_FF, D_MODEL), lay3),
        ],
        out_specs=pl.BlockSpec((tm, D_MODEL), row),
        compiler_params=_params(("parallel",)),
        name="ffn",
    )(x, gpre, gpost, w_in, w_out)


def _proj_body(x_ref, g_ref, wa_ref, wm_ref, wg_ref, wr_ref, oa_ref, om_ref, og_ref, or_ref):
    h = _rms(x_ref[...], g_ref[...]).astype(BF16)
    oa_ref[...] = jnp.dot(h, wa_ref[...], preferred_element_type=F32)
    om_ref[...] = jnp.dot(h, wm_ref[...], preferred_element_type=F32)
    og_ref[...] = jnp.dot(h, wg_ref[...], preferred_element_type=F32)
    or_ref[...] = jnp.dot(h, wr_ref[...], preferred_element_type=F32)


def _mix_proj(x, gain, wa, wm, wg, wr, layer, tm):
    s = x.shape[0]
    row = lambda i: (i, 0)
    lay3 = lambda i: (layer, 0, 0)
    widths = (ATTN_COLS, MLSTM_MAIN, MLSTM_GATES, RWKV_COLS)
    return pl.pallas_call(
        _proj_body,
        out_shape=[jax.ShapeDtypeStruct((s, w), F32) for w in widths],
        grid=(s // tm,),
        in_specs=[pl.BlockSpec((tm, D_MODEL), row), _layer_vec(D_MODEL, layer)]
        + [_resident((None, D_MODEL, w), lay3) for w in widths],
        out_specs=[pl.BlockSpec((tm, w), row) for w in widths],
        compiler_params=_params(("parallel",)),
        name="mix_proj",
    )(x, gain, wa, wm, wg, wr)


def _out_body(ya_ref, ym_ref, yr_ref, x_ref, w_ref, g_ref, o_ref):
    mix = jnp.dot(ya_ref[...], w_ref[0:ATTN_W, :], preferred_element_type=F32)
    mix += jnp.dot(ym_ref[...], w_ref[ATTN_W:ATTN_W + MLSTM_W, :], preferred_element_type=F32)
    mix += jnp.dot(yr_ref[...], w_ref[ATTN_W + MLSTM_W:, :], preferred_element_type=F32)
    o_ref[...] = x_ref[...] + _rms(mix, g_ref[...])


def _out_proj(ya, ym, yr, x, w_out, gain, layer, tm):
    s = x.shape[0]
    row = lambda i: (i, 0)
    return pl.pallas_call(
        _out_body,
        out_shape=jax.ShapeDtypeStruct((s, D_MODEL), F32),
        grid=(s // tm,),
        in_specs=[
            pl.BlockSpec((tm, ATTN_W), row),
            pl.BlockSpec((tm, MLSTM_W), row),
            pl.BlockSpec((tm, RWKV_W), row),
            pl.BlockSpec((tm, D_MODEL), row),
            _resident((None, D_MODEL, D_MODEL), lambda i: (layer, 0, 0)),
            _layer_vec(D_MODEL, layer),
        ],
        out_specs=pl.BlockSpec((tm, D_MODEL), row),
        compiler_params=_params(("parallel",)),
        name="out_proj",
    )(ya, ym, yr, x, w_out, gain)


def _ple_body(x_ref, p_ref, gpre_ref, gpost_ref, wg_ref, wp_ref, o_ref):
    x = x_ref[...]
    gate = jax.nn.sigmoid(jnp.dot(_rms(x, gpre_ref[...]).astype(BF16), wg_ref[...],
                                  preferred_element_type=F32))
    emb = jnp.dot(p_ref[...].astype(BF16), wp_ref[...], preferred_element_type=F32)
    o_ref[...] = x + _rms(gate * emb, gpost_ref[...])


def _ple(x, p, gpre, gpost, w_gate, w_proj, layer, tm):
    s = x.shape[0]
    row = lambda i: (i, 0)
    lay3 = lambda i: (layer, 0, 0)
    return pl.pallas_call(
        _ple_body,
        out_shape=jax.ShapeDtypeStruct((s, D_MODEL), F32),
        grid=(s // tm,),
        in_specs=[
            pl.BlockSpec((tm, D_MODEL), row),
            pl.BlockSpec((None, tm, D_PLE), lambda i: (layer, i, 0)),
            _layer_vec(D_MODEL, layer),
            _layer_vec(D_MODEL, layer),
            _resident((None, D_MODEL, D_MODEL), lay3),
            _resident((None, D_PLE, D_MODEL), lay3),
        ],
        out_specs=pl.BlockSpec((tm, D_MODEL), row),
        compiler_params=_params(("parallel",)),
        name="ple",
    )(x, p, gpre, gpost, w_gate, w_proj)


def _rope_body(pos_ref, invf_ref, cos_ref, sa_ref, sb_ref):
    ang = pos_ref[...].astype(F32) * invf_ref[...]
    sin = jnp.sin(ang)
    dim = lax.broadcasted_iota(jnp.int32, ang.shape, 1) & (HEAD_DIM - 1)
    cos_ref[...] = jnp.cos(ang)
    sa_ref[...] = jnp.where(dim < ROPE_HALF, -sin, 0.0)
    sb_ref[...] = jnp.where((dim >= ROPE_HALF) & (dim < ROPE_DIM), sin, 0.0)


def _rope_tables(positions, tm):
    s = positions.shape[0]
    lane = jnp.arange(2 * HEAD_DIM) % HEAD_DIM
    freq = ROPE_THETA ** (-jnp.arange(0, ROPE_DIM, 2, dtype=F32) / ROPE_DIM)
    invf = jnp.where(lane < ROPE_DIM, freq[lane % ROPE_HALF], 0.0).astype(F32)[None, :]
    row = lambda i: (i, 0)
    out = jax.ShapeDtypeStruct((s, 2 * HEAD_DIM), F32)
    return pl.pallas_call(
        _rope_body,
        out_shape=[out, out, out],
        grid=(s // tm,),
        in_specs=[pl.BlockSpec((tm, 1), row), pl.BlockSpec((1, 2 * HEAD_DIM), lambda i: (0, 0))],
        out_specs=[pl.BlockSpec((tm, 2 * HEAD_DIM), row)] * 3,
        compiler_params=_params(("parallel",)),
        name="rope_tables",
    )(positions, invf)


def _attn_body(sinks_ref, p_ref, cos_ref, sa_ref, sb_ref, o_ref, kprev_ref, vprev_ref):
    step = pl.program_id(0)

    @pl.when(step == 0)
    def _():
        kprev_ref[...] = jnp.zeros_like(kprev_ref)
        vprev_ref[...] = jnp.zeros_like(vprev_ref)

    cos, sin_a, sin_b = cos_ref[...], sa_ref[...], sb_ref[...]

    def rope(x):
        width = x.shape[1]
        reps = width // cos.shape[1]
        tile = (lambda t: jnp.concatenate([t] * reps, axis=1)) if reps > 1 else (lambda t: t)
        return (x * tile(cos) + pltpu.roll(x, width - ROPE_HALF, 1) * tile(sin_a)
                + pltpu.roll(x, ROPE_HALF, 1) * tile(sin_b))

    q = rope(p_ref[:, 0:ATTN_W])
    k_cur = rope(p_ref[:, ATTN_W:ATTN_W + KV_W]).astype(BF16)
    v_cur = p_ref[:, ATTN_W + KV_W:ATTN_COLS].astype(BF16)
    k_prev, v_prev = kprev_ref[...], vprev_ref[...]

    rows = ATTN_GROUP * WINDOW
    t = lax.broadcasted_iota(jnp.int32, (rows, WINDOW), 0) & (WINDOW - 1)
    j = lax.broadcasted_iota(jnp.int32, (rows, WINDOW), 1)
    head_of_row = lax.broadcasted_iota(jnp.int32, (rows, 1), 0) >> (WINDOW.bit_length() - 1)
    cur_ok = j <= t
    prev_ok = (j > t) & (step > 0)

    outs = []
    for g in range(ATTN_KV_HEADS):
        heads = range(g * ATTN_GROUP, (g + 1) * ATTN_GROUP)
        qs = jnp.concatenate([q[:, h * HEAD_DIM:(h + 1) * HEAD_DIM] for h in heads], axis=0)
        lanes = slice(g * HEAD_DIM, (g + 1) * HEAD_DIM)
        s_cur = jnp.where(cur_ok, _mm_nt(qs, k_cur[:, lanes]) * (HEAD_DIM ** -0.5), NEG_INF)
        s_prev = jnp.where(prev_ok, _mm_nt(qs, k_prev[:, lanes]) * (HEAD_DIM ** -0.5), NEG_INF)
        sink = jnp.zeros((rows, 1), F32)
        for n, h in enumerate(heads):
            sink = jnp.where(head_of_row == n, sinks_ref[h], sink)
        m = jnp.maximum(jnp.maximum(jnp.max(s_cur, axis=-1, keepdims=True),
                                    jnp.max(s_prev, axis=-1, keepdims=True)), sink)
        p_cur = jnp.exp(s_cur - m)
        p_prev = jnp.exp(s_prev - m)
        den = (jnp.sum(p_cur, axis=-1, keepdims=True) + jnp.sum(p_prev, axis=-1, keepdims=True)
               + jnp.exp(sink - m))
        o = (_mm(p_cur, v_cur[:, lanes]) + _mm(p_prev, v_prev[:, lanes])) / den
        outs += [o[n * WINDOW:(n + 1) * WINDOW, :] for n in range(ATTN_GROUP)]
    o_ref[...] = jnp.concatenate(outs, axis=1).astype(o_ref.dtype)
    kprev_ref[...] = k_cur
    vprev_ref[...] = v_cur


def _attention(sinks, attn_p, cos, sin_a, sin_b, layer):
    s = attn_p.shape[0]
    row = lambda i: (i, 0)
    return pl.pallas_call(
        _attn_body,
        out_shape=jax.ShapeDtypeStruct((s, ATTN_W), BF16),
        grid=(s // WINDOW,),
        in_specs=[
            pl.BlockSpec(memory_space=pltpu.SMEM),
            pl.BlockSpec((WINDOW, ATTN_COLS), row),
            pl.BlockSpec((WINDOW, 2 * HEAD_DIM), row),
            pl.BlockSpec((WINDOW, 2 * HEAD_DIM), row),
            pl.BlockSpec((WINDOW, 2 * HEAD_DIM), row),
        ],
        out_specs=pl.BlockSpec((WINDOW, ATTN_W), row),
        scratch_shapes=[pltpu.VMEM((WINDOW, KV_W), BF16), pltpu.VMEM((WINDOW, KV_W), BF16)],
        compiler_params=_params(("arbitrary",)),
        name="swa_attention",
    )(sinks[layer], attn_p, cos, sin_a, sin_b)


def _mlstm_body(p_ref, gates_ref, conv_ref, bias_ref, norm_ref, o_ref, xpad_ref, c_ref, n_ref, *, tb):
    step = pl.program_id(0)

    @pl.when(step == 0)
    def _():
        xpad_ref[0:SUBLANES, :] = jnp.zeros((SUBLANES, 2 * MLSTM_W), F32)
        c_ref[...] = jnp.zeros_like(c_ref)
        n_ref[...] = jnp.zeros_like(n_ref)

    xpad_ref[SUBLANES:SUBLANES + tb, :] = p_ref[:, 0:2 * MLSTM_W]
    conv = None
    for tap in range(MLSTM_CONV):
        shifted = xpad_ref[pl.ds(SUBLANES - (MLSTM_CONV - 1) + tap, tb), :]
        term = shifted * conv_ref[tap:tap + 1, :]
        conv = term if conv is None else conv + term
    xpad_ref[0:SUBLANES, :] = xpad_ref[tb:tb + SUBLANES, :]
    qk = conv * jax.nn.sigmoid(conv)
    q_all = qk[:, 0:MLSTM_W] * (HEAD_DIM ** -0.5)
    k_all = qk[:, MLSTM_W:2 * MLSTM_W]

    pre = GATE_CAP * jnp.tanh((gates_ref[...] + bias_ref[...]) / GATE_CAP)
    logsig = jnp.minimum(pre, 0.0) - jnp.log(1.0 + jnp.exp(-jnp.abs(pre)))

    row, col = _tri_masks(CHUNK)
    causal = col <= row
    eye = col == row
    g_cum = _cumsum_rows(_chunk_tri(tb), logsig)

    gate_col = lax.broadcasted_iota(jnp.int32, (tb, MLSTM_GATES), 1)
    gate_lanes = _split_dot(jnp.where(gate_col < MLSTM_HEADS, pre, g_cum), _gate_spread())
    i_all, g_all = gate_lanes[:, 0:MLSTM_W], gate_lanes[:, MLSTM_W:2 * MLSTM_W]
    head_ones = _head_ones(MLSTM_W)

    def to_row(lane_bcast):
        return jnp.sum(jnp.where(eye, lane_bcast, 0.0), axis=0, keepdims=True)

    chunk_starts = range(0, tb, CHUNK)
    units = [(c0, h) for c0 in chunk_starts for h in range(MLSTM_HEADS)]
    head_lanes = lambda h: slice(h * HEAD_DIM, (h + 1) * HEAD_DIM)
    unit_of = lambda x: [x[c0:c0 + CHUNK, head_lanes(h)] for c0, h in units]
    chunk_of = lambda x: [x[c0:c0 + CHUNK, :] for c0 in chunk_starts]
    v_all = p_ref[:, 2 * MLSTM_W:3 * MLSTM_W]

    g_chunk, i_chunk = chunk_of(g_all), chunk_of(i_all)
    g_tot = [g[CHUNK - 1:CHUNK, :] for g in g_chunk]
    w_in = [jnp.exp(t - g + i) for t, g, i in zip(g_tot, g_chunk, i_chunk)]
    e_tot = [jnp.exp(t) for t in g_tot]
    e_g = [jnp.exp(g) for g in g_chunk]
    n_loc = [jnp.sum(w * x, axis=0, keepdims=True) for w, x in zip(w_in, chunk_of(k_all))]
    wv = unit_of(jnp.concatenate([w * x for w, x in zip(w_in, chunk_of(v_all))], axis=0))

    q, k, v = unit_of(q_all), unit_of(k_all), unit_of(v_all)
    decay = [jnp.exp(jnp.where(causal, g - to_row(g) + to_row(i), NEG_INF))
             for g, i in zip(unit_of(g_all), unit_of(i_all))]
    s_mat = [_mm_nt(a, b) * d for a, b, d in zip(q, k, decay)]
    ones = jnp.ones((CHUNK, HEAD_DIM), F32)
    num_den = [_mm(s, jnp.concatenate([x, ones], axis=1)) for s, x in zip(s_mat, v)]
    c_loc = [_mm_tn(x, b) for x, b in zip(wv, k)]

    c_state = [c_ref[h] for h in range(MLSTM_HEADS)]
    n_state = n_ref[0:1, :]
    c_in, n_in = [], []
    for u, (c0, h) in enumerate(units):
        n_chunk = u // MLSTM_HEADS
        if h == 0:
            n_in.append(n_state)
            n_state = e_tot[n_chunk] * n_state + n_loc[n_chunk]
        c_in.append(c_state[h])
        c_state[h] = e_tot[n_chunk][:, head_lanes(h)] * c_state[h] + c_loc[u]
    for h in range(MLSTM_HEADS):
        c_ref[h] = c_state[h]
    n_ref[0:1, :] = n_state

    inter = [_mm_nt(a, jnp.concatenate(
        [c, jnp.broadcast_to(n_in[u // MLSTM_HEADS][:, head_lanes(h)], (CHUNK, HEAD_DIM))], axis=0))
        for u, ((c0, h), a, c) in enumerate(zip(units, q, c_in))]
    e_g_unit = unit_of(jnp.concatenate(e_g, axis=0))
    top = [nd[:, :HEAD_DIM] + e * x[:, :HEAD_DIM] for nd, e, x in zip(num_den, e_g_unit, inter)]
    bottom = [nd[:, HEAD_DIM:] + e * x[:, HEAD_DIM:] for nd, e, x in zip(num_den, e_g_unit, inter)]
    for n, c0 in enumerate(chunk_starts):
        heads = slice(n * MLSTM_HEADS, (n + 1) * MLSTM_HEADS)
        hid = (jnp.concatenate(top[heads], axis=1)
               / jnp.maximum(jnp.abs(jnp.concatenate(bottom[heads], axis=1)), 1.0))
        mean_sq = _split_dot(hid * hid, head_ones, terms=2) * (1.0 / HEAD_DIM)
        hid = hid * lax.rsqrt(mean_sq + NORM_EPS) * norm_ref[...]
        o_gate = p_ref[c0:c0 + CHUNK, 3 * MLSTM_W:4 * MLSTM_W]
        o_ref[c0:c0 + CHUNK, :] = (jax.nn.sigmoid(o_gate) * hid).astype(o_ref.dtype)


def _mlstm(mlstm_p, gates, conv_w, bias, norm, layer, tb):
    s = mlstm_p.shape[0]
    row = lambda i: (i, 0)
    return pl.pallas_call(
        functools.partial(_mlstm_body, tb=tb),
        out_shape=jax.ShapeDtypeStruct((s, MLSTM_W), BF16),
        grid=(s // tb,),
        in_specs=[
            pl.BlockSpec((tb, MLSTM_MAIN), row),
            pl.BlockSpec((tb, MLSTM_GATES), row),
            pl.BlockSpec((None, MLSTM_CONV, 2 * MLSTM_W), lambda i: (layer, 0, 0)),
            _layer_vec(MLSTM_GATES, layer),
            _layer_vec(MLSTM_W, layer),
        ],
        out_specs=pl.BlockSpec((tb, MLSTM_W), row),
        scratch_shapes=[
            pltpu.VMEM((tb + SUBLANES, 2 * MLSTM_W), F32),
            pltpu.VMEM((MLSTM_HEADS, HEAD_DIM, HEAD_DIM), F32),
            pltpu.VMEM((SUBLANES, MLSTM_W), F32),
        ],
        compiler_params=_params(("arbitrary",)),
        name="mlstm",
    )(mlstm_p, gates, conv_w, bias, norm)


def _rwkv_chunk_terms(a_t, b_t, k_t, r_t, b_end, k_end, v):
    row, col = _tri_masks(CHUNK)
    incl = col <= row
    strict = col < row
    eye = (col == row).astype(F32)
    half = HEAD_DIM

    quad = [_mm_nt(jnp.concatenate([a, r], axis=0), jnp.concatenate([b, k], axis=0))
            for a, r, b, k in zip(a_t, r_t, b_t, k_t)]
    n_mat = [jnp.where(strict, x[:CHUNK, :CHUNK], 0.0) for x in quad]
    a_ak = [jnp.where(strict, x[:CHUNK, CHUNK:], 0.0) for x in quad]
    c_rb = [jnp.where(incl, x[CHUNK:, :CHUNK], 0.0) for x in quad]
    c_rk = [jnp.where(incl, x[CHUNK:, CHUNK:], 0.0) for x in quad]

    inv = [eye + n for n in n_mat]
    power = n_mat
    for _ in range(NEUMANN_STEPS):
        power = [_mm(x, x) for x in power]
        inv = [m + _mm(m, x) for m, x in zip(inv, power)]

    z = [_mm(x, y) for x, y in zip(a_ak, v)]
    w12 = [_mm(m, jnp.concatenate([a, y], axis=1)) for m, a, y in zip(inv, a_t, z)]
    cw = [_mm(c, w) for c, w in zip(c_rb, w12)]
    ckv = [_mm(c, y) for c, y in zip(c_rk, v)]
    gh = [_mm_tn(w, b) for w, b in zip(w12, b_end)]
    vk = [_mm_tn(y, k) for y, k in zip(v, k_end)]
    q_eff = [r + x[:, :half] for r, x in zip(r_t, cw)]
    y_loc = [x[:, half:] + y for x, y in zip(cw, ckv)]
    g_mat = [x[:half, :] for x in gh]
    h_mat = [x[half:, :] + y for x, y in zip(gh, vk)]
    return q_eff, y_loc, g_mat, h_mat


def _rwkv_body(p_ref, mu_ref, w0_ref, wup_ref, a0_ref, aup_ref, gup_ref, kk_ref, ka_ref, rk_ref,
               lnw_ref, lnb_ref, o_ref, upad_ref, s_ref, *, tb):
    step = pl.program_id(0)

    @pl.when(step == 0)
    def _():
        upad_ref[0:SUBLANES, :] = jnp.zeros((SUBLANES, RWKV_COLS), F32)
        s_ref[...] = jnp.zeros_like(s_ref)

    u = p_ref[...]
    upad_ref[SUBLANES:SUBLANES + tb, :] = u
    prev = upad_ref[pl.ds(SUBLANES - 1, tb), :]
    upad_ref[0:SUBLANES, :] = upad_ref[tb:tb + SUBLANES, :]
    u = u + (prev - u) * mu_ref[...]

    r_all = u[:, 0:RWKV_W]
    k_raw = u[:, RWKV_W:2 * RWKV_W]
    v_all = u[:, 2 * RWKV_W:3 * RWKV_W]
    x_w = u[:, 3 * RWKV_W:3 * RWKV_W + RWKV_W_RANK]
    x_a = u[:, 3 * RWKV_W + RWKV_W_RANK:3 * RWKV_W + RWKV_W_RANK + RWKV_A_RANK]
    x_g = u[:, 3 * RWKV_W + RWKV_W_RANK + RWKV_A_RANK:RWKV_COLS]

    z = w0_ref[...] + _mm(jnp.tanh(x_w), wup_ref[...])
    ld_all = -jnp.exp(-0.5) * jax.nn.sigmoid(z)
    a_all = jax.nn.sigmoid(a0_ref[...] + _mm(x_a, aup_ref[...]))
    g_all = _mm(jax.nn.sigmoid(x_g), gup_ref[...])
    kk_all = k_raw * kk_ref[...]
    k_all = k_raw * (1.0 + (a_all - 1.0) * ka_ref[...])

    head_ones = _head_ones(RWKV_W)
    head_sum = lambda x: _split_dot(x, head_ones, terms=2)
    kk_all = kk_all * lax.rsqrt(jnp.maximum(head_sum(kk_all * kk_all), 1e-24))
    a_vec = -kk_all
    b_vec = kk_all * a_all
    lp_all = _cumsum_rows(_chunk_tri(tb), ld_all)

    chunk_starts = range(0, tb, CHUNK)
    units = [(c0, h) for c0 in chunk_starts for h in range(RWKV_HEADS)]
    head_lanes = lambda h: slice(h * HEAD_DIM, (h + 1) * HEAD_DIM)
    per_unit = {name: [] for name in ("a_t", "b_t", "k_t", "r_t", "b_end", "k_end", "v")}
    decay_end = []
    for c0 in chunk_starts:
        rows = slice(c0, c0 + CHUNK)
        lp, ld = lp_all[rows, :], ld_all[rows, :]
        lp_last = lp[CHUNK - 1:CHUNK, :]
        grow = jnp.exp(-lp)
        to_end = jnp.exp(lp_last - lp)
        dense = dict(a_t=a_vec[rows, :] * jnp.exp(lp - ld), b_t=b_vec[rows, :] * grow,
                     k_t=k_all[rows, :] * grow, r_t=r_all[rows, :] * jnp.exp(lp),
                     b_end=b_vec[rows, :] * to_end, k_end=k_all[rows, :] * to_end, v=v_all[rows, :])
        end = jnp.exp(lp_last)
        for h in range(RWKV_HEADS):
            for name, value in dense.items():
                per_unit[name].append(value[:, head_lanes(h)])
            decay_end.append(end[:, head_lanes(h)])

    q_eff, y_loc, g_mat, h_mat = _rwkv_chunk_terms(**per_unit)

    state = [s_ref[h] for h in range(RWKV_HEADS)]
    ys = []
    for u, (c0, h) in enumerate(units):
        ys.append(_mm_nt(q_eff[u], state[h]) + y_loc[u])
        state[h] = state[h] * decay_end[u] + _mm(state[h], g_mat[u]) + h_mat[u]
    for h in range(RWKV_HEADS):
        s_ref[h] = state[h]

    y_all = jnp.concatenate([jnp.concatenate(ys[n * RWKV_HEADS:(n + 1) * RWKV_HEADS], axis=1)
                             for n in range(len(chunk_starts))], axis=0)
    centred = y_all - head_sum(y_all) * (1.0 / HEAD_DIM)
    var = head_sum(centred * centred) * (1.0 / HEAD_DIM)
    normed = centred * lax.rsqrt(var + RWKV_GN_EPS) * lnw_ref[...] + lnb_ref[...]
    bonus = head_sum(r_all * k_all * rk_ref[...])
    o_ref[...] = ((normed + bonus * v_all) * g_all).astype(o_ref.dtype)


def _rwkv(rwkv_p, mu, w0, w_up, a0, a_up, g_up, k_k, k_a, r_k, ln_w, ln_b, layer, tb):
    s = rwkv_p.shape[0]
    row = lambda i: (i, 0)
    lay3 = lambda i: (layer, 0, 0)
    vec = lambda width: _layer_vec(width, layer)
    return pl.pallas_call(
        functools.partial(_rwkv_body, tb=tb),
        out_shape=jax.ShapeDtypeStruct((s, RWKV_W), BF16),
        grid=(s // tb,),
        in_specs=[
            pl.BlockSpec((tb, RWKV_COLS), row),
            vec(RWKV_COLS), vec(RWKV_W),
            pl.BlockSpec((None, RWKV_W_RANK, RWKV_W), lay3),
            vec(RWKV_W),
            pl.BlockSpec((None, RWKV_A_RANK, RWKV_W), lay3),
            pl.BlockSpec((None, RWKV_G_RANK, RWKV_W), lay3),
            vec(RWKV_W), vec(RWKV_W), vec(RWKV_W), vec(RWKV_W), vec(RWKV_W),
        ],
        out_specs=pl.BlockSpec((tb, RWKV_W), row),
        scratch_shapes=[
            pltpu.VMEM((tb + SUBLANES, RWKV_COLS), F32),
            pltpu.VMEM((RWKV_HEADS, HEAD_DIM, HEAD_DIM), F32),
        ],
        compiler_params=_params(("arbitrary",)),
        name="rwkv7",
    )(rwkv_p, mu, w0, w_up, a0, a_up, g_up, k_k, k_a, r_k, ln_w, ln_b)


def _tiles(s):
    return min(512, s), min(256, s)


def kernel(x, p, positions, ln_ffn1_pre, ln_ffn1_post, w_ffn1_in, w_ffn1_out, ln_mix_pre, w_in, attn_sinks, mlstm_conv, mlstm_i_bias, mlstm_f_bias, mlstm_norm, rwkv_mu, rwkv_w0, rwkv_w_up, rwkv_a0, rwkv_a_up, rwkv_g_up, rwkv_k_k, rwkv_k_a, rwkv_r_k, rwkv_ln_w, rwkv_ln_b, w_out, ln_mix_post, ln_ffn2_pre, ln_ffn2_post, w_ffn2_in, w_ffn2_out, ln_ple_pre, w_ple_gate, w_ple_proj, ln_ple_post):
    batch, seq, _ = x.shape
    assert batch == 1 and seq % WINDOW == 0
    depth = w_in.shape[0]
    tm, tb = _tiles(seq)

    bf = lambda w: w.astype(BF16)
    w_ffn1_in, w_ffn1_out, w_ffn2_in, w_ffn2_out = map(bf, (w_ffn1_in, w_ffn1_out, w_ffn2_in, w_ffn2_out))
    w_out, w_ple_gate, w_ple_proj = map(bf, (w_out, w_ple_gate, w_ple_proj))
    rwkv_w_up, rwkv_a_up, rwkv_g_up = map(bf, (rwkv_w_up, rwkv_a_up, rwkv_g_up))
    w_in = bf(w_in)
    m0 = ATTN_COLS
    g0 = m0 + MLSTM_MAIN
    r0 = g0 + MLSTM_GATES
    w_attn, w_mlstm, w_gates, w_rwkv = w_in[:, :, :m0], w_in[:, :, m0:g0], w_in[:, :, g0:r0], w_in[:, :, r0:]
    vec = lambda a: a.reshape(depth, 1, -1)
    gate_bias = vec(jnp.concatenate([mlstm_i_bias, mlstm_f_bias], axis=-1))
    (ln_ffn1_pre, ln_ffn1_post, ln_mix_pre, ln_mix_post, ln_ffn2_pre, ln_ffn2_post, ln_ple_pre,
     ln_ple_post, mlstm_norm, rwkv_mu, rwkv_w0, rwkv_a0, rwkv_k_k, rwkv_k_a, rwkv_r_k, rwkv_ln_w,
     rwkv_ln_b) = map(vec, (
         ln_ffn1_pre, ln_ffn1_post, ln_mix_pre, ln_mix_post, ln_ffn2_pre, ln_ffn2_post, ln_ple_pre,
         ln_ple_post, mlstm_norm, rwkv_mu, rwkv_w0, rwkv_a0, rwkv_k_k, rwkv_k_a, rwkv_r_k, rwkv_ln_w,
         rwkv_ln_b))

    cos, sin_a, sin_b = _rope_tables(positions.reshape(seq, 1), tm)
    xs = x.reshape(seq, D_MODEL)
    for l in range(depth):
        xs = _ffn(xs, ln_ffn1_pre, ln_ffn1_post, w_ffn1_in, w_ffn1_out, l, tm)
        attn_p, mlstm_p, gates, rwkv_p = _mix_proj(xs, ln_mix_pre, w_attn, w_mlstm, w_gates, w_rwkv, l, tm)
        y_attn = _attention(attn_sinks, attn_p, cos, sin_a, sin_b, l)
        y_mlstm = _mlstm(mlstm_p, gates, mlstm_conv, gate_bias, mlstm_norm, l, tb)
        y_rwkv = _rwkv(rwkv_p, rwkv_mu, rwkv_w0, rwkv_w_up, rwkv_a0, rwkv_a_up, rwkv_g_up,
                       rwkv_k_k, rwkv_k_a, rwkv_r_k, rwkv_ln_w, rwkv_ln_b, l, tb)
        xs = _out_proj(y_attn, y_mlstm, y_rwkv, xs, w_out, ln_mix_post, l, tm)
        xs = _ffn(xs, ln_ffn2_pre, ln_ffn2_post, w_ffn2_in, w_ffn2_out, l, tm)
        xs = _ple(xs, p.reshape(depth, seq, D_PLE), ln_ple_pre, ln_ple_post, w_ple_gate, w_ple_proj, l, tm)
    return xs.reshape(batch, seq, D_MODEL)
```

```python
import functools

import jax
import jax.numpy as jnp
from jax import lax
from jax.experimental import pallas as pl
from jax.experimental.pallas import tpu as pltpu

F32 = jnp.float32
BF16 = jnp.bfloat16

D_MODEL = 1024
HEAD_DIM = 64
D_FF = 2816
D_PLE = 256
ATTN_Q_HEADS = 8
ATTN_KV_HEADS = 2
ATTN_GROUP = ATTN_Q_HEADS // ATTN_KV_HEADS
WINDOW = 128
ROPE_THETA = 500000.0
ROPE_DIM = HEAD_DIM // 4
ROPE_HALF = ROPE_DIM // 2
MLSTM_HEADS = 4
MLSTM_CONV = 4
GATE_CAP = 15.0
RWKV_HEADS = 4
RWKV_W_RANK = 64
RWKV_A_RANK = 64
RWKV_G_RANK = 128
RWKV_GN_EPS = 64e-5
NORM_EPS = 1e-6
NEG_INF = -1e30

ATTN_W = ATTN_Q_HEADS * HEAD_DIM
KV_W = ATTN_KV_HEADS * HEAD_DIM
MLSTM_W = MLSTM_HEADS * HEAD_DIM
RWKV_W = RWKV_HEADS * HEAD_DIM
ATTN_COLS = ATTN_W + 2 * KV_W
MLSTM_MAIN = 4 * MLSTM_W
MLSTM_GATES = 2 * MLSTM_HEADS
RWKV_COLS = 3 * RWKV_W + RWKV_W_RANK + RWKV_A_RANK + RWKV_G_RANK

CHUNK = 64
NEUMANN_STEPS = 5
MXU_TILE = 256
FF_SPLITS = (0, 6 * MXU_TILE, D_FF)
DENSE_SUB_ROWS = 256
SUBLANES = 8
V7X_VMEM_LIMIT = 56 * 1024 * 1024


def _params(semantics):
    return pltpu.CompilerParams(dimension_semantics=semantics, vmem_limit_bytes=V7X_VMEM_LIMIT)


def _resident(block_shape, index_map):
    return pl.BlockSpec(block_shape, index_map, pipeline_mode=pl.Buffered(1))


def _layer_vec(width, layer):
    return pl.BlockSpec((None, 1, width), lambda i: (layer, 0, 0))


def _rms(x, gain):
    return x * lax.rsqrt(jnp.mean(x * x, axis=-1, keepdims=True) + NORM_EPS) * gain


def _mm(a, b):
    return jnp.dot(a.astype(BF16), b.astype(BF16), preferred_element_type=F32)


def _mm_nt(a, b):
    return lax.dot_general(a.astype(BF16), b.astype(BF16), (((1,), (1,)), ((), ())),
                           preferred_element_type=F32)


def _mm_tn(a, b):
    return lax.dot_general(a.astype(BF16), b.astype(BF16), (((0,), (0,)), ((), ())),
                           preferred_element_type=F32)


def _bf16_terms(x, terms):
    out, rest = [], x
    for n in range(terms):
        part = rest.astype(BF16)
        out.append(part)
        if n + 1 < terms:
            rest = rest - part.astype(F32)
    return out


def _cumsum_rows(tri, x):
    return sum(jnp.dot(tri, term, preferred_element_type=F32) for term in _bf16_terms(x, 3))


def _split_dot(x, ones, terms=3):
    return sum(jnp.dot(term, ones, preferred_element_type=F32) for term in _bf16_terms(x, terms))


def _head_ones(width):
    row, col = _tri_masks(width)
    shift = HEAD_DIM.bit_length() - 1
    return ((row >> shift) == (col >> shift)).astype(BF16)


def _gate_spread():
    lanes = MLSTM_GATES * HEAD_DIM
    row = lax.broadcasted_iota(jnp.int32, (MLSTM_GATES, lanes), 0)
    col = lax.broadcasted_iota(jnp.int32, (MLSTM_GATES, lanes), 1)
    return (row == (col >> (HEAD_DIM.bit_length() - 1))).astype(BF16)


def _tri_masks(n):
    row = lax.broadcasted_iota(jnp.int32, (n, n), 0)
    col = lax.broadcasted_iota(jnp.int32, (n, n), 1)
    return row, col


def _chunk_tri(n):
    row, col = _tri_masks(n)
    shift = CHUNK.bit_length() - 1
    return (((row >> shift) == (col >> shift)) & (col <= row)).astype(BF16)


def _macaron_half_step(x, gpre_ref, gpost_ref, win_ref, wout_ref):
    xn = _rms(x, gpre_ref[...]).astype(BF16)
    acc = None
    for lo, hi in zip(FF_SPLITS[:-1], FF_SPLITS[1:]):
        gate = jnp.dot(xn, win_ref[:, lo:hi], preferred_element_type=F32)
        up = jnp.dot(xn, win_ref[:, D_FF + lo:D_FF + hi], preferred_element_type=F32)
        act = (gate * jax.nn.sigmoid(gate) * up).astype(BF16)
        part = jnp.dot(act, wout_ref[lo:hi, :], preferred_element_type=F32)
        acc = part if acc is None else acc + part
    return x + 0.5 * _rms(acc, gpost_ref[...])


def _pre_mix_body(x_ref, gpre_ref, gpost_ref, win_ref, wout_ref, gmix_ref, wa_ref, wm_ref, wg_ref, wr_ref,
                  x_out_ref, oa_ref, om_ref, og_ref, or_ref, *, sub):
    for r0 in range(0, x_ref.shape[0], sub):
        rows = slice(r0, r0 + sub)
        x = _macaron_half_step(x_ref[rows, :], gpre_ref, gpost_ref, win_ref, wout_ref)
        x_out_ref[rows, :] = x
        h = _rms(x, gmix_ref[...]).astype(BF16)
        oa_ref[rows, :] = jnp.dot(h, wa_ref[...], preferred_element_type=F32)
        om_ref[rows, :] = jnp.dot(h, wm_ref[...], preferred_element_type=F32)
        og_ref[rows, :] = jnp.dot(h, wg_ref[...], preferred_element_type=F32)
        or_ref[rows, :] = jnp.dot(h, wr_ref[...], preferred_element_type=F32)


def _pre_mix(x, gpre, gpost, w_ffn_in, w_ffn_out, gmix, wa, wm, wg, wr, layer, tm):
    s = x.shape[0]
    row = lambda i: (i, 0)
    lay3 = lambda i: (layer, 0, 0)
    widths = (ATTN_COLS, MLSTM_MAIN, MLSTM_GATES, RWKV_COLS)
    return pl.pallas_call(
        functools.partial(_pre_mix_body, sub=min(tm, DENSE_SUB_ROWS)),
        out_shape=[jax.ShapeDtypeStruct((s, D_MODEL), F32)] + [jax.ShapeDtypeStruct((s, w), F32) for w in widths],
        grid=(s // tm,),
        in_specs=[
            pl.BlockSpec((tm, D_MODEL), row),
            _layer_vec(D_MODEL, layer),
            _layer_vec(D_MODEL, layer),
            _resident((None, D_MODEL, 2 * D_FF), lay3),
            _resident((None, D_FF, D_MODEL), lay3),
            _layer_vec(D_MODEL, layer),
        ] + [_resident((None, D_MODEL, w), lay3) for w in widths],
        out_specs=[pl.BlockSpec((tm, D_MODEL), row)] + [pl.BlockSpec((tm, w), row) for w in widths],
        compiler_params=_params(("parallel",)),
        name="pre_mix",
    )(x, gpre, gpost, w_ffn_in, w_ffn_out, gmix, wa, wm, wg, wr)


def _post_mix_body(ya_ref, ym_ref, yr_ref, x_ref, p_ref, wo_ref, gmix_ref, gpre_ref, gpost_ref, win_ref, wout_ref,
                   gple_pre_ref, gple_post_ref, wgate_ref, wemb_ref, o_ref, *, sub):
    for r0 in range(0, x_ref.shape[0], sub):
        rows = slice(r0, r0 + sub)
        mix = jnp.dot(ya_ref[rows, :], wo_ref[0:ATTN_W, :], preferred_element_type=F32)
        mix += jnp.dot(ym_ref[rows, :], wo_ref[ATTN_W:ATTN_W + MLSTM_W, :], preferred_element_type=F32)
        mix += jnp.dot(yr_ref[rows, :], wo_ref[ATTN_W + MLSTM_W:, :], preferred_element_type=F32)
        x = x_ref[rows, :] + _rms(mix, gmix_ref[...])
        x = _macaron_half_step(x, gpre_ref, gpost_ref, win_ref, wout_ref)
        gate = jax.nn.sigmoid(jnp.dot(_rms(x, gple_pre_ref[...]).astype(BF16), wgate_ref[...],
                                      preferred_element_type=F32))
        emb = jnp.dot(p_ref[rows, :].astype(BF16), wemb_ref[...], preferred_element_type=F32)
        o_ref[rows, :] = x + _rms(gate * emb, gple_post_ref[...])


def _post_mix(ya, ym, yr, x, p, w_out, gmix, gpre, gpost, w_ffn_in, w_ffn_out, gple_pre, gple_post,
              w_gate, w_emb, layer, tm):
    s = x.shape[0]
    row = lambda i: (i, 0)
    lay3 = lambda i: (layer, 0, 0)
    vec = _layer_vec(D_MODEL, layer)
    return pl.pallas_call(
        functools.partial(_post_mix_body, sub=min(tm, DENSE_SUB_ROWS)),
        out_shape=jax.ShapeDtypeStruct((s, D_MODEL), F32),
        grid=(s // tm,),
        in_specs=[
            pl.BlockSpec((tm, ATTN_W), row),
            pl.BlockSpec((tm, MLSTM_W), row),
            pl.BlockSpec((tm, RWKV_W), row),
            pl.BlockSpec((tm, D_MODEL), row),
            pl.BlockSpec((None, tm, D_PLE), lambda i: (layer, i, 0)),
            _resident((None, D_MODEL, D_MODEL), lay3),
            vec, vec, vec,
            _resident((None, D_MODEL, 2 * D_FF), lay3),
            _resident((None, D_FF, D_MODEL), lay3),
            vec, vec,
            _resident((None, D_MODEL, D_MODEL), lay3),
            _resident((None, D_PLE, D_MODEL), lay3),
        ],
        out_specs=pl.BlockSpec((tm, D_MODEL), row),
        compiler_params=_params(("parallel",)),
        name="post_mix",
    )(ya, ym, yr, x, p, w_out, gmix, gpre, gpost, w_ffn_in, w_ffn_out, gple_pre, gple_post, w_gate, w_emb)


def _rope_body(pos_ref, invf_ref, cos_ref, sin_ref):
    ang = pos_ref[...].astype(F32) * invf_ref[...]
    sin = jnp.sin(ang)
    dim = lax.broadcasted_iota(jnp.int32, ang.shape, 1) & (HEAD_DIM - 1)
    cos_ref[...] = jnp.cos(ang)
    sin_ref[...] = jnp.where(dim < ROPE_HALF, -sin, jnp.where(dim < ROPE_DIM, sin, 0.0))


def _rope_tables(positions, tm):
    s = positions.shape[0]
    lane = jnp.arange(2 * HEAD_DIM) % HEAD_DIM
    freq = ROPE_THETA ** (-jnp.arange(0, ROPE_DIM, 2, dtype=F32) / ROPE_DIM)
    invf = jnp.where(lane < ROPE_DIM, freq[lane % ROPE_HALF], 0.0).astype(F32)[None, :]
    row = lambda i: (i, 0)
    out = jax.ShapeDtypeStruct((s, 2 * HEAD_DIM), F32)
    return pl.pallas_call(
        _rope_body,
        out_shape=[out, out],
        grid=(s // tm,),
        in_specs=[pl.BlockSpec((tm, 1), row), pl.BlockSpec((1, 2 * HEAD_DIM), lambda i: (0, 0))],
        out_specs=[pl.BlockSpec((tm, 2 * HEAD_DIM), row)] * 2,
        compiler_params=_params(("parallel",)),
        name="rope_tables",
    )(positions, invf)


ATTN_BLOCKS = 4
GROUP_W = ATTN_GROUP * HEAD_DIM
KEYS = 2 * WINDOW


def _rope_partner():
    row, col = _tri_masks(2 * HEAD_DIM)
    dim = col & (HEAD_DIM - 1)
    first = (dim < ROPE_HALF) & (row == col + ROPE_HALF)
    second = (dim >= ROPE_HALF) & (dim < ROPE_DIM) & (row == col - ROPE_HALF)
    return (first | second).astype(BF16)


def _attn_body(sinks_ref, p_ref, cos_ref, sin_ref, o_ref, ktprev_ref, vprev_ref, *, nblk):
    step = pl.program_id(0)

    @pl.when(step == 0)
    def _():
        ktprev_ref[...] = jnp.zeros_like(ktprev_ref)
        vprev_ref[...] = jnp.zeros_like(vprev_ref)

    cos, sin = cos_ref[...], sin_ref[...]
    partner = _rope_partner()
    period = 2 * HEAD_DIM

    def rope(x):
        slabs = [x[:, c:c + period] for c in range(0, x.shape[1], period)]
        return jnp.concatenate(
            [s * cos + jnp.dot(s.astype(BF16), partner, preferred_element_type=F32) * sin for s in slabs], axis=1)

    q = (rope(p_ref[:, 0:ATTN_W]) * (HEAD_DIM ** -0.5)).astype(BF16)
    k_cur = rope(p_ref[:, ATTN_W:ATTN_W + KV_W])
    v_cur = p_ref[:, ATTN_W + KV_W:ATTN_COLS].astype(BF16)
    kt = jnp.concatenate([ktprev_ref[...], k_cur.T.astype(BF16)], axis=1)
    vv = jnp.concatenate([vprev_ref[...], v_cur], axis=0)
    ktprev_ref[...] = kt[:, nblk * WINDOW:]
    vprev_ref[...] = vv[nblk * WINDOW:, :]

    t = lax.broadcasted_iota(jnp.int32, (WINDOW, ATTN_GROUP * KEYS), 0)
    lane = lax.broadcasted_iota(jnp.int32, (WINDOW, ATTN_GROUP * KEYS), 1)
    c = lane & (KEYS - 1)
    head = lane >> (KEYS.bit_length() - 1)
    cur_ok = (c >= WINDOW) & (c - WINDOW <= t)
    prev_ok = (c < WINDOW) & (c > t)

    def bias(g, has_prev):
        b = jnp.where(cur_ok | (prev_ok & has_prev), 0.0, NEG_INF)
        sink = jnp.zeros_like(b)
        for n in range(ATTN_GROUP):
            sink = jnp.where(head == n, sinks_ref[g * ATTN_GROUP + n], sink)
        return jnp.where(c == 0, sink, b)

    bias_first = [bias(g, step > 0) for g in range(ATTN_KV_HEADS)]
    bias_rest = [bias(g, True) for g in range(ATTN_KV_HEADS)]

    key_lane = lax.broadcasted_iota(jnp.int32, (HEAD_DIM, KEYS), 1)
    zero_kt = jnp.zeros((HEAD_DIM, KEYS), BF16)
    vrow = lax.broadcasted_iota(jnp.int32, (KEYS, GROUP_W), 0)
    vlane_head = lax.broadcasted_iota(jnp.int32, (KEYS, GROUP_W), 1) >> (HEAD_DIM.bit_length() - 1)
    spread_row = lax.broadcasted_iota(jnp.int32, (KV_W, GROUP_W), 0)
    spread_col = lax.broadcasted_iota(jnp.int32, (KV_W, GROUP_W), 1)

    units = [(b, g) for b in range(nblk) for g in range(ATTN_KV_HEADS)]
    kbd, vo = [], []
    for b, g in units:
        ktg = kt[g * HEAD_DIM:(g + 1) * HEAD_DIM, b * WINDOW:b * WINDOW + KEYS]
        ktg = jnp.where(key_lane == 0, jnp.zeros_like(ktg), ktg)
        kbd.append(jnp.concatenate(
            [jnp.concatenate([ktg if m == n else zero_kt for m in range(ATTN_GROUP)], axis=1)
             for n in range(ATTN_GROUP)], axis=0))
        spread = (spread_row == g * HEAD_DIM + (spread_col & (HEAD_DIM - 1))).astype(BF16)
        vt = jnp.dot(vv[b * WINDOW:b * WINDOW + KEYS, :], spread, preferred_element_type=F32).astype(BF16)
        vt = jnp.where(vrow == 0, jnp.zeros_like(vt), vt)
        vbd = jnp.concatenate(
            [jnp.where(vlane_head == n, vt, jnp.zeros_like(vt)) for n in range(ATTN_GROUP)], axis=0)
        ones = jnp.concatenate([(vlane_head == n).astype(BF16) for n in range(ATTN_GROUP)], axis=0)
        vo.append(jnp.concatenate([vbd, ones], axis=1))

    scores = [jnp.dot(q[b * WINDOW:(b + 1) * WINDOW, g * GROUP_W:(g + 1) * GROUP_W], kb,
                      preferred_element_type=F32) + (bias_first[g] if b == 0 else bias_rest[g])
              for (b, g), kb in zip(units, kbd)]
    probs = []
    for s in scores:
        parts = []
        for n in range(ATTN_GROUP):
            sn = s[:, n * KEYS:(n + 1) * KEYS]
            parts.append(jnp.exp(sn - jnp.max(sn, axis=-1, keepdims=True)).astype(BF16))
        probs.append(jnp.concatenate(parts, axis=1))
    outs = [jnp.dot(pr, x, preferred_element_type=F32) for pr, x in zip(probs, vo)]
    for (b, g), od in zip(units, outs):
        o_ref[b * WINDOW:(b + 1) * WINDOW, g * GROUP_W:(g + 1) * GROUP_W] = (
            od[:, :GROUP_W] / od[:, GROUP_W:]).astype(o_ref.dtype)


def _attention(sinks, attn_p, cos, sin, layer):
    s = attn_p.shape[0]
    nblk = min(ATTN_BLOCKS, s // WINDOW)
    tq = nblk * WINDOW
    row = lambda i: (i, 0)
    return pl.pallas_call(
        functools.partial(_attn_body, nblk=nblk),
        out_shape=jax.ShapeDtypeStruct((s, ATTN_W), BF16),
        grid=(s // tq,),
        in_specs=[
            pl.BlockSpec(memory_space=pltpu.SMEM),
            pl.BlockSpec((tq, ATTN_COLS), row),
            pl.BlockSpec((tq, 2 * HEAD_DIM), row),
            pl.BlockSpec((tq, 2 * HEAD_DIM), row),
        ],
        out_specs=pl.BlockSpec((tq, ATTN_W), row),
        scratch_shapes=[pltpu.VMEM((KV_W, WINDOW), BF16), pltpu.VMEM((WINDOW, KV_W), BF16)],
        compiler_params=_params(("arbitrary",)),
        name="swa_attention",
    )(sinks[layer], attn_p, cos, sin)


def _mlstm_body(p_ref, gates_ref, conv_ref, bias_ref, norm_ref, o_ref, xpad_ref, c_ref, n_ref, *, tb):
    step = pl.program_id(0)

    @pl.when(step == 0)
    def _():
        xpad_ref[0:SUBLANES, :] = jnp.zeros((SUBLANES, 2 * MLSTM_W), F32)
        c_ref[...] = jnp.zeros_like(c_ref)
        n_ref[...] = jnp.zeros_like(n_ref)

    xpad_ref[SUBLANES:SUBLANES + tb, :] = p_ref[:, 0:2 * MLSTM_W]
    conv = None
    for tap in range(MLSTM_CONV):
        shifted = xpad_ref[pl.ds(SUBLANES - (MLSTM_CONV - 1) + tap, tb), :]
        term = shifted * conv_ref[tap:tap + 1, :]
        conv = term if conv is None else conv + term
    xpad_ref[0:SUBLANES, :] = xpad_ref[tb:tb + SUBLANES, :]
    qk = conv * jax.nn.sigmoid(conv)
    q_all = qk[:, 0:MLSTM_W] * (HEAD_DIM ** -0.5)
    k_all = qk[:, MLSTM_W:2 * MLSTM_W]

    pre = GATE_CAP * jnp.tanh((gates_ref[...] + bias_ref[...]) / GATE_CAP)
    logsig = jnp.minimum(pre, 0.0) - jnp.log(1.0 + jnp.exp(-jnp.abs(pre)))

    row, col = _tri_masks(CHUNK)
    causal = col <= row
    eye = col == row
    g_cum = _cumsum_rows(_chunk_tri(tb), logsig)

    gate_col = lax.broadcasted_iota(jnp.int32, (tb, MLSTM_GATES), 1)
    gate_lanes = _split_dot(jnp.where(gate_col < MLSTM_HEADS, pre, g_cum), _gate_spread())
    i_all, g_all = gate_lanes[:, 0:MLSTM_W], gate_lanes[:, MLSTM_W:2 * MLSTM_W]
    head_ones = _head_ones(MLSTM_W)

    def to_row(lane_bcast):
        return jnp.sum(jnp.where(eye, lane_bcast, 0.0), axis=0, keepdims=True)

    chunk_starts = range(0, tb, CHUNK)
    units = [(c0, h) for c0 in chunk_starts for h in range(MLSTM_HEADS)]
    head_lanes = lambda h: slice(h * HEAD_DIM, (h + 1) * HEAD_DIM)
    unit_of = lambda x: [x[c0:c0 + CHUNK, head_lanes(h)] for c0, h in units]
    chunk_of = lambda x: [x[c0:c0 + CHUNK, :] for c0 in chunk_starts]
    v_all = p_ref[:, 2 * MLSTM_W:3 * MLSTM_W]

    g_chunk, i_chunk = chunk_of(g_all), chunk_of(i_all)
    g_tot = [g[CHUNK - 1:CHUNK, :] for g in g_chunk]
    w_in = [jnp.exp(t - g + i) for t, g, i in zip(g_tot, g_chunk, i_chunk)]
    e_tot = [jnp.exp(t) for t in g_tot]
    e_g = [jnp.exp(g) for g in g_chunk]
    n_loc = [jnp.sum(w * x, axis=0, keepdims=True) for w, x in zip(w_in, chunk_of(k_all))]
    wv = unit_of(jnp.concatenate([w * x for w, x in zip(w_in, chunk_of(v_all))], axis=0))

    q, k, v = unit_of(q_all), unit_of(k_all), unit_of(v_all)
    decay = [jnp.exp(jnp.where(causal, g - to_row(g) + to_row(i), NEG_INF))
             for g, i in zip(unit_of(g_all), unit_of(i_all))]
    s_mat = [_mm_nt(a, b) * d for a, b, d in zip(q, k, decay)]
    ones = jnp.ones((CHUNK, HEAD_DIM), F32)
    num_den = [_mm(s, jnp.concatenate([x, ones], axis=1)) for s, x in zip(s_mat, v)]
    c_loc = [_mm_tn(x, b) for x, b in zip(wv, k)]

    c_state = [c_ref[h] for h in range(MLSTM_HEADS)]
    n_state = n_ref[0:1, :]
    c_in, n_in = [], []
    for u, (c0, h) in enumerate(units):
        n_chunk = u // MLSTM_HEADS
        if h == 0:
            n_in.append(n_state)
            n_state = e_tot[n_chunk] * n_state + n_loc[n_chunk]
        c_in.append(c_state[h])
        c_state[h] = e_tot[n_chunk][:, head_lanes(h)] * c_state[h] + c_loc[u]
    for h in range(MLSTM_HEADS):
        c_ref[h] = c_state[h]
    n_ref[0:1, :] = n_state

    inter = [_mm_nt(a, jnp.concatenate(
        [c, jnp.broadcast_to(n_in[u // MLSTM_HEADS][:, head_lanes(h)], (CHUNK, HEAD_DIM))], axis=0))
        for u, ((c0, h), a, c) in enumerate(zip(units, q, c_in))]
    e_g_unit = unit_of(jnp.concatenate(e_g, axis=0))
    top = [nd[:, :HEAD_DIM] + e * x[:, :HEAD_DIM] for nd, e, x in zip(num_den, e_g_unit, inter)]
    bottom = [nd[:, HEAD_DIM:] + e * x[:, HEAD_DIM:] for nd, e, x in zip(num_den, e_g_unit, inter)]
    for n, c0 in enumerate(chunk_starts):
        heads = slice(n * MLSTM_HEADS, (n + 1) * MLSTM_HEADS)
        hid = (jnp.concatenate(top[heads], axis=1)
               / jnp.maximum(jnp.abs(jnp.concatenate(bottom[heads], axis=1)), 1.0))
        mean_sq = _split_dot(hid * hid, head_ones, terms=2) * (1.0 / HEAD_DIM)
        hid = hid * lax.rsqrt(mean_sq + NORM_EPS) * norm_ref[...]
        o_gate = p_ref[c0:c0 + CHUNK, 3 * MLSTM_W:4 * MLSTM_W]
        o_ref[c0:c0 + CHUNK, :] = (jax.nn.sigmoid(o_gate) * hid).astype(o_ref.dtype)


def _mlstm(mlstm_p, gates, conv_w, bias, norm, layer, tb):
    s = mlstm_p.shape[0]
    row = lambda i: (i, 0)
    return pl.pallas_call(
        functools.partial(_mlstm_body, tb=tb),
        out_shape=jax.ShapeDtypeStruct((s, MLSTM_W), BF16),
        grid=(s // tb,),
        in_specs=[
            pl.BlockSpec((tb, MLSTM_MAIN), row),
            pl.BlockSpec((tb, MLSTM_GATES), row),
            pl.BlockSpec((None, MLSTM_CONV, 2 * MLSTM_W), lambda i: (layer, 0, 0)),
            _layer_vec(MLSTM_GATES, layer),
            _layer_vec(MLSTM_W, layer),
        ],
        out_specs=pl.BlockSpec((tb, MLSTM_W), row),
        scratch_shapes=[
            pltpu.VMEM((tb + SUBLANES, 2 * MLSTM_W), F32),
            pltpu.VMEM((MLSTM_HEADS, HEAD_DIM, HEAD_DIM), F32),
            pltpu.VMEM((SUBLANES, MLSTM_W), F32),
        ],
        compiler_params=_params(("arbitrary",)),
        name="mlstm",
    )(mlstm_p, gates, conv_w, bias, norm)


def _rwkv_chunk_terms(a_t, b_t, k_t, r_t, b_end, k_end, v):
    row, col = _tri_masks(CHUNK)
    incl = col <= row
    strict = col < row
    eye = (col == row).astype(F32)
    half = HEAD_DIM

    quad = [_mm_nt(jnp.concatenate([a, r], axis=0), jnp.concatenate([b, k], axis=0))
            for a, r, b, k in zip(a_t, r_t, b_t, k_t)]
    n_mat = [jnp.where(strict, x[:CHUNK, :CHUNK], 0.0) for x in quad]
    a_ak = [jnp.where(strict, x[:CHUNK, CHUNK:], 0.0) for x in quad]
    c_rb = [jnp.where(incl, x[CHUNK:, :CHUNK], 0.0) for x in quad]
    c_rk = [jnp.where(incl, x[CHUNK:, CHUNK:], 0.0) for x in quad]

    inv = [eye + n for n in n_mat]
    power = n_mat
    for _ in range(NEUMANN_STEPS):
        power = [_mm(x, x) for x in power]
        inv = [m + _mm(m, x) for m, x in zip(inv, power)]

    z = [_mm(x, y) for x, y in zip(a_ak, v)]
    w12 = [_mm(m, jnp.concatenate([a, y], axis=1)) for m, a, y in zip(inv, a_t, z)]
    cw = [_mm(c, w) for c, w in zip(c_rb, w12)]
    ckv = [_mm(c, y) for c, y in zip(c_rk, v)]
    gh = [_mm_tn(w, b) for w, b in zip(w12, b_end)]
    vk = [_mm_tn(y, k) for y, k in zip(v, k_end)]
    q_eff = [r + x[:, :half] for r, x in zip(r_t, cw)]
    y_loc = [x[:, half:] + y for x, y in zip(cw, ckv)]
    g_mat = [x[:half, :] for x in gh]
    h_mat = [x[half:, :] + y for x, y in zip(gh, vk)]
    return q_eff, y_loc, g_mat, h_mat


def _rwkv_body(p_ref, mu_ref, w0_ref, wup_ref, a0_ref, aup_ref, gup_ref, kk_ref, ka_ref, rk_ref,
               lnw_ref, lnb_ref, o_ref, upad_ref, s_ref, *, tb):
    step = pl.program_id(0)

    @pl.when(step == 0)
    def _():
        upad_ref[0:SUBLANES, :] = jnp.zeros((SUBLANES, RWKV_COLS), F32)
        s_ref[...] = jnp.zeros_like(s_ref)

    u = p_ref[...]
    upad_ref[SUBLANES:SUBLANES + tb, :] = u
    prev = upad_ref[pl.ds(SUBLANES - 1, tb), :]
    upad_ref[0:SUBLANES, :] = upad_ref[tb:tb + SUBLANES, :]
    u = u + (prev - u) * mu_ref[...]

    r_all = u[:, 0:RWKV_W]
    k_raw = u[:, RWKV_W:2 * RWKV_W]
    v_all = u[:, 2 * RWKV_W:3 * RWKV_W]
    x_w = u[:, 3 * RWKV_W:3 * RWKV_W + RWKV_W_RANK]
    x_a = u[:, 3 * RWKV_W + RWKV_W_RANK:3 * RWKV_W + RWKV_W_RANK + RWKV_A_RANK]
    x_g = u[:, 3 * RWKV_W + RWKV_W_RANK + RWKV_A_RANK:RWKV_COLS]

    z = w0_ref[...] + _mm(jnp.tanh(x_w), wup_ref[...])
    ld_all = -jnp.exp(-0.5) * jax.nn.sigmoid(z)
    a_all = jax.nn.sigmoid(a0_ref[...] + _mm(x_a, aup_ref[...]))
    g_all = _mm(jax.nn.sigmoid(x_g), gup_ref[...])
    kk_all = k_raw * kk_ref[...]
    k_all = k_raw * (1.0 + (a_all - 1.0) * ka_ref[...])

    head_ones = _head_ones(RWKV_W)
    head_sum = lambda x: _split_dot(x, head_ones, terms=2)
    kk_all = kk_all * lax.rsqrt(jnp.maximum(head_sum(kk_all * kk_all), 1e-24))
    a_vec = -kk_all
    b_vec = kk_all * a_all
    lp_all = _cumsum_rows(_chunk_tri(tb), ld_all)

    chunk_starts = range(0, tb, CHUNK)
    units = [(c0, h) for c0 in chunk_starts for h in range(RWKV_HEADS)]
    head_lanes = lambda h: slice(h * HEAD_DIM, (h + 1) * HEAD_DIM)
    per_unit = {name: [] for name in ("a_t", "b_t", "k_t", "r_t", "b_end", "k_end", "v")}
    decay_end = []
    for c0 in chunk_starts:
        rows = slice(c0, c0 + CHUNK)
        lp, ld = lp_all[rows, :], ld_all[rows, :]
        lp_last = lp[CHUNK - 1:CHUNK, :]
        grow = jnp.exp(-lp)
        to_end = jnp.exp(lp_last - lp)
        dense = dict(a_t=a_vec[rows, :] * jnp.exp(lp - ld), b_t=b_vec[rows, :] * grow,
                     k_t=k_all[rows, :] * grow, r_t=r_all[rows, :] * jnp.exp(lp),
                     b_end=b_vec[rows, :] * to_end, k_end=k_all[rows, :] * to_end, v=v_all[rows, :])
        end = jnp.exp(lp_last)
        for h in range(RWKV_HEADS):
            for name, value in dense.items():
                per_unit[name].append(value[:, head_lanes(h)])
            decay_end.append(end[:, head_lanes(h)])

    q_eff, y_loc, g_mat, h_mat = _rwkv_chunk_terms(**per_unit)

    state = [s_ref[h] for h in range(RWKV_HEADS)]
    ys = []
    for u, (c0, h) in enumerate(units):
        ys.append(_mm_nt(q_eff[u], state[h]) + y_loc[u])
        state[h] = state[h] * decay_end[u] + _mm(state[h], g_mat[u]) + h_mat[u]
    for h in range(RWKV_HEADS):
        s_ref[h] = state[h]

    y_all = jnp.concatenate([jnp.concatenate(ys[n * RWKV_HEADS:(n + 1) * RWKV_HEADS], axis=1)
                             for n in range(len(chunk_starts))], axis=0)
    centred = y_all - head_sum(y_all) * (1.0 / HEAD_DIM)
    var = head_sum(centred * centred) * (1.0 / HEAD_DIM)
    normed = centred * lax.rsqrt(var + RWKV_GN_EPS) * lnw_ref[...] + lnb_ref[...]
    bonus = head_sum(r_all * k_all * rk_ref[...])
    o_ref[...] = ((normed + bonus * v_all) * g_all).astype(o_ref.dtype)


def _rwkv(rwkv_p, mu, w0, w_up, a0, a_up, g_up, k_k, k_a, r_k, ln_w, ln_b, layer, tb):
    s = rwkv_p.shape[0]
    row = lambda i: (i, 0)
    lay3 = lambda i: (layer, 0, 0)
    vec = lambda width: _layer_vec(width, layer)
    return pl.pallas_call(
        functools.partial(_rwkv_body, tb=tb),
        out_shape=jax.ShapeDtypeStruct((s, RWKV_W), BF16),
        grid=(s // tb,),
        in_specs=[
            pl.BlockSpec((tb, RWKV_COLS), row),
            vec(RWKV_COLS), vec(RWKV_W),
            pl.BlockSpec((None, RWKV_W_RANK, RWKV_W), lay3),
            vec(RWKV_W),
            pl.BlockSpec((None, RWKV_A_RANK, RWKV_W), lay3),
            pl.BlockSpec((None, RWKV_G_RANK, RWKV_W), lay3),
            vec(RWKV_W), vec(RWKV_W), vec(RWKV_W), vec(RWKV_W), vec(RWKV_W),
        ],
        out_specs=pl.BlockSpec((tb, RWKV_W), row),
        scratch_shapes=[
            pltpu.VMEM((tb + SUBLANES, RWKV_COLS), F32),
            pltpu.VMEM((RWKV_HEADS, HEAD_DIM, HEAD_DIM), F32),
        ],
        compiler_params=_params(("arbitrary",)),
        name="rwkv7",
    )(rwkv_p, mu, w0, w_up, a0, a_up, g_up, k_k, k_a, r_k, ln_w, ln_b)


def _tiles(s):
    return min(512, s), min(256, s)


def kernel(x, p, positions, ln_ffn1_pre, ln_ffn1_post, w_ffn1_in, w_ffn1_out, ln_mix_pre, w_in, attn_sinks, mlstm_conv, mlstm_i_bias, mlstm_f_bias, mlstm_norm, rwkv_mu, rwkv_w0, rwkv_w_up, rwkv_a0, rwkv_a_up, rwkv_g_up, rwkv_k_k, rwkv_k_a, rwkv_r_k, rwkv_ln_w, rwkv_ln_b, w_out, ln_mix_post, ln_ffn2_pre, ln_ffn2_post, w_ffn2_in, w_ffn2_out, ln_ple_pre, w_ple_gate, w_ple_proj, ln_ple_post):
    batch, seq, _ = x.shape
    assert batch == 1 and seq % WINDOW == 0
    depth = w_in.shape[0]
    tm, tb = _tiles(seq)

    bf = lambda w: w.astype(BF16)
    w_ffn1_in, w_ffn1_out, w_ffn2_in, w_ffn2_out = map(bf, (w_ffn1_in, w_ffn1_out, w_ffn2_in, w_ffn2_out))
    w_out, w_ple_gate, w_ple_proj = map(bf, (w_out, w_ple_gate, w_ple_proj))
    rwkv_w_up, rwkv_a_up, rwkv_g_up = map(bf, (rwkv_w_up, rwkv_a_up, rwkv_g_up))
    w_in = bf(w_in)
    m0 = ATTN_COLS
    g0 = m0 + MLSTM_MAIN
    r0 = g0 + MLSTM_GATES
    w_attn, w_mlstm, w_gates, w_rwkv = w_in[:, :, :m0], w_in[:, :, m0:g0], w_in[:, :, g0:r0], w_in[:, :, r0:]
    vec = lambda a: a.reshape(depth, 1, -1)
    gate_bias = vec(jnp.concatenate([mlstm_i_bias, mlstm_f_bias], axis=-1))
    (ln_ffn1_pre, ln_ffn1_post, ln_mix_pre, ln_mix_post, ln_ffn2_pre, ln_ffn2_post, ln_ple_pre,
     ln_ple_post, mlstm_norm, rwkv_mu, rwkv_w0, rwkv_a0, rwkv_k_k, rwkv_k_a, rwkv_r_k, rwkv_ln_w,
     rwkv_ln_b) = map(vec, (
         ln_ffn1_pre, ln_ffn1_post, ln_mix_pre, ln_mix_post, ln_ffn2_pre, ln_ffn2_post, ln_ple_pre,
         ln_ple_post, mlstm_norm, rwkv_mu, rwkv_w0, rwkv_a0, rwkv_k_k, rwkv_k_a, rwkv_r_k, rwkv_ln_w,
         rwkv_ln_b))

    cos, sin = _rope_tables(positions.reshape(seq, 1), tm)
    xs = x.reshape(seq, D_MODEL)
    for l in range(depth):
        xs, attn_p, mlstm_p, gates, rwkv_p = _pre_mix(
            xs, ln_ffn1_pre, ln_ffn1_post, w_ffn1_in, w_ffn1_out, ln_mix_pre, w_attn, w_mlstm, w_gates, w_rwkv, l, tm)
        y_attn = _attention(attn_sinks, attn_p, cos, sin, l)
        y_mlstm = _mlstm(mlstm_p, gates, mlstm_conv, gate_bias, mlstm_norm, l, tb)
        y_rwkv = _rwkv(rwkv_p, rwkv_mu, rwkv_w0, rwkv_w_up, rwkv_a0, rwkv_a_up, rwkv_g_up,
                       rwkv_k_k, rwkv_k_a, rwkv_r_k, rwkv_ln_w, rwkv_ln_b, l, tb)
        xs = _post_mix(y_attn, y_mlstm, y_rwkv, xs, p.reshape(depth, seq, D_PLE), w_out, ln_mix_post,
                       ln_ffn2_pre, ln_ffn2_post, w_ffn2_in, w_ffn2_out, ln_ple_pre, ln_ple_post,
                       w_ple_gate, w_ple_proj, l, tm)
    return xs.reshape(batch, seq, D_MODEL)
```

```python
import functools

import jax
import jax.numpy as jnp
from jax import lax
from jax.experimental import pallas as pl
from jax.experimental.pallas import tpu as pltpu

F32 = jnp.float32
BF16 = jnp.bfloat16

D_MODEL = 1024
HEAD_DIM = 64
D_FF = 2816
D_PLE = 256
ATTN_Q_HEADS = 8
ATTN_KV_HEADS = 2
ATTN_GROUP = ATTN_Q_HEADS // ATTN_KV_HEADS
WINDOW = 128
ROPE_THETA = 500000.0
ROPE_DIM = HEAD_DIM // 4
ROPE_HALF = ROPE_DIM // 2
MLSTM_HEADS = 4
MLSTM_CONV = 4
GATE_CAP = 15.0
RWKV_HEADS = 4
RWKV_W_RANK = 64
RWKV_A_RANK = 64
RWKV_G_RANK = 128
RWKV_GN_EPS = 64e-5
NORM_EPS = 1e-6
NEG_INF = -1e30

ATTN_W = ATTN_Q_HEADS * HEAD_DIM
KV_W = ATTN_KV_HEADS * HEAD_DIM
MLSTM_W = MLSTM_HEADS * HEAD_DIM
RWKV_W = RWKV_HEADS * HEAD_DIM
ATTN_COLS = ATTN_W + 2 * KV_W
MLSTM_MAIN = 4 * MLSTM_W
MLSTM_GATES = 2 * MLSTM_HEADS
RWKV_COLS = 3 * RWKV_W + RWKV_W_RANK + RWKV_A_RANK + RWKV_G_RANK

CHUNK = 64
NEUMANN_STEPS = 5
RWKV_SLAB = 128
RWKV_STAGED = ("a_t", "b_t", "k_t", "r_t", "b_end", "k_end", "v", "bonus", "gate")
MXU_TILE = 256
FF_SPLITS = (0, 6 * MXU_TILE, D_FF)
SUBLANES = 8
V7X_VMEM_LIMIT = 56 * 1024 * 1024


def _params(semantics):
    return pltpu.CompilerParams(dimension_semantics=semantics, vmem_limit_bytes=V7X_VMEM_LIMIT)


def _resident(block_shape, index_map):
    return pl.BlockSpec(block_shape, index_map, pipeline_mode=pl.Buffered(1))


def _layer_vec(width, layer):
    return pl.BlockSpec((None, 1, width), lambda i: (layer, 0, 0))


def _rms(x, gain):
    return x * lax.rsqrt(jnp.mean(x * x, axis=-1, keepdims=True) + NORM_EPS) * gain


def _mm(a, b):
    return jnp.dot(a.astype(BF16), b.astype(BF16), preferred_element_type=F32)


def _mm_nt(a, b):
    return lax.dot_general(a.astype(BF16), b.astype(BF16), (((1,), (1,)), ((), ())),
                           preferred_element_type=F32)


def _mm_tn(a, b):
    return lax.dot_general(a.astype(BF16), b.astype(BF16), (((0,), (0,)), ((), ())),
                           preferred_element_type=F32)


def _bf16_terms(x, terms):
    out, rest = [], x
    for n in range(terms):
        part = rest.astype(BF16)
        out.append(part)
        if n + 1 < terms:
            rest = rest - part.astype(F32)
    return out


def _cumsum_rows(tri, x):
    return sum(jnp.dot(tri, term, preferred_element_type=F32) for term in _bf16_terms(x, 3))


def _split_dot(x, ones, terms=3):
    return sum(jnp.dot(term, ones, preferred_element_type=F32) for term in _bf16_terms(x, terms))


def _head_ones(width):
    row, col = _tri_masks(width)
    shift = HEAD_DIM.bit_length() - 1
    return ((row >> shift) == (col >> shift)).astype(BF16)


def _gate_spread():
    lanes = MLSTM_GATES * HEAD_DIM
    row = lax.broadcasted_iota(jnp.int32, (MLSTM_GATES, lanes), 0)
    col = lax.broadcasted_iota(jnp.int32, (MLSTM_GATES, lanes), 1)
    return (row == (col >> (HEAD_DIM.bit_length() - 1))).astype(BF16)


def _interleave(*gens):
    results = [None] * len(gens)
    live = dict(enumerate(gens))
    while live:
        for n in list(live):
            try:
                next(live[n])
            except StopIteration as stop:
                results[n] = stop.value
                del live[n]
    return results


def _tri_masks(n):
    row = lax.broadcasted_iota(jnp.int32, (n, n), 0)
    col = lax.broadcasted_iota(jnp.int32, (n, n), 1)
    return row, col


def _chunk_tri(n):
    row, col = _tri_masks(n)
    shift = CHUNK.bit_length() - 1
    return (((row >> shift) == (col >> shift)) & (col <= row)).astype(BF16)


def _macaron_half_step(x, gpre_ref, gpost_ref, win_ref, wout_ref):
    xn = _rms(x, gpre_ref[...]).astype(BF16)
    acc = None
    for lo, hi in zip(FF_SPLITS[:-1], FF_SPLITS[1:]):
        gate = jnp.dot(xn, win_ref[:, lo:hi], preferred_element_type=F32)
        up = jnp.dot(xn, win_ref[:, D_FF + lo:D_FF + hi], preferred_element_type=F32)
        act = (gate * jax.nn.sigmoid(gate) * up).astype(BF16)
        part = jnp.dot(act, wout_ref[lo:hi, :], preferred_element_type=F32)
        acc = part if acc is None else acc + part
    return x + 0.5 * _rms(acc, gpost_ref[...])


def _pre_mix_body(x_ref, gpre_ref, gpost_ref, win_ref, wout_ref, gmix_ref, wa_ref, wm_ref, wg_ref, wr_ref,
                  x_out_ref, oa_ref, om_ref, og_ref, or_ref):
    x = _macaron_half_step(x_ref[...], gpre_ref, gpost_ref, win_ref, wout_ref)
    x_out_ref[...] = x
    h = _rms(x, gmix_ref[...]).astype(BF16)
    oa_ref[...] = jnp.dot(h, wa_ref[...], preferred_element_type=F32)
    om_ref[...] = jnp.dot(h, wm_ref[...], preferred_element_type=F32)
    og_ref[...] = jnp.dot(h, wg_ref[...], preferred_element_type=F32)
    or_ref[...] = jnp.dot(h, wr_ref[...], preferred_element_type=F32)


def _pre_mix(x, gpre, gpost, w_ffn_in, w_ffn_out, gmix, wa, wm, wg, wr, layer, tm):
    s = x.shape[0]
    row = lambda i: (i, 0)
    lay3 = lambda i: (layer, 0, 0)
    widths = (ATTN_COLS, MLSTM_MAIN, MLSTM_GATES, RWKV_COLS)
    return pl.pallas_call(
        _pre_mix_body,
        out_shape=[jax.ShapeDtypeStruct((s, D_MODEL), F32)] + [jax.ShapeDtypeStruct((s, w), F32) for w in widths],
        grid=(s // tm,),
        in_specs=[
            pl.BlockSpec((tm, D_MODEL), row),
            _layer_vec(D_MODEL, layer),
            _layer_vec(D_MODEL, layer),
            _resident((None, D_MODEL, 2 * D_FF), lay3),
            _resident((None, D_FF, D_MODEL), lay3),
            _layer_vec(D_MODEL, layer),
        ] + [_resident((None, D_MODEL, w), lay3) for w in widths],
        out_specs=[pl.BlockSpec((tm, D_MODEL), row)] + [pl.BlockSpec((tm, w), row) for w in widths],
        compiler_params=_params(("parallel",)),
        name="pre_mix",
    )(x, gpre, gpost, w_ffn_in, w_ffn_out, gmix, wa, wm, wg, wr)


def _post_mix_body(ya_ref, ym_ref, yr_ref, x_ref, p_ref, wo_ref, gmix_ref, gpre_ref, gpost_ref, win_ref, wout_ref,
                   gple_pre_ref, gple_post_ref, wgate_ref, wemb_ref, o_ref):
    mix = jnp.dot(ya_ref[...], wo_ref[0:ATTN_W, :], preferred_element_type=F32)
    mix += jnp.dot(ym_ref[...], wo_ref[ATTN_W:ATTN_W + MLSTM_W, :], preferred_element_type=F32)
    mix += jnp.dot(yr_ref[...], wo_ref[ATTN_W + MLSTM_W:, :], preferred_element_type=F32)
    x = x_ref[...] + _rms(mix, gmix_ref[...])
    x = _macaron_half_step(x, gpre_ref, gpost_ref, win_ref, wout_ref)
    gate = jax.nn.sigmoid(jnp.dot(_rms(x, gple_pre_ref[...]).astype(BF16), wgate_ref[...],
                                  preferred_element_type=F32))
    emb = jnp.dot(p_ref[...].astype(BF16), wemb_ref[...], preferred_element_type=F32)
    o_ref[...] = x + _rms(gate * emb, gple_post_ref[...])


def _post_mix(ya, ym, yr, x, p, w_out, gmix, gpre, gpost, w_ffn_in, w_ffn_out, gple_pre, gple_post,
              w_gate, w_emb, layer, tm):
    s = x.shape[0]
    row = lambda i: (i, 0)
    lay3 = lambda i: (layer, 0, 0)
    vec = _layer_vec(D_MODEL, layer)
    return pl.pallas_call(
        _post_mix_body,
        out_shape=jax.ShapeDtypeStruct((s, D_MODEL), F32),
        grid=(s // tm,),
        in_specs=[
            pl.BlockSpec((tm, ATTN_W), row),
            pl.BlockSpec((tm, MLSTM_W), row),
            pl.BlockSpec((tm, RWKV_W), row),
            pl.BlockSpec((tm, D_MODEL), row),
            pl.BlockSpec((None, tm, D_PLE), lambda i: (layer, i, 0)),
            _resident((None, D_MODEL, D_MODEL), lay3),
            vec, vec, vec,
            _resident((None, D_MODEL, 2 * D_FF), lay3),
            _resident((None, D_FF, D_MODEL), lay3),
            vec, vec,
            _resident((None, D_MODEL, D_MODEL), lay3),
            _resident((None, D_PLE, D_MODEL), lay3),
        ],
        out_specs=pl.BlockSpec((tm, D_MODEL), row),
        compiler_params=_params(("parallel",)),
        name="post_mix",
    )(ya, ym, yr, x, p, w_out, gmix, gpre, gpost, w_ffn_in, w_ffn_out, gple_pre, gple_post, w_gate, w_emb)


def _rope_body(pos_ref, invf_ref, cos_ref, sin_ref):
    ang = pos_ref[...].astype(F32) * invf_ref[...]
    sin = jnp.sin(ang)
    dim = lax.broadcasted_iota(jnp.int32, ang.shape, 1) & (HEAD_DIM - 1)
    cos_ref[...] = jnp.cos(ang)
    sin_ref[...] = jnp.where(dim < ROPE_HALF, -sin, jnp.where(dim < ROPE_DIM, sin, 0.0))


def _rope_tables(positions, tm):
    s = positions.shape[0]
    lane = jnp.arange(2 * HEAD_DIM) % HEAD_DIM
    freq = ROPE_THETA ** (-jnp.arange(0, ROPE_DIM, 2, dtype=F32) / ROPE_DIM)
    invf = jnp.where(lane < ROPE_DIM, freq[lane % ROPE_HALF], 0.0).astype(F32)[None, :]
    row = lambda i: (i, 0)
    out = jax.ShapeDtypeStruct((s, 2 * HEAD_DIM), F32)
    return pl.pallas_call(
        _rope_body,
        out_shape=[out, out],
        grid=(s // tm,),
        in_specs=[pl.BlockSpec((tm, 1), row), pl.BlockSpec((1, 2 * HEAD_DIM), lambda i: (0, 0))],
        out_specs=[pl.BlockSpec((tm, 2 * HEAD_DIM), row)] * 2,
        compiler_params=_params(("parallel",)),
        name="rope_tables",
    )(positions, invf)


ATTN_BLOCKS = 4
GROUP_W = ATTN_GROUP * HEAD_DIM
KEYS = 2 * WINDOW


def _rope_partner():
    row, col = _tri_masks(2 * HEAD_DIM)
    dim = col & (HEAD_DIM - 1)
    first = (dim < ROPE_HALF) & (row == col + ROPE_HALF)
    second = (dim >= ROPE_HALF) & (dim < ROPE_DIM) & (row == col - ROPE_HALF)
    return (first | second).astype(BF16)


def _attn_body(sinks_ref, p_ref, cos_ref, sin_ref, o_ref, ktprev_ref, vprev_ref, *, nblk):
    step = pl.program_id(0)

    @pl.when(step == 0)
    def _():
        ktprev_ref[...] = jnp.zeros_like(ktprev_ref)
        vprev_ref[...] = jnp.zeros_like(vprev_ref)

    cos, sin = cos_ref[...], sin_ref[...]
    partner = _rope_partner()
    period = 2 * HEAD_DIM

    def rope(x):
        slabs = [x[:, c:c + period] for c in range(0, x.shape[1], period)]
        return jnp.concatenate(
            [s * cos + jnp.dot(s.astype(BF16), partner, preferred_element_type=F32) * sin for s in slabs], axis=1)

    q = (rope(p_ref[:, 0:ATTN_W]) * (HEAD_DIM ** -0.5)).astype(BF16)
    k_cur = rope(p_ref[:, ATTN_W:ATTN_W + KV_W])
    v_cur = p_ref[:, ATTN_W + KV_W:ATTN_COLS].astype(BF16)
    kt = jnp.concatenate([ktprev_ref[...], k_cur.T.astype(BF16)], axis=1)
    vv = jnp.concatenate([vprev_ref[...], v_cur], axis=0)
    ktprev_ref[...] = kt[:, nblk * WINDOW:]
    vprev_ref[...] = vv[nblk * WINDOW:, :]

    t = lax.broadcasted_iota(jnp.int32, (WINDOW, ATTN_GROUP * KEYS), 0)
    lane = lax.broadcasted_iota(jnp.int32, (WINDOW, ATTN_GROUP * KEYS), 1)
    c = lane & (KEYS - 1)
    head = lane >> (KEYS.bit_length() - 1)
    cur_ok = (c >= WINDOW) & (c - WINDOW <= t)
    prev_ok = (c < WINDOW) & (c > t)

    def bias(g, has_prev):
        b = jnp.where(cur_ok | (prev_ok & has_prev), 0.0, NEG_INF)
        sink = jnp.zeros_like(b)
        for n in range(ATTN_GROUP):
            sink = jnp.where(head == n, sinks_ref[g * ATTN_GROUP + n], sink)
        return jnp.where(c == 0, sink, b)

    bias_first = [bias(g, step > 0) for g in range(ATTN_KV_HEADS)]
    bias_rest = [bias(g, True) for g in range(ATTN_KV_HEADS)]

    key_lane = lax.broadcasted_iota(jnp.int32, (HEAD_DIM, KEYS), 1)
    zero_kt = jnp.zeros((HEAD_DIM, KEYS), BF16)
    vrow = lax.broadcasted_iota(jnp.int32, (KEYS, GROUP_W), 0)
    vlane_head = lax.broadcasted_iota(jnp.int32, (KEYS, GROUP_W), 1) >> (HEAD_DIM.bit_length() - 1)
    spread_row = lax.broadcasted_iota(jnp.int32, (KV_W, GROUP_W), 0)
    spread_col = lax.broadcasted_iota(jnp.int32, (KV_W, GROUP_W), 1)

    units = [(b, g) for b in range(nblk) for g in range(ATTN_KV_HEADS)]
    kbd, vo = [], []
    for b, g in units:
        ktg = kt[g * HEAD_DIM:(g + 1) * HEAD_DIM, b * WINDOW:b * WINDOW + KEYS]
        ktg = jnp.where(key_lane == 0, jnp.zeros_like(ktg), ktg)
        kbd.append(jnp.concatenate(
            [jnp.concatenate([ktg if m == n else zero_kt for m in range(ATTN_GROUP)], axis=1)
             for n in range(ATTN_GROUP)], axis=0))
        spread = (spread_row == g * HEAD_DIM + (spread_col & (HEAD_DIM - 1))).astype(BF16)
        vt = jnp.dot(vv[b * WINDOW:b * WINDOW + KEYS, :], spread, preferred_element_type=F32).astype(BF16)
        vt = jnp.where(vrow == 0, jnp.zeros_like(vt), vt)
        vbd = jnp.concatenate(
            [jnp.where(vlane_head == n, vt, jnp.zeros_like(vt)) for n in range(ATTN_GROUP)], axis=0)
        ones = jnp.concatenate([(vlane_head == n).astype(BF16) for n in range(ATTN_GROUP)], axis=0)
        vo.append(jnp.concatenate([vbd, ones], axis=1))

    scores = [jnp.dot(q[b * WINDOW:(b + 1) * WINDOW, g * GROUP_W:(g + 1) * GROUP_W], kb,
                      preferred_element_type=F32) + (bias_first[g] if b == 0 else bias_rest[g])
              for (b, g), kb in zip(units, kbd)]
    probs = []
    for s in scores:
        parts = []
        for n in range(ATTN_GROUP):
            sn = s[:, n * KEYS:(n + 1) * KEYS]
            parts.append(jnp.exp(sn - jnp.max(sn, axis=-1, keepdims=True)).astype(BF16))
        probs.append(jnp.concatenate(parts, axis=1))
    outs = [jnp.dot(pr, x, preferred_element_type=F32) for pr, x in zip(probs, vo)]
    for (b, g), od in zip(units, outs):
        o_ref[b * WINDOW:(b + 1) * WINDOW, g * GROUP_W:(g + 1) * GROUP_W] = (
            od[:, :GROUP_W] / od[:, GROUP_W:]).astype(o_ref.dtype)


def _attention(sinks, attn_p, cos, sin, layer):
    s = attn_p.shape[0]
    nblk = min(ATTN_BLOCKS, s // WINDOW)
    tq = nblk * WINDOW
    row = lambda i: (i, 0)
    return pl.pallas_call(
        functools.partial(_attn_body, nblk=nblk),
        out_shape=jax.ShapeDtypeStruct((s, ATTN_W), BF16),
        grid=(s // tq,),
        in_specs=[
            pl.BlockSpec(memory_space=pltpu.SMEM),
            pl.BlockSpec((tq, ATTN_COLS), row),
            pl.BlockSpec((tq, 2 * HEAD_DIM), row),
            pl.BlockSpec((tq, 2 * HEAD_DIM), row),
        ],
        out_specs=pl.BlockSpec((tq, ATTN_W), row),
        scratch_shapes=[pltpu.VMEM((KV_W, WINDOW), BF16), pltpu.VMEM((WINDOW, KV_W), BF16)],
        compiler_params=_params(("arbitrary",)),
        name="swa_attention",
    )(sinks[layer], attn_p, cos, sin)


def _mlstm_body(p_ref, gates_ref, conv_ref, bias_ref, norm_ref, o_ref, xpad_ref, c_ref, n_ref, *, tb):
    step = pl.program_id(0)

    @pl.when(step == 0)
    def _():
        xpad_ref[0:SUBLANES, :] = jnp.zeros((SUBLANES, 2 * MLSTM_W), F32)
        c_ref[...] = jnp.zeros_like(c_ref)
        n_ref[...] = jnp.zeros_like(n_ref)

    xpad_ref[SUBLANES:SUBLANES + tb, :] = p_ref[:, 0:2 * MLSTM_W]
    conv = None
    for tap in range(MLSTM_CONV):
        shifted = xpad_ref[pl.ds(SUBLANES - (MLSTM_CONV - 1) + tap, tb), :]
        term = shifted * conv_ref[tap:tap + 1, :]
        conv = term if conv is None else conv + term
    xpad_ref[0:SUBLANES, :] = xpad_ref[tb:tb + SUBLANES, :]
    qk = conv * jax.nn.sigmoid(conv)
    q_all = qk[:, 0:MLSTM_W] * (HEAD_DIM ** -0.5)
    k_all = qk[:, MLSTM_W:2 * MLSTM_W]

    pre = GATE_CAP * jnp.tanh((gates_ref[...] + bias_ref[...]) / GATE_CAP)
    logsig = jnp.minimum(pre, 0.0) - jnp.log(1.0 + jnp.exp(-jnp.abs(pre)))

    row, col = _tri_masks(CHUNK)
    causal = col <= row
    eye = col == row
    g_cum = _cumsum_rows(_chunk_tri(tb), logsig)

    gate_col = lax.broadcasted_iota(jnp.int32, (tb, MLSTM_GATES), 1)
    gate_lanes = _split_dot(jnp.where(gate_col < MLSTM_HEADS, pre, g_cum), _gate_spread())
    i_all, g_all = gate_lanes[:, 0:MLSTM_W], gate_lanes[:, MLSTM_W:2 * MLSTM_W]
    head_ones = _head_ones(MLSTM_W)

    def to_row(lane_bcast):
        return jnp.sum(jnp.where(eye, lane_bcast, 0.0), axis=0, keepdims=True)

    chunk_starts = range(0, tb, CHUNK)
    units = [(c0, h) for c0 in chunk_starts for h in range(MLSTM_HEADS)]
    head_lanes = lambda h: slice(h * HEAD_DIM, (h + 1) * HEAD_DIM)
    unit_of = lambda x: [x[c0:c0 + CHUNK, head_lanes(h)] for c0, h in units]
    chunk_of = lambda x: [x[c0:c0 + CHUNK, :] for c0 in chunk_starts]
    v_all = p_ref[:, 2 * MLSTM_W:3 * MLSTM_W]

    g_chunk, i_chunk = chunk_of(g_all), chunk_of(i_all)
    g_tot = [g[CHUNK - 1:CHUNK, :] for g in g_chunk]
    w_in = [jnp.exp(t - g + i) for t, g, i in zip(g_tot, g_chunk, i_chunk)]
    e_tot = [jnp.exp(t) for t in g_tot]
    e_g = [jnp.exp(g) for g in g_chunk]
    n_loc = [jnp.sum(w * x, axis=0, keepdims=True) for w, x in zip(w_in, chunk_of(k_all))]
    wv = unit_of(jnp.concatenate([w * x for w, x in zip(w_in, chunk_of(v_all))], axis=0))

    q, k, v = unit_of(q_all), unit_of(k_all), unit_of(v_all)
    decay = [jnp.exp(jnp.where(causal, g - to_row(g) + to_row(i), NEG_INF))
             for g, i in zip(unit_of(g_all), unit_of(i_all))]
    s_mat = [_mm_nt(a, b) * d for a, b, d in zip(q, k, decay)]
    ones = jnp.ones((CHUNK, HEAD_DIM), F32)
    num_den = [_mm(s, jnp.concatenate([x, ones], axis=1)) for s, x in zip(s_mat, v)]
    c_loc = [_mm_tn(x, b) for x, b in zip(wv, k)]

    c_state = [c_ref[h] for h in range(MLSTM_HEADS)]
    n_state = n_ref[0:1, :]
    c_in, n_in = [], []
    for u, (c0, h) in enumerate(units):
        n_chunk = u // MLSTM_HEADS
        if h == 0:
            n_in.append(n_state)
            n_state = e_tot[n_chunk] * n_state + n_loc[n_chunk]
        c_in.append(c_state[h])
        c_state[h] = e_tot[n_chunk][:, head_lanes(h)] * c_state[h] + c_loc[u]
    for h in range(MLSTM_HEADS):
        c_ref[h] = c_state[h]
    n_ref[0:1, :] = n_state

    inter = [_mm_nt(a, jnp.concatenate(
        [c, jnp.broadcast_to(n_in[u // MLSTM_HEADS][:, head_lanes(h)], (CHUNK, HEAD_DIM))], axis=0))
        for u, ((c0, h), a, c) in enumerate(zip(units, q, c_in))]
    e_g_unit = unit_of(jnp.concatenate(e_g, axis=0))
    top = [nd[:, :HEAD_DIM] + e * x[:, :HEAD_DIM] for nd, e, x in zip(num_den, e_g_unit, inter)]
    bottom = [nd[:, HEAD_DIM:] + e * x[:, HEAD_DIM:] for nd, e, x in zip(num_den, e_g_unit, inter)]
    for n, c0 in enumerate(chunk_starts):
        heads = slice(n * MLSTM_HEADS, (n + 1) * MLSTM_HEADS)
        hid = (jnp.concatenate(top[heads], axis=1)
               / jnp.maximum(jnp.abs(jnp.concatenate(bottom[heads], axis=1)), 1.0))
        mean_sq = _split_dot(hid * hid, head_ones, terms=2) * (1.0 / HEAD_DIM)
        hid = hid * lax.rsqrt(mean_sq + NORM_EPS) * norm_ref[...]
        o_gate = p_ref[c0:c0 + CHUNK, 3 * MLSTM_W:4 * MLSTM_W]
        o_ref[c0:c0 + CHUNK, :] = (jax.nn.sigmoid(o_gate) * hid).astype(o_ref.dtype)


def _mlstm(mlstm_p, gates, conv_w, bias, norm, layer, tb):
    s = mlstm_p.shape[0]
    row = lambda i: (i, 0)
    return pl.pallas_call(
        functools.partial(_mlstm_body, tb=tb),
        out_shape=jax.ShapeDtypeStruct((s, MLSTM_W), BF16),
        grid=(s // tb,),
        in_specs=[
            pl.BlockSpec((tb, MLSTM_MAIN), row),
            pl.BlockSpec((tb, MLSTM_GATES), row),
            pl.BlockSpec((None, MLSTM_CONV, 2 * MLSTM_W), lambda i: (layer, 0, 0)),
            _layer_vec(MLSTM_GATES, layer),
            _layer_vec(MLSTM_W, layer),
        ],
        out_specs=pl.BlockSpec((tb, MLSTM_W), row),
        scratch_shapes=[
            pltpu.VMEM((tb + SUBLANES, 2 * MLSTM_W), F32),
            pltpu.VMEM((MLSTM_HEADS, HEAD_DIM, HEAD_DIM), F32),
            pltpu.VMEM((SUBLANES, MLSTM_W), F32),
        ],
        compiler_params=_params(("arbitrary",)),
        name="mlstm",
    )(mlstm_p, gates, conv_w, bias, norm)


def _rwkv_chunk_terms(a_t, b_t, k_t, r_t, b_end, k_end, v):
    row, col = _tri_masks(CHUNK)
    incl = col <= row
    strict = col < row
    eye = (col == row).astype(F32)
    half = HEAD_DIM

    quad = [_mm_nt(jnp.concatenate([a, r], axis=0), jnp.concatenate([b, k], axis=0))
            for a, r, b, k in zip(a_t, r_t, b_t, k_t)]
    yield
    n_mat = [jnp.where(strict, x[:CHUNK, :CHUNK], 0.0) for x in quad]
    a_ak = [jnp.where(strict, x[:CHUNK, CHUNK:], 0.0) for x in quad]
    c_rb = [jnp.where(incl, x[CHUNK:, :CHUNK], 0.0) for x in quad]
    c_rk = [jnp.where(incl, x[CHUNK:, CHUNK:], 0.0) for x in quad]

    inv = [eye + n for n in n_mat]
    power = n_mat
    for _ in range(NEUMANN_STEPS):
        power = [_mm(x, x) for x in power]
        yield
        inv = [m + _mm(m, x) for m, x in zip(inv, power)]
        yield

    z = [_mm(x, y) for x, y in zip(a_ak, v)]
    yield
    w12 = [_mm(m, jnp.concatenate([a, y], axis=1)) for m, a, y in zip(inv, a_t, z)]
    yield
    cw = [_mm(c, w) for c, w in zip(c_rb, w12)]
    yield
    ckv = [_mm(c, y) for c, y in zip(c_rk, v)]
    yield
    gh = [_mm_tn(w, b) for w, b in zip(w12, b_end)]
    yield
    vk = [_mm_tn(y, k) for y, k in zip(v, k_end)]
    yield
    q_eff = [r + x[:, :half] for r, x in zip(r_t, cw)]
    y_loc = [x[:, half:] + y for x, y in zip(cw, ckv)]
    g_mat = [x[:half, :] for x in gh]
    h_mat = [x[half:, :] + y for x, y in zip(gh, vk)]
    return q_eff, y_loc, g_mat, h_mat


def _rwkv_body(p_ref, mu_ref, w0_ref, wup_ref, a0_ref, aup_ref, gup_ref, kk_ref, ka_ref, rk_ref,
               lnw_ref, lnb_ref, o_ref, upad_ref, s_ref, stage_ref, end_ref, *, tb):
    step = pl.program_id(0)

    @pl.when(step == 0)
    def _():
        upad_ref[0:SUBLANES, :] = jnp.zeros((SUBLANES, RWKV_COLS), F32)
        s_ref[...] = jnp.zeros_like(s_ref)
        stage_ref[...] = jnp.zeros_like(stage_ref)
        end_ref[...] = jnp.zeros_like(end_ref)

    refs = (p_ref, mu_ref, w0_ref, wup_ref, a0_ref, aup_ref, gup_ref, kk_ref, ka_ref, rk_ref,
            lnw_ref, lnb_ref, o_ref, upad_ref, s_ref, stage_ref, end_ref)
    for cur in (0, 1):
        pl.when(lax.rem(step, 2) == cur)(functools.partial(_rwkv_step, *refs, tb=tb, cur=cur, prv=1 - cur))


def _rwkv_step(p_ref, mu_ref, w0_ref, wup_ref, a0_ref, aup_ref, gup_ref, kk_ref, ka_ref, rk_ref,
               lnw_ref, lnb_ref, o_ref, upad_ref, s_ref, stage_ref, end_ref, *, tb, cur, prv):
    head_ones = _head_ones(RWKV_W)
    head_sum = lambda x: _split_dot(x, head_ones, terms=2)
    head_lanes = lambda h: slice(h * HEAD_DIM, (h + 1) * HEAD_DIM)
    chunk_starts = range(0, tb, CHUNK)
    units = [(n, c0, h) for n, c0 in enumerate(chunk_starts) for h in range(RWKV_HEADS)]

    def finish_previous():
        staged = {name: stage_ref[prv, n] for n, name in enumerate(RWKV_STAGED)}
        ends = end_ref[prv]
        per_unit = {name: [staged[name][c0:c0 + CHUNK, head_lanes(h)] for _, c0, h in units]
                    for name in RWKV_STAGED[:7]}
        q_eff, y_loc, g_mat, h_mat = yield from _rwkv_chunk_terms(**per_unit)
        state = [s_ref[h] for h in range(RWKV_HEADS)]
        ys = []
        for u, (n, c0, h) in enumerate(units):
            ys.append(_mm_nt(q_eff[u], state[h]) + y_loc[u])
            state[h] = state[h] * ends[n:n + 1, head_lanes(h)] + _mm(state[h], g_mat[u]) + h_mat[u]
            if h == RWKV_HEADS - 1:
                yield
        for h in range(RWKV_HEADS):
            s_ref[h] = state[h]
        y_all = jnp.concatenate([jnp.concatenate(ys[n:n + RWKV_HEADS], axis=1)
                                 for n in range(0, len(ys), RWKV_HEADS)], axis=0)
        centred = y_all - head_sum(y_all) * (1.0 / HEAD_DIM)
        var = head_sum(centred * centred) * (1.0 / HEAD_DIM)
        normed = centred * lax.rsqrt(var + RWKV_GN_EPS) * lnw_ref[...] + lnb_ref[...]
        o_ref[...] = ((normed + staged["bonus"] * staged["v"]) * staged["gate"]).astype(o_ref.dtype)

    def stage_current():
        upad_ref[SUBLANES:SUBLANES + tb, :] = p_ref[...]
        yield
        for r0 in range(0, tb, RWKV_SLAB):
            rows = slice(r0, r0 + RWKV_SLAB)

            def stage(name, value):
                stage_ref[cur, RWKV_STAGED.index(name), rows, :] = value

            u = p_ref[rows, :]
            prev = upad_ref[pl.ds(SUBLANES - 1 + r0, RWKV_SLAB), :]
            u = u + (prev - u) * mu_ref[...]
            yield
            r_all = u[:, 0:RWKV_W]
            k_raw = u[:, RWKV_W:2 * RWKV_W]
            v_all = u[:, 2 * RWKV_W:3 * RWKV_W]
            x_w = u[:, 3 * RWKV_W:3 * RWKV_W + RWKV_W_RANK]
            x_a = u[:, 3 * RWKV_W + RWKV_W_RANK:3 * RWKV_W + RWKV_W_RANK + RWKV_A_RANK]
            x_g = u[:, 3 * RWKV_W + RWKV_W_RANK + RWKV_A_RANK:RWKV_COLS]
            z = w0_ref[...] + _mm(jnp.tanh(x_w), wup_ref[...])
            ld_all = -jnp.exp(-0.5) * jax.nn.sigmoid(z)
            a_all = jax.nn.sigmoid(a0_ref[...] + _mm(x_a, aup_ref[...]))
            yield
            kk_all = k_raw * kk_ref[...]
            k_all = k_raw * (1.0 + (a_all - 1.0) * ka_ref[...])
            kk_all = kk_all * lax.rsqrt(jnp.maximum(head_sum(kk_all * kk_all), 1e-24))
            a_vec = -kk_all
            b_vec = kk_all * a_all
            yield
            lp_all = _cumsum_rows(_chunk_tri(RWKV_SLAB), ld_all)
            stage("v", v_all)
            stage("bonus", head_sum(r_all * k_all * rk_ref[...]))
            stage("gate", _mm(jax.nn.sigmoid(x_g), gup_ref[...]))
            yield
            starts = range(0, RWKV_SLAB, CHUNK)
            lp_last = jnp.concatenate(
                [jnp.broadcast_to(lp_all[c0 + CHUNK - 1:c0 + CHUNK, :], (CHUNK, RWKV_W)) for c0 in starts], axis=0)
            grow = jnp.exp(-lp_all)
            to_end = jnp.exp(lp_last - lp_all)
            stage("a_t", a_vec * jnp.exp(lp_all - ld_all))
            stage("b_t", b_vec * grow)
            stage("k_t", k_all * grow)
            yield
            stage("r_t", r_all * jnp.exp(lp_all))
            stage("b_end", b_vec * to_end)
            stage("k_end", k_all * to_end)
            for c0 in starts:
                n = (r0 + c0) // CHUNK
                end_ref[cur, n:n + 1, :] = jnp.exp(lp_all[c0 + CHUNK - 1:c0 + CHUNK, :])
            yield
        upad_ref[0:SUBLANES, :] = upad_ref[tb:tb + SUBLANES, :]

    _interleave(finish_previous(), stage_current())


def _rwkv(rwkv_p, mu, w0, w_up, a0, a_up, g_up, k_k, k_a, r_k, ln_w, ln_b, layer, tb):
    s = rwkv_p.shape[0]
    blocks = s // tb
    lay3 = lambda i: (layer, 0, 0)
    vec = lambda width: _layer_vec(width, layer)
    return pl.pallas_call(
        functools.partial(_rwkv_body, tb=tb),
        out_shape=jax.ShapeDtypeStruct((s, RWKV_W), BF16),
        grid=(blocks + 1,),
        in_specs=[
            pl.BlockSpec((tb, RWKV_COLS), lambda i: (jnp.minimum(i, blocks - 1), 0)),
            vec(RWKV_COLS), vec(RWKV_W),
            pl.BlockSpec((None, RWKV_W_RANK, RWKV_W), lay3),
            vec(RWKV_W),
            pl.BlockSpec((None, RWKV_A_RANK, RWKV_W), lay3),
            pl.BlockSpec((None, RWKV_G_RANK, RWKV_W), lay3),
            vec(RWKV_W), vec(RWKV_W), vec(RWKV_W), vec(RWKV_W), vec(RWKV_W),
        ],
        out_specs=pl.BlockSpec((tb, RWKV_W), lambda i: (jnp.maximum(i - 1, 0), 0)),
        scratch_shapes=[
            pltpu.VMEM((tb + SUBLANES, RWKV_COLS), F32),
            pltpu.VMEM((RWKV_HEADS, HEAD_DIM, HEAD_DIM), F32),
            pltpu.VMEM((2, len(RWKV_STAGED), tb, RWKV_W), F32),
            pltpu.VMEM((2, tb // CHUNK, RWKV_W), F32),
        ],
        compiler_params=_params(("arbitrary",)),
        name="rwkv7",
    )(rwkv_p, mu, w0, w_up, a0, a_up, g_up, k_k, k_a, r_k, ln_w, ln_b)


def _tiles(s):
    return min(512, s), min(512, s)


def kernel(x, p, positions, ln_ffn1_pre, ln_ffn1_post, w_ffn1_in, w_ffn1_out, ln_mix_pre, w_in, attn_sinks, mlstm_conv, mlstm_i_bias, mlstm_f_bias, mlstm_norm, rwkv_mu, rwkv_w0, rwkv_w_up, rwkv_a0, rwkv_a_up, rwkv_g_up, rwkv_k_k, rwkv_k_a, rwkv_r_k, rwkv_ln_w, rwkv_ln_b, w_out, ln_mix_post, ln_ffn2_pre, ln_ffn2_post, w_ffn2_in, w_ffn2_out, ln_ple_pre, w_ple_gate, w_ple_proj, ln_ple_post):
    batch, seq, _ = x.shape
    assert batch == 1 and seq % WINDOW == 0
    depth = w_in.shape[0]
    tm, tb = _tiles(seq)

    bf = lambda w: w.astype(BF16)
    w_ffn1_in, w_ffn1_out, w_ffn2_in, w_ffn2_out = map(bf, (w_ffn1_in, w_ffn1_out, w_ffn2_in, w_ffn2_out))
    w_out, w_ple_gate, w_ple_proj = map(bf, (w_out, w_ple_gate, w_ple_proj))
    rwkv_w_up, rwkv_a_up, rwkv_g_up = map(bf, (rwkv_w_up, rwkv_a_up, rwkv_g_up))
    w_in = bf(w_in)
    m0 = ATTN_COLS
    g0 = m0 + MLSTM_MAIN
    r0 = g0 + MLSTM_GATES
    w_attn, w_mlstm, w_gates, w_rwkv = w_in[:, :, :m0], w_in[:, :, m0:g0], w_in[:, :, g0:r0], w_in[:, :, r0:]
    vec = lambda a: a.reshape(depth, 1, -1)
    gate_bias = vec(jnp.concatenate([mlstm_i_bias, mlstm_f_bias], axis=-1))
    (ln_ffn1_pre, ln_ffn1_post, ln_mix_pre, ln_mix_post, ln_ffn2_pre, ln_ffn2_post, ln_ple_pre,
     ln_ple_post, mlstm_norm, rwkv_mu, rwkv_w0, rwkv_a0, rwkv_k_k, rwkv_k_a, rwkv_r_k, rwkv_ln_w,
     rwkv_ln_b) = map(vec, (
         ln_ffn1_pre, ln_ffn1_post, ln_mix_pre, ln_mix_post, ln_ffn2_pre, ln_ffn2_post, ln_ple_pre,
         ln_ple_post, mlstm_norm, rwkv_mu, rwkv_w0, rwkv_a0, rwkv_k_k, rwkv_k_a, rwkv_r_k, rwkv_ln_w,
         rwkv_ln_b))

    cos, sin = _rope_tables(positions.reshape(seq, 1), tm)
    xs = x.reshape(seq, D_MODEL)
    for l in range(depth):
        xs, attn_p, mlstm_p, gates, rwkv_p = _pre_mix(
            xs, ln_ffn1_pre, ln_ffn1_post, w_ffn1_in, w_ffn1_out, ln_mix_pre, w_attn, w_mlstm, w_gates, w_rwkv, l, tm)
        y_attn = _attention(attn_sinks, attn_p, cos, sin, l)
        y_mlstm = _mlstm(mlstm_p, gates, mlstm_conv, gate_bias, mlstm_norm, l, tb)
        y_rwkv = _rwkv(rwkv_p, rwkv_mu, rwkv_w0, rwkv_w_up, rwkv_a0, rwkv_a_up, rwkv_g_up,
                       rwkv_k_k, rwkv_k_a, rwkv_r_k, rwkv_ln_w, rwkv_ln_b, l, tb)
        xs = _post_mix(y_attn, y_mlstm, y_rwkv, xs, p.reshape(depth, seq, D_PLE), w_out, ln_mix_post,
                       ln_ffn2_pre, ln_ffn2_post, w_ffn2_in, w_ffn2_out, ln_ple_pre, ln_ple_post,
                       w_ple_gate, w_ple_proj, l, tm)
    return xs.reshape(batch, seq, D_MODEL)
```

```python
import functools

import jax
import jax.numpy as jnp
from jax import lax
from jax.experimental import pallas as pl
from jax.experimental.pallas import tpu as pltpu

F32 = jnp.float32
BF16 = jnp.bfloat16

D_MODEL = 1024
HEAD_DIM = 64
D_FF = 2816
D_PLE = 256
ATTN_Q_HEADS = 8
ATTN_KV_HEADS = 2
ATTN_GROUP = ATTN_Q_HEADS // ATTN_KV_HEADS
WINDOW = 128
ROPE_THETA = 500000.0
ROPE_DIM = HEAD_DIM // 4
ROPE_HALF = ROPE_DIM // 2
MLSTM_HEADS = 4
MLSTM_CONV = 4
GATE_CAP = 15.0
RWKV_HEADS = 4
RWKV_W_RANK = 64
RWKV_A_RANK = 64
RWKV_G_RANK = 128
RWKV_GN_EPS = 64e-5
NORM_EPS = 1e-6
NEG_INF = -1e30

ATTN_W = ATTN_Q_HEADS * HEAD_DIM
KV_W = ATTN_KV_HEADS * HEAD_DIM
MLSTM_W = MLSTM_HEADS * HEAD_DIM
RWKV_W = RWKV_HEADS * HEAD_DIM
ATTN_COLS = ATTN_W + 2 * KV_W
MLSTM_MAIN = 4 * MLSTM_W
MLSTM_GATES = 2 * MLSTM_HEADS
RWKV_COLS = 3 * RWKV_W + RWKV_W_RANK + RWKV_A_RANK + RWKV_G_RANK

CHUNK = 64
NEUMANN_STEPS = 5
DENSE_HALF_MIN = 256
RWKV_SLAB = 128
RWKV_STAGED = ("a_t", "b_t", "k_t", "r_t", "b_end", "k_end", "v", "bonus", "gate")
MXU_TILE = 256
FF_SPLITS = (0, 6 * MXU_TILE, D_FF)
SUBLANES = 8
V7X_VMEM_LIMIT = 56 * 1024 * 1024


def _params(semantics):
    return pltpu.CompilerParams(dimension_semantics=semantics, vmem_limit_bytes=V7X_VMEM_LIMIT)


def _resident(block_shape, index_map):
    return pl.BlockSpec(block_shape, index_map, pipeline_mode=pl.Buffered(1))


def _layer_vec(width, layer):
    return pl.BlockSpec((None, 1, width), lambda i: (layer, 0, 0))


def _rms(x, gain):
    return x * lax.rsqrt(jnp.mean(x * x, axis=-1, keepdims=True) + NORM_EPS) * gain


def _mm(a, b):
    return jnp.dot(a.astype(BF16), b.astype(BF16), preferred_element_type=F32)


def _mm_nt(a, b):
    return lax.dot_general(a.astype(BF16), b.astype(BF16), (((1,), (1,)), ((), ())),
                           preferred_element_type=F32)


def _mm_tn(a, b):
    return lax.dot_general(a.astype(BF16), b.astype(BF16), (((0,), (0,)), ((), ())),
                           preferred_element_type=F32)


def _bf16_terms(x, terms):
    out, rest = [], x
    for n in range(terms):
        part = rest.astype(BF16)
        out.append(part)
        if n + 1 < terms:
            rest = rest - part.astype(F32)
    return out


def _cumsum_rows(tri, x):
    return sum(jnp.dot(tri, term, preferred_element_type=F32) for term in _bf16_terms(x, 3))


def _split_dot(x, ones, terms=3):
    return sum(jnp.dot(term, ones, preferred_element_type=F32) for term in _bf16_terms(x, terms))


def _head_ones(width):
    row, col = _tri_masks(width)
    shift = HEAD_DIM.bit_length() - 1
    return ((row >> shift) == (col >> shift)).astype(BF16)


def _gate_spread():
    lanes = MLSTM_GATES * HEAD_DIM
    row = lax.broadcasted_iota(jnp.int32, (MLSTM_GATES, lanes), 0)
    col = lax.broadcasted_iota(jnp.int32, (MLSTM_GATES, lanes), 1)
    return (row == (col >> (HEAD_DIM.bit_length() - 1))).astype(BF16)


def _interleave(*gens):
    results = [None] * len(gens)
    live = dict(enumerate(gens))
    while live:
        for n in list(live):
            try:
                next(live[n])
            except StopIteration as stop:
                results[n] = stop.value
                del live[n]
    return results


def _tri_masks(n):
    row = lax.broadcasted_iota(jnp.int32, (n, n), 0)
    col = lax.broadcasted_iota(jnp.int32, (n, n), 1)
    return row, col


def _chunk_tri(n):
    row, col = _tri_masks(n)
    shift = CHUNK.bit_length() - 1
    return (((row >> shift) == (col >> shift)) & (col <= row)).astype(BF16)


def _macaron_half_step(x, gpre_ref, gpost_ref, win_ref, wout_ref):
    xn = _rms(x, gpre_ref[...]).astype(BF16)
    yield
    splits = list(zip(FF_SPLITS[:-1], FF_SPLITS[1:]))
    gate_up = []
    acc = None
    for n in range(len(splits) + 1):
        if n < len(splits):
            lo, hi = splits[n]
            gate_up.append((jnp.dot(xn, win_ref[:, lo:hi], preferred_element_type=F32),
                            jnp.dot(xn, win_ref[:, D_FF + lo:D_FF + hi], preferred_element_type=F32)))
        if n > 0:
            lo, hi = splits[n - 1]
            gate, up = gate_up[n - 1]
            act = (gate * jax.nn.sigmoid(gate) * up).astype(BF16)
            yield
            part = jnp.dot(act, wout_ref[lo:hi, :], preferred_element_type=F32)
            acc = part if acc is None else acc + part
        yield
    return x + 0.5 * _rms(acc, gpost_ref[...])


def _two_row_halves(rows, tile):
    if rows < 2 * DENSE_HALF_MIN:
        return _interleave(tile(0, rows))
    parts = rows // DENSE_HALF_MIN
    size = rows // parts

    def delayed(gen, phases):
        for _ in range(phases):
            yield
        yield from gen

    return _interleave(*[delayed(tile(n * size, size), n) for n in range(parts)])


def _pre_mix_body(x_ref, gpre_ref, gpost_ref, win_ref, wout_ref, gmix_ref, wa_ref, wm_ref, wg_ref, wr_ref,
                  x_out_ref, oa_ref, om_ref, og_ref, or_ref):
    def tile(r0, n):
        rows = slice(r0, r0 + n)
        x = yield from _macaron_half_step(x_ref[rows, :], gpre_ref, gpost_ref, win_ref, wout_ref)
        x_out_ref[rows, :] = x
        h = _rms(x, gmix_ref[...]).astype(BF16)
        yield
        oa_ref[rows, :] = jnp.dot(h, wa_ref[...], preferred_element_type=F32)
        om_ref[rows, :] = jnp.dot(h, wm_ref[...], preferred_element_type=F32)
        og_ref[rows, :] = jnp.dot(h, wg_ref[...], preferred_element_type=F32)
        or_ref[rows, :] = jnp.dot(h, wr_ref[...], preferred_element_type=F32)

    _two_row_halves(x_ref.shape[0], tile)


def _pre_mix(x, gpre, gpost, w_ffn_in, w_ffn_out, gmix, wa, wm, wg, wr, layer, tm):
    s = x.shape[0]
    row = lambda i: (i, 0)
    lay3 = lambda i: (layer, 0, 0)
    widths = (ATTN_COLS, MLSTM_MAIN, MLSTM_GATES, RWKV_COLS)
    return pl.pallas_call(
        _pre_mix_body,
        out_shape=[jax.ShapeDtypeStruct((s, D_MODEL), F32)] + [jax.ShapeDtypeStruct((s, w), F32) for w in widths],
        grid=(s // tm,),
        in_specs=[
            pl.BlockSpec((tm, D_MODEL), row),
            _layer_vec(D_MODEL, layer),
            _layer_vec(D_MODEL, layer),
            _resident((None, D_MODEL, 2 * D_FF), lay3),
            _resident((None, D_FF, D_MODEL), lay3),
            _layer_vec(D_MODEL, layer),
        ] + [_resident((None, D_MODEL, w), lay3) for w in widths],
        out_specs=[pl.BlockSpec((tm, D_MODEL), row)] + [pl.BlockSpec((tm, w), row) for w in widths],
        compiler_params=_params(("parallel",)),
        name="pre_mix",
    )(x, gpre, gpost, w_ffn_in, w_ffn_out, gmix, wa, wm, wg, wr)


def _post_mix_body(ya_ref, ym_ref, yr_ref, x_ref, p_ref, wo_ref, gmix_ref, gpre_ref, gpost_ref, win_ref, wout_ref,
                   gple_pre_ref, gple_post_ref, wgate_ref, wemb_ref, o_ref):
    def tile(r0, n):
        rows = slice(r0, r0 + n)
        mix = jnp.dot(ya_ref[rows, :], wo_ref[0:ATTN_W, :], preferred_element_type=F32)
        mix += jnp.dot(ym_ref[rows, :], wo_ref[ATTN_W:ATTN_W + MLSTM_W, :], preferred_element_type=F32)
        mix += jnp.dot(yr_ref[rows, :], wo_ref[ATTN_W + MLSTM_W:, :], preferred_element_type=F32)
        emb = jnp.dot(p_ref[rows, :].astype(BF16), wemb_ref[...], preferred_element_type=F32)
        yield
        x = x_ref[rows, :] + _rms(mix, gmix_ref[...])
        x = yield from _macaron_half_step(x, gpre_ref, gpost_ref, win_ref, wout_ref)
        h = _rms(x, gple_pre_ref[...]).astype(BF16)
        yield
        gate = jax.nn.sigmoid(jnp.dot(h, wgate_ref[...], preferred_element_type=F32))
        yield
        o_ref[rows, :] = x + _rms(gate * emb, gple_post_ref[...])

    _two_row_halves(x_ref.shape[0], tile)


def _post_mix(ya, ym, yr, x, p, w_out, gmix, gpre, gpost, w_ffn_in, w_ffn_out, gple_pre, gple_post,
              w_gate, w_emb, layer, tm):
    s = x.shape[0]
    row = lambda i: (i, 0)
    lay3 = lambda i: (layer, 0, 0)
    vec = _layer_vec(D_MODEL, layer)
    return pl.pallas_call(
        _post_mix_body,
        out_shape=jax.ShapeDtypeStruct((s, D_MODEL), F32),
        grid=(s // tm,),
        in_specs=[
            pl.BlockSpec((tm, ATTN_W), row),
            pl.BlockSpec((tm, MLSTM_W), row),
            pl.BlockSpec((tm, RWKV_W), row),
            pl.BlockSpec((tm, D_MODEL), row),
            pl.BlockSpec((None, tm, D_PLE), lambda i: (layer, i, 0)),
            _resident((None, D_MODEL, D_MODEL), lay3),
            vec, vec, vec,
            _resident((None, D_MODEL, 2 * D_FF), lay3),
            _resident((None, D_FF, D_MODEL), lay3),
            vec, vec,
            _resident((None, D_MODEL, D_MODEL), lay3),
            _resident((None, D_PLE, D_MODEL), lay3),
        ],
        out_specs=pl.BlockSpec((tm, D_MODEL), row),
        compiler_params=_params(("parallel",)),
        name="post_mix",
    )(ya, ym, yr, x, p, w_out, gmix, gpre, gpost, w_ffn_in, w_ffn_out, gple_pre, gple_post, w_gate, w_emb)


def _rope_body(pos_ref, invf_ref, cos_ref, sin_ref):
    ang = pos_ref[...].astype(F32) * invf_ref[...]
    sin = jnp.sin(ang)
    dim = lax.broadcasted_iota(jnp.int32, ang.shape, 1) & (HEAD_DIM - 1)
    cos_ref[...] = jnp.cos(ang)
    sin_ref[...] = jnp.where(dim < ROPE_HALF, -sin, jnp.where(dim < ROPE_DIM, sin, 0.0))


def _rope_tables(positions, tm):
    s = positions.shape[0]
    lane = jnp.arange(2 * HEAD_DIM) % HEAD_DIM
    freq = ROPE_THETA ** (-jnp.arange(0, ROPE_DIM, 2, dtype=F32) / ROPE_DIM)
    invf = jnp.where(lane < ROPE_DIM, freq[lane % ROPE_HALF], 0.0).astype(F32)[None, :]
    row = lambda i: (i, 0)
    out = jax.ShapeDtypeStruct((s, 2 * HEAD_DIM), F32)
    return pl.pallas_call(
        _rope_body,
        out_shape=[out, out],
        grid=(s // tm,),
        in_specs=[pl.BlockSpec((tm, 1), row), pl.BlockSpec((1, 2 * HEAD_DIM), lambda i: (0, 0))],
        out_specs=[pl.BlockSpec((tm, 2 * HEAD_DIM), row)] * 2,
        compiler_params=_params(("parallel",)),
        name="rope_tables",
    )(positions, invf)


ATTN_BLOCKS = 4
GROUP_W = ATTN_GROUP * HEAD_DIM
KEYS = 2 * WINDOW


def _rope_partner():
    row, col = _tri_masks(2 * HEAD_DIM)
    dim = col & (HEAD_DIM - 1)
    first = (dim < ROPE_HALF) & (row == col + ROPE_HALF)
    second = (dim >= ROPE_HALF) & (dim < ROPE_DIM) & (row == col - ROPE_HALF)
    return (first | second).astype(BF16)


def _attn_body(sinks_ref, p_ref, cos_ref, sin_ref, o_ref, ktprev_ref, vprev_ref, *, nblk):
    step = pl.program_id(0)

    @pl.when(step == 0)
    def _():
        ktprev_ref[...] = jnp.zeros_like(ktprev_ref)
        vprev_ref[...] = jnp.zeros_like(vprev_ref)

    cos, sin = cos_ref[...], sin_ref[...]
    partner = _rope_partner()
    period = 2 * HEAD_DIM

    def rope(x):
        slabs = [x[:, c:c + period] for c in range(0, x.shape[1], period)]
        return jnp.concatenate(
            [s * cos + jnp.dot(s.astype(BF16), partner, preferred_element_type=F32) * sin for s in slabs], axis=1)

    q = (rope(p_ref[:, 0:ATTN_W]) * (HEAD_DIM ** -0.5)).astype(BF16)
    k_cur = rope(p_ref[:, ATTN_W:ATTN_W + KV_W])
    v_cur = p_ref[:, ATTN_W + KV_W:ATTN_COLS].astype(BF16)
    kt = jnp.concatenate([ktprev_ref[...], k_cur.T.astype(BF16)], axis=1)
    vv = jnp.concatenate([vprev_ref[...], v_cur], axis=0)
    ktprev_ref[...] = kt[:, nblk * WINDOW:]
    vprev_ref[...] = vv[nblk * WINDOW:, :]

    t = lax.broadcasted_iota(jnp.int32, (WINDOW, ATTN_GROUP * KEYS), 0)
    lane = lax.broadcasted_iota(jnp.int32, (WINDOW, ATTN_GROUP * KEYS), 1)
    c = lane & (KEYS - 1)
    head = lane >> (KEYS.bit_length() - 1)
    cur_ok = (c >= WINDOW) & (c - WINDOW <= t)
    prev_ok = (c < WINDOW) & (c > t)

    def bias(g, has_prev):
        b = jnp.where(cur_ok | (prev_ok & has_prev), 0.0, NEG_INF)
        sink = jnp.zeros_like(b)
        for n in range(ATTN_GROUP):
            sink = jnp.where(head == n, sinks_ref[g * ATTN_GROUP + n], sink)
        return jnp.where(c == 0, sink, b)

    bias_first = [bias(g, step > 0) for g in range(ATTN_KV_HEADS)]
    bias_rest = [bias(g, True) for g in range(ATTN_KV_HEADS)]

    key_lane = lax.broadcasted_iota(jnp.int32, (HEAD_DIM, KEYS), 1)
    zero_kt = jnp.zeros((HEAD_DIM, KEYS), BF16)
    vrow = lax.broadcasted_iota(jnp.int32, (KEYS, GROUP_W), 0)
    vlane_head = lax.broadcasted_iota(jnp.int32, (KEYS, GROUP_W), 1) >> (HEAD_DIM.bit_length() - 1)
    spread_row = lax.broadcasted_iota(jnp.int32, (KV_W, GROUP_W), 0)
    spread_col = lax.broadcasted_iota(jnp.int32, (KV_W, GROUP_W), 1)

    units = [(b, g) for b in range(nblk) for g in range(ATTN_KV_HEADS)]
    kbd, vo = [], []
    for b, g in units:
        ktg = kt[g * HEAD_DIM:(g + 1) * HEAD_DIM, b * WINDOW:b * WINDOW + KEYS]
        ktg = jnp.where(key_lane == 0, jnp.zeros_like(ktg), ktg)
        kbd.append(jnp.concatenate(
            [jnp.concatenate([ktg if m == n else zero_kt for m in range(ATTN_GROUP)], axis=1)
             for n in range(ATTN_GROUP)], axis=0))
        spread = (spread_row == g * HEAD_DIM + (spread_col & (HEAD_DIM - 1))).astype(BF16)
        vt = jnp.dot(vv[b * WINDOW:b * WINDOW + KEYS, :], spread, preferred_element_type=F32).astype(BF16)
        vt = jnp.where(vrow == 0, jnp.zeros_like(vt), vt)
        vbd = jnp.concatenate(
            [jnp.where(vlane_head == n, vt, jnp.zeros_like(vt)) for n in range(ATTN_GROUP)], axis=0)
        ones = jnp.concatenate([(vlane_head == n).astype(BF16) for n in range(ATTN_GROUP)], axis=0)
        vo.append(jnp.concatenate([vbd, ones], axis=1))

    scores = [jnp.dot(q[b * WINDOW:(b + 1) * WINDOW, g * GROUP_W:(g + 1) * GROUP_W], kb,
                      preferred_element_type=F32) + (bias_first[g] if b == 0 else bias_rest[g])
              for (b, g), kb in zip(units, kbd)]
    probs = []
    for s in scores:
        parts = []
        for n in range(ATTN_GROUP):
            sn = s[:, n * KEYS:(n + 1) * KEYS]
            parts.append(jnp.exp(sn - jnp.max(sn, axis=-1, keepdims=True)).astype(BF16))
        probs.append(jnp.concatenate(parts, axis=1))
    outs = [jnp.dot(pr, x, preferred_element_type=F32) for pr, x in zip(probs, vo)]
    for (b, g), od in zip(units, outs):
        o_ref[b * WINDOW:(b + 1) * WINDOW, g * GROUP_W:(g + 1) * GROUP_W] = (
            od[:, :GROUP_W] / od[:, GROUP_W:]).astype(o_ref.dtype)


def _attention(sinks, attn_p, cos, sin, layer):
    s = attn_p.shape[0]
    nblk = min(ATTN_BLOCKS, s // WINDOW)
    tq = nblk * WINDOW
    row = lambda i: (i, 0)
    return pl.pallas_call(
        functools.partial(_attn_body, nblk=nblk),
        out_shape=jax.ShapeDtypeStruct((s, ATTN_W), BF16),
        grid=(s // tq,),
        in_specs=[
            pl.BlockSpec(memory_space=pltpu.SMEM),
            pl.BlockSpec((tq, ATTN_COLS), row),
            pl.BlockSpec((tq, 2 * HEAD_DIM), row),
            pl.BlockSpec((tq, 2 * HEAD_DIM), row),
        ],
        out_specs=pl.BlockSpec((tq, ATTN_W), row),
        scratch_shapes=[pltpu.VMEM((KV_W, WINDOW), BF16), pltpu.VMEM((WINDOW, KV_W), BF16)],
        compiler_params=_params(("arbitrary",)),
        name="swa_attention",
    )(sinks[layer], attn_p, cos, sin)


def _mlstm_body(p_ref, gates_ref, conv_ref, bias_ref, norm_ref, o_ref, xpad_ref, c_ref, n_ref, *, tb):
    step = pl.program_id(0)

    @pl.when(step == 0)
    def _():
        xpad_ref[0:SUBLANES, :] = jnp.zeros((SUBLANES, 2 * MLSTM_W), F32)
        c_ref[...] = jnp.zeros_like(c_ref)
        n_ref[...] = jnp.zeros_like(n_ref)

    xpad_ref[SUBLANES:SUBLANES + tb, :] = p_ref[:, 0:2 * MLSTM_W]
    conv = None
    for tap in range(MLSTM_CONV):
        shifted = xpad_ref[pl.ds(SUBLANES - (MLSTM_CONV - 1) + tap, tb), :]
        term = shifted * conv_ref[tap:tap + 1, :]
        conv = term if conv is None else conv + term
    xpad_ref[0:SUBLANES, :] = xpad_ref[tb:tb + SUBLANES, :]
    qk = conv * jax.nn.sigmoid(conv)
    q_all = qk[:, 0:MLSTM_W] * (HEAD_DIM ** -0.5)
    k_all = qk[:, MLSTM_W:2 * MLSTM_W]

    pre = GATE_CAP * jnp.tanh((gates_ref[...] + bias_ref[...]) / GATE_CAP)
    logsig = jnp.minimum(pre, 0.0) - jnp.log(1.0 + jnp.exp(-jnp.abs(pre)))

    row, col = _tri_masks(CHUNK)
    causal = col <= row
    eye = col == row
    g_cum = _cumsum_rows(_chunk_tri(tb), logsig)

    gate_col = lax.broadcasted_iota(jnp.int32, (tb, MLSTM_GATES), 1)
    gate_lanes = _split_dot(jnp.where(gate_col < MLSTM_HEADS, pre, g_cum), _gate_spread())
    i_all, g_all = gate_lanes[:, 0:MLSTM_W], gate_lanes[:, MLSTM_W:2 * MLSTM_W]
    head_ones = _head_ones(MLSTM_W)

    def to_row(lane_bcast):
        return jnp.sum(jnp.where(eye, lane_bcast, 0.0), axis=0, keepdims=True)

    chunk_starts = range(0, tb, CHUNK)
    units = [(c0, h) for c0 in chunk_starts for h in range(MLSTM_HEADS)]
    head_lanes = lambda h: slice(h * HEAD_DIM, (h + 1) * HEAD_DIM)
    unit_of = lambda x: [x[c0:c0 + CHUNK, head_lanes(h)] for c0, h in units]
    chunk_of = lambda x: [x[c0:c0 + CHUNK, :] for c0 in chunk_starts]
    v_all = p_ref[:, 2 * MLSTM_W:3 * MLSTM_W]

    g_chunk, i_chunk = chunk_of(g_all), chunk_of(i_all)
    g_tot = [g[CHUNK - 1:CHUNK, :] for g in g_chunk]
    w_in = [jnp.exp(t - g + i) for t, g, i in zip(g_tot, g_chunk, i_chunk)]
    e_tot = [jnp.exp(t) for t in g_tot]
    e_g = [jnp.exp(g) for g in g_chunk]
    n_loc = [jnp.sum(w * x, axis=0, keepdims=True) for w, x in zip(w_in, chunk_of(k_all))]
    wv = unit_of(jnp.concatenate([w * x for w, x in zip(w_in, chunk_of(v_all))], axis=0))

    q, k, v = unit_of(q_all), unit_of(k_all), unit_of(v_all)
    decay = [jnp.exp(jnp.where(causal, g - to_row(g) + to_row(i), NEG_INF))
             for g, i in zip(unit_of(g_all), unit_of(i_all))]
    s_mat = [_mm_nt(a, b) * d for a, b, d in zip(q, k, decay)]
    ones = jnp.ones((CHUNK, HEAD_DIM), F32)
    num_den = [_mm(s, jnp.concatenate([x, ones], axis=1)) for s, x in zip(s_mat, v)]
    c_loc = [_mm_tn(x, b) for x, b in zip(wv, k)]

    c_state = [c_ref[h] for h in range(MLSTM_HEADS)]
    n_state = n_ref[0:1, :]
    c_in, n_in = [], []
    for u, (c0, h) in enumerate(units):
        n_chunk = u // MLSTM_HEADS
        if h == 0:
            n_in.append(n_state)
            n_state = e_tot[n_chunk] * n_state + n_loc[n_chunk]
        c_in.append(c_state[h])
        c_state[h] = e_tot[n_chunk][:, head_lanes(h)] * c_state[h] + c_loc[u]
    for h in range(MLSTM_HEADS):
        c_ref[h] = c_state[h]
    n_ref[0:1, :] = n_state

    inter = [_mm_nt(a, jnp.concatenate(
        [c, jnp.broadcast_to(n_in[u // MLSTM_HEADS][:, head_lanes(h)], (CHUNK, HEAD_DIM))], axis=0))
        for u, ((c0, h), a, c) in enumerate(zip(units, q, c_in))]
    e_g_unit = unit_of(jnp.concatenate(e_g, axis=0))
    top = [nd[:, :HEAD_DIM] + e * x[:, :HEAD_DIM] for nd, e, x in zip(num_den, e_g_unit, inter)]
    bottom = [nd[:, HEAD_DIM:] + e * x[:, HEAD_DIM:] for nd, e, x in zip(num_den, e_g_unit, inter)]
    for n, c0 in enumerate(chunk_starts):
        heads = slice(n * MLSTM_HEADS, (n + 1) * MLSTM_HEADS)
        hid = (jnp.concatenate(top[heads], axis=1)
               / jnp.maximum(jnp.abs(jnp.concatenate(bottom[heads], axis=1)), 1.0))
        mean_sq = _split_dot(hid * hid, head_ones, terms=2) * (1.0 / HEAD_DIM)
        hid = hid * lax.rsqrt(mean_sq + NORM_EPS) * norm_ref[...]
        o_gate = p_ref[c0:c0 + CHUNK, 3 * MLSTM_W:4 * MLSTM_W]
        o_ref[c0:c0 + CHUNK, :] = (jax.nn.sigmoid(o_gate) * hid).astype(o_ref.dtype)


def _mlstm(mlstm_p, gates, conv_w, bias, norm, layer, tb):
    s = mlstm_p.shape[0]
    row = lambda i: (i, 0)
    return pl.pallas_call(
        functools.partial(_mlstm_body, tb=tb),
        out_shape=jax.ShapeDtypeStruct((s, MLSTM_W), BF16),
        grid=(s // tb,),
        in_specs=[
            pl.BlockSpec((tb, MLSTM_MAIN), row),
            pl.BlockSpec((tb, MLSTM_GATES), row),
            pl.BlockSpec((None, MLSTM_CONV, 2 * MLSTM_W), lambda i: (layer, 0, 0)),
            _layer_vec(MLSTM_GATES, layer),
            _layer_vec(MLSTM_W, layer),
        ],
        out_specs=pl.BlockSpec((tb, MLSTM_W), row),
        scratch_shapes=[
            pltpu.VMEM((tb + SUBLANES, 2 * MLSTM_W), F32),
            pltpu.VMEM((MLSTM_HEADS, HEAD_DIM, HEAD_DIM), F32),
            pltpu.VMEM((SUBLANES, MLSTM_W), F32),
        ],
        compiler_params=_params(("arbitrary",)),
        name="mlstm",
    )(mlstm_p, gates, conv_w, bias, norm)


def _rwkv_chunk_terms(a_t, b_t, k_t, r_t, b_end, k_end, v):
    row, col = _tri_masks(CHUNK)
    incl = col <= row
    strict = col < row
    eye = (col == row).astype(F32)
    half = HEAD_DIM

    quad = [_mm_nt(jnp.concatenate([a, r], axis=0), jnp.concatenate([b, k], axis=0))
            for a, r, b, k in zip(a_t, r_t, b_t, k_t)]
    yield
    n_mat = [jnp.where(strict, x[:CHUNK, :CHUNK], 0.0) for x in quad]
    a_ak = [jnp.where(strict, x[:CHUNK, CHUNK:], 0.0) for x in quad]
    c_rb = [jnp.where(incl, x[CHUNK:, :CHUNK], 0.0) for x in quad]
    c_rk = [jnp.where(incl, x[CHUNK:, CHUNK:], 0.0) for x in quad]

    inv = [eye + n for n in n_mat]
    power = n_mat
    for _ in range(NEUMANN_STEPS):
        power = [_mm(x, x) for x in power]
        yield
        inv = [m + _mm(m, x) for m, x in zip(inv, power)]
        yield

    z = [_mm(x, y) for x, y in zip(a_ak, v)]
    yield
    w12 = [_mm(m, jnp.concatenate([a, y], axis=1)) for m, a, y in zip(inv, a_t, z)]
    yield
    cw = [_mm(c, w) for c, w in zip(c_rb, w12)]
    yield
    ckv = [_mm(c, y) for c, y in zip(c_rk, v)]
    yield
    gh = [_mm_tn(w, b) for w, b in zip(w12, b_end)]
    yield
    vk = [_mm_tn(y, k) for y, k in zip(v, k_end)]
    yield
    q_eff = [r + x[:, :half] for r, x in zip(r_t, cw)]
    y_loc = [x[:, half:] + y for x, y in zip(cw, ckv)]
    g_mat = [x[:half, :] for x in gh]
    h_mat = [x[half:, :] + y for x, y in zip(gh, vk)]
    return q_eff, y_loc, g_mat, h_mat


def _rwkv_body(p_ref, mu_ref, w0_ref, wup_ref, a0_ref, aup_ref, gup_ref, kk_ref, ka_ref, rk_ref,
               lnw_ref, lnb_ref, o_ref, upad_ref, s_ref, stage_ref, end_ref, *, tb):
    step = pl.program_id(0)

    @pl.when(step == 0)
    def _():
        upad_ref[0:SUBLANES, :] = jnp.zeros((SUBLANES, RWKV_COLS), F32)
        s_ref[...] = jnp.zeros_like(s_ref)
        stage_ref[...] = jnp.zeros_like(stage_ref)
        end_ref[...] = jnp.zeros_like(end_ref)

    refs = (p_ref, mu_ref, w0_ref, wup_ref, a0_ref, aup_ref, gup_ref, kk_ref, ka_ref, rk_ref,
            lnw_ref, lnb_ref, o_ref, upad_ref, s_ref, stage_ref, end_ref)
    for cur in (0, 1):
        pl.when(lax.rem(step, 2) == cur)(functools.partial(_rwkv_step, *refs, tb=tb, cur=cur, prv=1 - cur))


def _rwkv_step(p_ref, mu_ref, w0_ref, wup_ref, a0_ref, aup_ref, gup_ref, kk_ref, ka_ref, rk_ref,
               lnw_ref, lnb_ref, o_ref, upad_ref, s_ref, stage_ref, end_ref, *, tb, cur, prv):
    head_ones = _head_ones(RWKV_W)
    head_sum = lambda x: _split_dot(x, head_ones, terms=2)
    head_lanes = lambda h: slice(h * HEAD_DIM, (h + 1) * HEAD_DIM)
    chunk_starts = range(0, tb, CHUNK)
    units = [(n, c0, h) for n, c0 in enumerate(chunk_starts) for h in range(RWKV_HEADS)]

    def finish_previous():
        staged = {name: stage_ref[prv, n] for n, name in enumerate(RWKV_STAGED)}
        ends = end_ref[prv]
        per_unit = {name: [staged[name][c0:c0 + CHUNK, head_lanes(h)] for _, c0, h in units]
                    for name in RWKV_STAGED[:7]}
        q_eff, y_loc, g_mat, h_mat = yield from _rwkv_chunk_terms(**per_unit)
        state = [s_ref[h] for h in range(RWKV_HEADS)]
        ys = []
        for u, (n, c0, h) in enumerate(units):
            ys.append(_mm_nt(q_eff[u], state[h]) + y_loc[u])
            state[h] = state[h] * ends[n:n + 1, head_lanes(h)] + _mm(state[h], g_mat[u]) + h_mat[u]
            if h == RWKV_HEADS - 1:
                yield
        for h in range(RWKV_HEADS):
            s_ref[h] = state[h]
        y_all = jnp.concatenate([jnp.concatenate(ys[n:n + RWKV_HEADS], axis=1)
                                 for n in range(0, len(ys), RWKV_HEADS)], axis=0)
        centred = y_all - head_sum(y_all) * (1.0 / HEAD_DIM)
        var = head_sum(centred * centred) * (1.0 / HEAD_DIM)
        normed = centred * lax.rsqrt(var + RWKV_GN_EPS) * lnw_ref[...] + lnb_ref[...]
        o_ref[...] = ((normed + staged["bonus"] * staged["v"]) * staged["gate"]).astype(o_ref.dtype)

    def stage_current():
        upad_ref[SUBLANES:SUBLANES + tb, :] = p_ref[...]
        yield
        for r0 in range(0, tb, RWKV_SLAB):
            rows = slice(r0, r0 + RWKV_SLAB)

            def stage(name, value):
                stage_ref[cur, RWKV_STAGED.index(name), rows, :] = value

            u = p_ref[rows, :]
            prev = upad_ref[pl.ds(SUBLANES - 1 + r0, RWKV_SLAB), :]
            u = u + (prev - u) * mu_ref[...]
            yield
            r_all = u[:, 0:RWKV_W]
            k_raw = u[:, RWKV_W:2 * RWKV_W]
            v_all = u[:, 2 * RWKV_W:3 * RWKV_W]
            x_w = u[:, 3 * RWKV_W:3 * RWKV_W + RWKV_W_RANK]
            x_a = u[:, 3 * RWKV_W + RWKV_W_RANK:3 * RWKV_W + RWKV_W_RANK + RWKV_A_RANK]
            x_g = u[:, 3 * RWKV_W + RWKV_W_RANK + RWKV_A_RANK:RWKV_COLS]
            z = w0_ref[...] + _mm(jnp.tanh(x_w), wup_ref[...])
            ld_all = -jnp.exp(-0.5) * jax.nn.sigmoid(z)
            a_all = jax.nn.sigmoid(a0_ref[...] + _mm(x_a, aup_ref[...]))
            yield
            kk_all = k_raw * kk_ref[...]
            k_all = k_raw * (1.0 + (a_all - 1.0) * ka_ref[...])
            kk_all = kk_all * lax.rsqrt(jnp.maximum(head_sum(kk_all * kk_all), 1e-24))
            a_vec = -kk_all
            b_vec = kk_all * a_all
            yield
            lp_all = _cumsum_rows(_chunk_tri(RWKV_SLAB), ld_all)
            stage("v", v_all)
            stage("bonus", head_sum(r_all * k_all * rk_ref[...]))
            stage("gate", _mm(jax.nn.sigmoid(x_g), gup_ref[...]))
            yield
            starts = range(0, RWKV_SLAB, CHUNK)
            lp_last = jnp.concatenate(
                [jnp.broadcast_to(lp_all[c0 + CHUNK - 1:c0 + CHUNK, :], (CHUNK, RWKV_W)) for c0 in starts], axis=0)
            grow = jnp.exp(-lp_all)
            to_end = jnp.exp(lp_last - lp_all)
            stage("a_t", a_vec * jnp.exp(lp_all - ld_all))
            stage("b_t", b_vec * grow)
            stage("k_t", k_all * grow)
            yield
            stage("r_t", r_all * jnp.exp(lp_all))
            stage("b_end", b_vec * to_end)
            stage("k_end", k_all * to_end)
            for c0 in starts:
                n = (r0 + c0) // CHUNK
                end_ref[cur, n:n + 1, :] = jnp.exp(lp_all[c0 + CHUNK - 1:c0 + CHUNK, :])
            yield
        upad_ref[0:SUBLANES, :] = upad_ref[tb:tb + SUBLANES, :]

    _interleave(finish_previous(), stage_current())


def _rwkv(rwkv_p, mu, w0, w_up, a0, a_up, g_up, k_k, k_a, r_k, ln_w, ln_b, layer, tb):
    s = rwkv_p.shape[0]
    blocks = s // tb
    lay3 = lambda i: (layer, 0, 0)
    vec = lambda width: _layer_vec(width, layer)
    return pl.pallas_call(
        functools.partial(_rwkv_body, tb=tb),
        out_shape=jax.ShapeDtypeStruct((s, RWKV_W), BF16),
        grid=(blocks + 1,),
        in_specs=[
            pl.BlockSpec((tb, RWKV_COLS), lambda i: (jnp.minimum(i, blocks - 1), 0)),
            vec(RWKV_COLS), vec(RWKV_W),
            pl.BlockSpec((None, RWKV_W_RANK, RWKV_W), lay3),
            vec(RWKV_W),
            pl.BlockSpec((None, RWKV_A_RANK, RWKV_W), lay3),
            pl.BlockSpec((None, RWKV_G_RANK, RWKV_W), lay3),
            vec(RWKV_W), vec(RWKV_W), vec(RWKV_W), vec(RWKV_W), vec(RWKV_W),
        ],
        out_specs=pl.BlockSpec((tb, RWKV_W), lambda i: (jnp.maximum(i - 1, 0), 0)),
        scratch_shapes=[
            pltpu.VMEM((tb + SUBLANES, RWKV_COLS), F32),
            pltpu.VMEM((RWKV_HEADS, HEAD_DIM, HEAD_DIM), F32),
            pltpu.VMEM((2, len(RWKV_STAGED), tb, RWKV_W), F32),
            pltpu.VMEM((2, tb // CHUNK, RWKV_W), F32),
        ],
        compiler_params=_params(("arbitrary",)),
        name="rwkv7",
    )(rwkv_p, mu, w0, w_up, a0, a_up, g_up, k_k, k_a, r_k, ln_w, ln_b)


def _tiles(s):
    return min(512, s), min(512, s)


def kernel(x, p, positions, ln_ffn1_pre, ln_ffn1_post, w_ffn1_in, w_ffn1_out, ln_mix_pre, w_in, attn_sinks, mlstm_conv, mlstm_i_bias, mlstm_f_bias, mlstm_norm, rwkv_mu, rwkv_w0, rwkv_w_up, rwkv_a0, rwkv_a_up, rwkv_g_up, rwkv_k_k, rwkv_k_a, rwkv_r_k, rwkv_ln_w, rwkv_ln_b, w_out, ln_mix_post, ln_ffn2_pre, ln_ffn2_post, w_ffn2_in, w_ffn2_out, ln_ple_pre, w_ple_gate, w_ple_proj, ln_ple_post):
    batch, seq, _ = x.shape
    assert batch == 1 and seq % WINDOW == 0
    depth = w_in.shape[0]
    tm, tb = _tiles(seq)

    bf = lambda w: w.astype(BF16)
    w_ffn1_in, w_ffn1_out, w_ffn2_in, w_ffn2_out = map(bf, (w_ffn1_in, w_ffn1_out, w_ffn2_in, w_ffn2_out))
    w_out, w_ple_gate, w_ple_proj = map(bf, (w_out, w_ple_gate, w_ple_proj))
    rwkv_w_up, rwkv_a_up, rwkv_g_up = map(bf, (rwkv_w_up, rwkv_a_up, rwkv_g_up))
    w_in = bf(w_in)
    m0 = ATTN_COLS
    g0 = m0 + MLSTM_MAIN
    r0 = g0 + MLSTM_GATES
    w_attn, w_mlstm, w_gates, w_rwkv = w_in[:, :, :m0], w_in[:, :, m0:g0], w_in[:, :, g0:r0], w_in[:, :, r0:]
    vec = lambda a: a.reshape(depth, 1, -1)
    gate_bias = vec(jnp.concatenate([mlstm_i_bias, mlstm_f_bias], axis=-1))
    (ln_ffn1_pre, ln_ffn1_post, ln_mix_pre, ln_mix_post, ln_ffn2_pre, ln_ffn2_post, ln_ple_pre,
     ln_ple_post, mlstm_norm, rwkv_mu, rwkv_w0, rwkv_a0, rwkv_k_k, rwkv_k_a, rwkv_r_k, rwkv_ln_w,
     rwkv_ln_b) = map(vec, (
         ln_ffn1_pre, ln_ffn1_post, ln_mix_pre, ln_mix_post, ln_ffn2_pre, ln_ffn2_post, ln_ple_pre,
         ln_ple_post, mlstm_norm, rwkv_mu, rwkv_w0, rwkv_a0, rwkv_k_k, rwkv_k_a, rwkv_r_k, rwkv_ln_w,
         rwkv_ln_b))

    cos, sin = _rope_tables(positions.reshape(seq, 1), tm)
    xs = x.reshape(seq, D_MODEL)
    for l in range(depth):
        xs, attn_p, mlstm_p, gates, rwkv_p = _pre_mix(
            xs, ln_ffn1_pre, ln_ffn1_post, w_ffn1_in, w_ffn1_out, ln_mix_pre, w_attn, w_mlstm, w_gates, w_rwkv, l, tm)
        y_attn = _attention(attn_sinks, attn_p, cos, sin, l)
        y_mlstm = _mlstm(mlstm_p, gates, mlstm_conv, gate_bias, mlstm_norm, l, tb)
        y_rwkv = _rwkv(rwkv_p, rwkv_mu, rwkv_w0, rwkv_w_up, rwkv_a0, rwkv_a_up, rwkv_g_up,
                       rwkv_k_k, rwkv_k_a, rwkv_r_k, rwkv_ln_w, rwkv_ln_b, l, tb)
        xs = _post_mix(y_attn, y_mlstm, y_rwkv, xs, p.reshape(depth, seq, D_PLE), w_out, ln_mix_post,
                       ln_ffn2_pre, ln_ffn2_post, w_ffn2_in, w_ffn2_out, ln_ple_pre, ln_ple_post,
                       w_ple_gate, w_ple_proj, l, tm)
    return xs.reshape(batch, seq, D_MODEL)
```

```python
import functools

import jax
import jax.numpy as jnp
from jax import lax
from jax.experimental import pallas as pl
from jax.experimental.pallas import tpu as pltpu

F32 = jnp.float32
BF16 = jnp.bfloat16

D_MODEL = 1024
HEAD_DIM = 64
D_FF = 2816
D_PLE = 256
ATTN_Q_HEADS = 8
ATTN_KV_HEADS = 2
ATTN_GROUP = ATTN_Q_HEADS // ATTN_KV_HEADS
WINDOW = 128
ROPE_THETA = 500000.0
ROPE_DIM = HEAD_DIM // 4
ROPE_HALF = ROPE_DIM // 2
MLSTM_HEADS = 4
MLSTM_CONV = 4
GATE_CAP = 15.0
RWKV_HEADS = 4
RWKV_W_RANK = 64
RWKV_A_RANK = 64
RWKV_G_RANK = 128
RWKV_GN_EPS = 64e-5
NORM_EPS = 1e-6
NEG_INF = -1e30

ATTN_W = ATTN_Q_HEADS * HEAD_DIM
KV_W = ATTN_KV_HEADS * HEAD_DIM
MLSTM_W = MLSTM_HEADS * HEAD_DIM
RWKV_W = RWKV_HEADS * HEAD_DIM
ATTN_COLS = ATTN_W + 2 * KV_W
MLSTM_MAIN = 4 * MLSTM_W
MLSTM_GATES = 2 * MLSTM_HEADS
RWKV_COLS = 3 * RWKV_W + RWKV_W_RANK + RWKV_A_RANK + RWKV_G_RANK

CHUNK = 64
NEUMANN_STEPS = 5
DENSE_HALF_MIN = 256
RWKV_SLAB = 128
RWKV_STAGED = ("a_t", "b_t", "k_t", "r_t", "b_end", "k_end", "v", "bonus", "gate")
MXU_TILE = 256
FF_SPLITS = (0, 6 * MXU_TILE, D_FF)
SUBLANES = 8
V7X_VMEM_LIMIT = 56 * 1024 * 1024


def _params(semantics):
    return pltpu.CompilerParams(dimension_semantics=semantics, vmem_limit_bytes=V7X_VMEM_LIMIT)


def _resident(block_shape, index_map):
    return pl.BlockSpec(block_shape, index_map, pipeline_mode=pl.Buffered(1))


def _layer_vec(width, layer):
    return pl.BlockSpec((None, 1, width), lambda i: (layer, 0, 0))


def _rms(x, gain):
    return x * lax.rsqrt(jnp.mean(x * x, axis=-1, keepdims=True) + NORM_EPS) * gain


def _mm(a, b):
    return jnp.dot(a.astype(BF16), b.astype(BF16), preferred_element_type=F32)


def _mm_nt(a, b):
    return lax.dot_general(a.astype(BF16), b.astype(BF16), (((1,), (1,)), ((), ())),
                           preferred_element_type=F32)


def _mm_tn(a, b):
    return lax.dot_general(a.astype(BF16), b.astype(BF16), (((0,), (0,)), ((), ())),
                           preferred_element_type=F32)


def _bf16_terms(x, terms):
    out, rest = [], x
    for n in range(terms):
        part = rest.astype(BF16)
        out.append(part)
        if n + 1 < terms:
            rest = rest - part.astype(F32)
    return out


def _cumsum_rows(tri, x):
    return sum(jnp.dot(tri, term, preferred_element_type=F32) for term in _bf16_terms(x, 3))


def _split_dot(x, ones, terms=3):
    return sum(jnp.dot(term, ones, preferred_element_type=F32) for term in _bf16_terms(x, terms))


def _head_ones(width):
    row, col = _tri_masks(width)
    shift = HEAD_DIM.bit_length() - 1
    return ((row >> shift) == (col >> shift)).astype(BF16)


def _gate_spread():
    lanes = MLSTM_GATES * HEAD_DIM
    row = lax.broadcasted_iota(jnp.int32, (MLSTM_GATES, lanes), 0)
    col = lax.broadcasted_iota(jnp.int32, (MLSTM_GATES, lanes), 1)
    return (row == (col >> (HEAD_DIM.bit_length() - 1))).astype(BF16)


def _interleave(*gens):
    results = [None] * len(gens)
    live = dict(enumerate(gens))
    while live:
        for n in list(live):
            try:
                next(live[n])
            except StopIteration as stop:
                results[n] = stop.value
                del live[n]
    return results


def _tri_masks(n):
    row = lax.broadcasted_iota(jnp.int32, (n, n), 0)
    col = lax.broadcasted_iota(jnp.int32, (n, n), 1)
    return row, col


def _chunk_tri(n):
    row, col = _tri_masks(n)
    shift = CHUNK.bit_length() - 1
    return (((row >> shift) == (col >> shift)) & (col <= row)).astype(BF16)


def _macaron_half_step(x, gpre_ref, gpost_ref, win_ref, wout_ref):
    xn = _rms(x, gpre_ref[...]).astype(BF16)
    yield
    splits = list(zip(FF_SPLITS[:-1], FF_SPLITS[1:]))
    gate_up = []
    acc = None
    for n in range(len(splits) + 1):
        if n < len(splits):
            lo, hi = splits[n]
            gate_up.append((jnp.dot(xn, win_ref[:, lo:hi], preferred_element_type=F32),
                            jnp.dot(xn, win_ref[:, D_FF + lo:D_FF + hi], preferred_element_type=F32)))
        if n > 0:
            lo, hi = splits[n - 1]
            gate, up = gate_up[n - 1]
            act = (gate * jax.nn.sigmoid(gate) * up).astype(BF16)
            yield
            part = jnp.dot(act, wout_ref[lo:hi, :], preferred_element_type=F32)
            acc = part if acc is None else acc + part
        yield
    return x + 0.5 * _rms(acc, gpost_ref[...])


def _two_row_halves(rows, tile):
    if rows < 2 * DENSE_HALF_MIN:
        return _interleave(tile(0, rows))
    parts = rows // DENSE_HALF_MIN
    size = rows // parts

    def delayed(gen, phases):
        for _ in range(phases):
            yield
        yield from gen

    return _interleave(*[delayed(tile(n * size, size), n) for n in range(parts)])


def _pre_mix_body(x_ref, gpre_ref, gpost_ref, win_ref, wout_ref, gmix_ref, wa_ref, wm_ref, wg_ref, wr_ref,
                  x_out_ref, oa_ref, om_ref, og_ref, or_ref):
    def tile(r0, n):
        rows = slice(r0, r0 + n)
        x = yield from _macaron_half_step(x_ref[rows, :], gpre_ref, gpost_ref, win_ref, wout_ref)
        x_out_ref[rows, :] = x
        h = _rms(x, gmix_ref[...]).astype(BF16)
        yield
        oa_ref[rows, :] = jnp.dot(h, wa_ref[...], preferred_element_type=F32)
        om_ref[rows, :] = jnp.dot(h, wm_ref[...], preferred_element_type=F32)
        og_ref[rows, :] = jnp.dot(h, wg_ref[...], preferred_element_type=F32)
        or_ref[rows, :] = jnp.dot(h, wr_ref[...], preferred_element_type=F32)

    _two_row_halves(x_ref.shape[0], tile)


def _pre_mix(x, gpre, gpost, w_ffn_in, w_ffn_out, gmix, wa, wm, wg, wr, layer, tm):
    s = x.shape[0]
    row = lambda i: (i, 0)
    lay3 = lambda i: (layer, 0, 0)
    widths = (ATTN_COLS, MLSTM_MAIN, MLSTM_GATES, RWKV_COLS)
    return pl.pallas_call(
        _pre_mix_body,
        out_shape=[jax.ShapeDtypeStruct((s, D_MODEL), F32)] + [jax.ShapeDtypeStruct((s, w), F32) for w in widths],
        grid=(s // tm,),
        in_specs=[
            pl.BlockSpec((tm, D_MODEL), row),
            _layer_vec(D_MODEL, layer),
            _layer_vec(D_MODEL, layer),
            _resident((None, D_MODEL, 2 * D_FF), lay3),
            _resident((None, D_FF, D_MODEL), lay3),
            _layer_vec(D_MODEL, layer),
        ] + [_resident((None, D_MODEL, w), lay3) for w in widths],
        out_specs=[pl.BlockSpec((tm, D_MODEL), row)] + [pl.BlockSpec((tm, w), row) for w in widths],
        compiler_params=_params(("parallel",)),
        name="pre_mix",
    )(x, gpre, gpost, w_ffn_in, w_ffn_out, gmix, wa, wm, wg, wr)


def _post_mix_body(ya_ref, ym_ref, yr_ref, x_ref, p_ref, wo_ref, gmix_ref, gpre_ref, gpost_ref, win_ref, wout_ref,
                   gple_pre_ref, gple_post_ref, wgate_ref, wemb_ref, o_ref):
    def tile(r0, n):
        rows = slice(r0, r0 + n)
        mix = jnp.dot(ya_ref[rows, :], wo_ref[0:ATTN_W, :], preferred_element_type=F32)
        mix += jnp.dot(ym_ref[rows, :], wo_ref[ATTN_W:ATTN_W + MLSTM_W, :], preferred_element_type=F32)
        mix += jnp.dot(yr_ref[rows, :], wo_ref[ATTN_W + MLSTM_W:, :], preferred_element_type=F32)
        emb = jnp.dot(p_ref[rows, :].astype(BF16), wemb_ref[...], preferred_element_type=F32)
        yield
        x = x_ref[rows, :] + _rms(mix, gmix_ref[...])
        x = yield from _macaron_half_step(x, gpre_ref, gpost_ref, win_ref, wout_ref)
        h = _rms(x, gple_pre_ref[...]).astype(BF16)
        yield
        gate = jax.nn.sigmoid(jnp.dot(h, wgate_ref[...], preferred_element_type=F32))
        yield
        o_ref[rows, :] = x + _rms(gate * emb, gple_post_ref[...])

    _two_row_halves(x_ref.shape[0], tile)


def _post_mix(ya, ym, yr, x, p, w_out, gmix, gpre, gpost, w_ffn_in, w_ffn_out, gple_pre, gple_post,
              w_gate, w_emb, layer, tm):
    s = x.shape[0]
    row = lambda i: (i, 0)
    lay3 = lambda i: (layer, 0, 0)
    vec = _layer_vec(D_MODEL, layer)
    return pl.pallas_call(
        _post_mix_body,
        out_shape=jax.ShapeDtypeStruct((s, D_MODEL), F32),
        grid=(s // tm,),
        in_specs=[
            pl.BlockSpec((tm, ATTN_W), row),
            pl.BlockSpec((tm, MLSTM_W), row),
            pl.BlockSpec((tm, RWKV_W), row),
            pl.BlockSpec((tm, D_MODEL), row),
            pl.BlockSpec((None, tm, D_PLE), lambda i: (layer, i, 0)),
            _resident((None, D_MODEL, D_MODEL), lay3),
            vec, vec, vec,
            _resident((None, D_MODEL, 2 * D_FF), lay3),
            _resident((None, D_FF, D_MODEL), lay3),
            vec, vec,
            _resident((None, D_MODEL, D_MODEL), lay3),
            _resident((None, D_PLE, D_MODEL), lay3),
        ],
        out_specs=pl.BlockSpec((tm, D_MODEL), row),
        compiler_params=_params(("parallel",)),
        name="post_mix",
    )(ya, ym, yr, x, p, w_out, gmix, gpre, gpost, w_ffn_in, w_ffn_out, gple_pre, gple_post, w_gate, w_emb)


def _rope_body(pos_ref, invf_ref, cos_ref, sin_ref):
    ang = pos_ref[...].astype(F32) * invf_ref[...]
    sin = jnp.sin(ang)
    dim = lax.broadcasted_iota(jnp.int32, ang.shape, 1) & (HEAD_DIM - 1)
    cos_ref[...] = jnp.cos(ang)
    sin_ref[...] = jnp.where(dim < ROPE_HALF, -sin, jnp.where(dim < ROPE_DIM, sin, 0.0))


def _rope_tables(positions, tm):
    s = positions.shape[0]
    lane = jnp.arange(2 * HEAD_DIM) % HEAD_DIM
    freq = ROPE_THETA ** (-jnp.arange(0, ROPE_DIM, 2, dtype=F32) / ROPE_DIM)
    invf = jnp.where(lane < ROPE_DIM, freq[lane % ROPE_HALF], 0.0).astype(F32)[None, :]
    row = lambda i: (i, 0)
    out = jax.ShapeDtypeStruct((s, 2 * HEAD_DIM), F32)
    return pl.pallas_call(
        _rope_body,
        out_shape=[out, out],
        grid=(s // tm,),
        in_specs=[pl.BlockSpec((tm, 1), row), pl.BlockSpec((1, 2 * HEAD_DIM), lambda i: (0, 0))],
        out_specs=[pl.BlockSpec((tm, 2 * HEAD_DIM), row)] * 2,
        compiler_params=_params(("parallel",)),
        name="rope_tables",
    )(positions, invf)


ATTN_BLOCKS = 8
GROUP_W = ATTN_GROUP * HEAD_DIM
KEYS = 2 * WINDOW


def _rope_partner():
    row, col = _tri_masks(2 * HEAD_DIM)
    dim = col & (HEAD_DIM - 1)
    first = (dim < ROPE_HALF) & (row == col + ROPE_HALF)
    second = (dim >= ROPE_HALF) & (dim < ROPE_DIM) & (row == col - ROPE_HALF)
    return (first | second).astype(BF16)


def _attn_body(sinks_ref, p_ref, cos_ref, sin_ref, o_ref, ktprev_ref, vprev_ref, *, nblk):
    step = pl.program_id(0)

    @pl.when(step == 0)
    def _():
        ktprev_ref[...] = jnp.zeros_like(ktprev_ref)
        vprev_ref[...] = jnp.zeros_like(vprev_ref)

    cos, sin = cos_ref[...], sin_ref[...]
    partner = _rope_partner()
    period = 2 * HEAD_DIM

    def rope(x):
        slabs = [x[:, c:c + period] for c in range(0, x.shape[1], period)]
        return jnp.concatenate(
            [s * cos + jnp.dot(s.astype(BF16), partner, preferred_element_type=F32) * sin for s in slabs], axis=1)

    q = (rope(p_ref[:, 0:ATTN_W]) * (HEAD_DIM ** -0.5)).astype(BF16)
    k_cur = rope(p_ref[:, ATTN_W:ATTN_W + KV_W])
    v_f32 = p_ref[:, ATTN_W + KV_W:ATTN_COLS]
    kt = jnp.concatenate([ktprev_ref[...], k_cur.T.astype(BF16)], axis=1)
    vv = jnp.concatenate([vprev_ref[0], v_f32.astype(BF16)], axis=0)
    vs = jnp.concatenate([vprev_ref[1], pltpu.roll(v_f32, HEAD_DIM, 1).astype(BF16)], axis=0)
    ktprev_ref[...] = kt[:, nblk * WINDOW:]
    vprev_ref[0] = vv[nblk * WINDOW:, :]
    vprev_ref[1] = vs[nblk * WINDOW:, :]

    t = lax.broadcasted_iota(jnp.int32, (WINDOW, ATTN_GROUP * KEYS), 0)
    lane = lax.broadcasted_iota(jnp.int32, (WINDOW, ATTN_GROUP * KEYS), 1)
    c = lane & (KEYS - 1)
    head = lane >> (KEYS.bit_length() - 1)
    cur_ok = (c >= WINDOW) & (c - WINDOW <= t)
    prev_ok = (c < WINDOW) & (c > t)

    def bias(g, has_prev):
        b = jnp.where(cur_ok | (prev_ok & has_prev), 0.0, NEG_INF)
        sink = jnp.zeros_like(b)
        for n in range(ATTN_GROUP):
            sink = jnp.where(head == n, sinks_ref[g * ATTN_GROUP + n], sink)
        return jnp.where(c == 0, sink, b)

    bias_first = [bias(g, step > 0) for g in range(ATTN_KV_HEADS)]
    bias_rest = [bias(g, True) for g in range(ATTN_KV_HEADS)]

    key_lane = lax.broadcasted_iota(jnp.int32, (HEAD_DIM, KEYS), 1)
    zero_kt = jnp.zeros((HEAD_DIM, KEYS), BF16)
    vrow = lax.broadcasted_iota(jnp.int32, (KEYS, KV_W), 0)
    v_low = lax.broadcasted_iota(jnp.int32, (KEYS, KV_W), 1) < HEAD_DIM
    o_low = lax.broadcasted_iota(jnp.int32, (WINDOW, KV_W), 1) < HEAD_DIM
    one = jnp.ones((KEYS, KV_W), BF16)

    def blocks():
        units = [(b, g) for b in range(nblk) for g in range(ATTN_KV_HEADS)]
        kbd, v_even, v_odd = [], [], []
        for b, g in units:
            ktg = kt[g * HEAD_DIM:(g + 1) * HEAD_DIM, b * WINDOW:b * WINDOW + KEYS]
            ktg = jnp.where(key_lane == 0, jnp.zeros_like(ktg), ktg)
            kbd.append(jnp.concatenate(
                [jnp.concatenate([ktg if m == n else zero_kt for m in range(ATTN_GROUP)], axis=1)
                 for n in range(ATTN_GROUP)], axis=0))
            window = slice(b * WINDOW, b * WINDOW + KEYS)
            in_low, in_high = (vv, vs) if g == 0 else (vs, vv)
            v_even.append(jnp.where(v_low, jnp.where(vrow == 0, jnp.zeros_like(one), in_low[window, :]), one))
            v_odd.append(jnp.where(v_low, one, jnp.where(vrow == 0, jnp.zeros_like(one), in_high[window, :])))
        yield
        scores = [jnp.dot(q[b * WINDOW:(b + 1) * WINDOW, g * GROUP_W:(g + 1) * GROUP_W], kb,
                          preferred_element_type=F32) + (bias_first[g] if b == 0 else bias_rest[g])
                  for (b, g), kb in zip(units, kbd)]
        yield
        probs = []
        for s in scores:
            parts = []
            for n in range(ATTN_GROUP):
                sn = s[:, n * KEYS:(n + 1) * KEYS]
                parts.append(jnp.exp(sn - jnp.max(sn, axis=-1, keepdims=True)).astype(BF16))
            probs.append(parts)
        yield
        outs = [[jnp.dot(pn, even if n % 2 == 0 else odd, preferred_element_type=F32) for n, pn in enumerate(parts)]
                for parts, even, odd in zip(probs, v_even, v_odd)]
        yield
        for (b, g), heads in zip(units, outs):
            normed = [od * pltpu.roll(1.0 / od, HEAD_DIM, 1) for od in heads]
            pairs = [jnp.where(o_low, normed[n], normed[n + 1]) for n in range(0, ATTN_GROUP, 2)]
            o_ref[b * WINDOW:(b + 1) * WINDOW, g * GROUP_W:(g + 1) * GROUP_W] = (
                jnp.concatenate(pairs, axis=1).astype(o_ref.dtype))

    _interleave(blocks())


def _attention(sinks, attn_p, cos, sin, layer):
    s = attn_p.shape[0]
    nblk = min(ATTN_BLOCKS, s // WINDOW)
    tq = nblk * WINDOW
    row = lambda i: (i, 0)
    return pl.pallas_call(
        functools.partial(_attn_body, nblk=nblk),
        out_shape=jax.ShapeDtypeStruct((s, ATTN_W), BF16),
        grid=(s // tq,),
        in_specs=[
            pl.BlockSpec(memory_space=pltpu.SMEM),
            pl.BlockSpec((tq, ATTN_COLS), row),
            pl.BlockSpec((tq, 2 * HEAD_DIM), row),
            pl.BlockSpec((tq, 2 * HEAD_DIM), row),
        ],
        out_specs=pl.BlockSpec((tq, ATTN_W), row),
        scratch_shapes=[pltpu.VMEM((KV_W, WINDOW), BF16), pltpu.VMEM((2, WINDOW, KV_W), BF16)],
        compiler_params=_params(("arbitrary",)),
        name="swa_attention",
    )(sinks[layer], attn_p, cos, sin)


def _mlstm_body(p_ref, gates_ref, conv_ref, bias_ref, norm_ref, o_ref, xpad_ref, c_ref, n_ref, *, tb):
    step = pl.program_id(0)

    @pl.when(step == 0)
    def _():
        xpad_ref[0:SUBLANES, :] = jnp.zeros((SUBLANES, 2 * MLSTM_W), F32)
        c_ref[...] = jnp.zeros_like(c_ref)
        n_ref[...] = jnp.zeros_like(n_ref)

    xpad_ref[SUBLANES:SUBLANES + tb, :] = p_ref[:, 0:2 * MLSTM_W]
    conv = None
    for tap in range(MLSTM_CONV):
        shifted = xpad_ref[pl.ds(SUBLANES - (MLSTM_CONV - 1) + tap, tb), :]
        term = shifted * conv_ref[tap:tap + 1, :]
        conv = term if conv is None else conv + term
    xpad_ref[0:SUBLANES, :] = xpad_ref[tb:tb + SUBLANES, :]
    qk = conv * jax.nn.sigmoid(conv)
    q_all = qk[:, 0:MLSTM_W] * (HEAD_DIM ** -0.5)
    k_all = qk[:, MLSTM_W:2 * MLSTM_W]

    pre = GATE_CAP * jnp.tanh((gates_ref[...] + bias_ref[...]) / GATE_CAP)
    logsig = jnp.minimum(pre, 0.0) - jnp.log(1.0 + jnp.exp(-jnp.abs(pre)))

    row, col = _tri_masks(CHUNK)
    causal = col <= row
    eye = col == row
    g_cum = _cumsum_rows(_chunk_tri(tb), logsig)

    gate_col = lax.broadcasted_iota(jnp.int32, (tb, MLSTM_GATES), 1)
    gate_lanes = _split_dot(jnp.where(gate_col < MLSTM_HEADS, pre, g_cum), _gate_spread())
    i_all, g_all = gate_lanes[:, 0:MLSTM_W], gate_lanes[:, MLSTM_W:2 * MLSTM_W]
    head_ones = _head_ones(MLSTM_W)

    def to_row(lane_bcast):
        return jnp.sum(jnp.where(eye, lane_bcast, 0.0), axis=0, keepdims=True)

    chunk_starts = range(0, tb, CHUNK)
    units = [(c0, h) for c0 in chunk_starts for h in range(MLSTM_HEADS)]
    head_lanes = lambda h: slice(h * HEAD_DIM, (h + 1) * HEAD_DIM)
    unit_of = lambda x: [x[c0:c0 + CHUNK, head_lanes(h)] for c0, h in units]
    chunk_of = lambda x: [x[c0:c0 + CHUNK, :] for c0 in chunk_starts]
    v_all = p_ref[:, 2 * MLSTM_W:3 * MLSTM_W]

    g_chunk, i_chunk = chunk_of(g_all), chunk_of(i_all)
    g_tot = [g[CHUNK - 1:CHUNK, :] for g in g_chunk]
    w_in = [jnp.exp(t - g + i) for t, g, i in zip(g_tot, g_chunk, i_chunk)]
    e_tot = [jnp.exp(t) for t in g_tot]
    e_g = [jnp.exp(g) for g in g_chunk]
    n_loc = [jnp.sum(w * x, axis=0, keepdims=True) for w, x in zip(w_in, chunk_of(k_all))]
    wv = unit_of(jnp.concatenate([w * x for w, x in zip(w_in, chunk_of(v_all))], axis=0))

    q, k, v = unit_of(q_all), unit_of(k_all), unit_of(v_all)
    decay = [jnp.exp(jnp.where(causal, g - to_row(g) + to_row(i), NEG_INF))
             for g, i in zip(unit_of(g_all), unit_of(i_all))]
    s_mat = [_mm_nt(a, b) * d for a, b, d in zip(q, k, decay)]
    ones = jnp.ones((CHUNK, HEAD_DIM), F32)
    num_den = [_mm(s, jnp.concatenate([x, ones], axis=1)) for s, x in zip(s_mat, v)]
    c_loc = [_mm_tn(x, b) for x, b in zip(wv, k)]

    c_state = [c_ref[h] for h in range(MLSTM_HEADS)]
    n_state = n_ref[0:1, :]
    c_in, n_in = [], []
    for u, (c0, h) in enumerate(units):
        n_chunk = u // MLSTM_HEADS
        if h == 0:
            n_in.append(n_state)
            n_state = e_tot[n_chunk] * n_state + n_loc[n_chunk]
        c_in.append(c_state[h])
        c_state[h] = e_tot[n_chunk][:, head_lanes(h)] * c_state[h] + c_loc[u]
    for h in range(MLSTM_HEADS):
        c_ref[h] = c_state[h]
    n_ref[0:1, :] = n_state

    inter = [_mm_nt(a, jnp.concatenate(
        [c, jnp.broadcast_to(n_in[u // MLSTM_HEADS][:, head_lanes(h)], (CHUNK, HEAD_DIM))], axis=0))
        for u, ((c0, h), a, c) in enumerate(zip(units, q, c_in))]
    e_g_unit = unit_of(jnp.concatenate(e_g, axis=0))
    top = [nd[:, :HEAD_DIM] + e * x[:, :HEAD_DIM] for nd, e, x in zip(num_den, e_g_unit, inter)]
    bottom = [nd[:, HEAD_DIM:] + e * x[:, HEAD_DIM:] for nd, e, x in zip(num_den, e_g_unit, inter)]
    for n, c0 in enumerate(chunk_starts):
        heads = slice(n * MLSTM_HEADS, (n + 1) * MLSTM_HEADS)
        hid = (jnp.concatenate(top[heads], axis=1)
               / jnp.maximum(jnp.abs(jnp.concatenate(bottom[heads], axis=1)), 1.0))
        mean_sq = _split_dot(hid * hid, head_ones, terms=2) * (1.0 / HEAD_DIM)
        hid = hid * lax.rsqrt(mean_sq + NORM_EPS) * norm_ref[...]
        o_gate = p_ref[c0:c0 + CHUNK, 3 * MLSTM_W:4 * MLSTM_W]
        o_ref[c0:c0 + CHUNK, :] = (jax.nn.sigmoid(o_gate) * hid).astype(o_ref.dtype)


def _mlstm(mlstm_p, gates, conv_w, bias, norm, layer, tb):
    s = mlstm_p.shape[0]
    row = lambda i: (i, 0)
    return pl.pallas_call(
        functools.partial(_mlstm_body, tb=tb),
        out_shape=jax.ShapeDtypeStruct((s, MLSTM_W), BF16),
        grid=(s // tb,),
        in_specs=[
            pl.BlockSpec((tb, MLSTM_MAIN), row),
            pl.BlockSpec((tb, MLSTM_GATES), row),
            pl.BlockSpec((None, MLSTM_CONV, 2 * MLSTM_W), lambda i: (layer, 0, 0)),
            _layer_vec(MLSTM_GATES, layer),
            _layer_vec(MLSTM_W, layer),
        ],
        out_specs=pl.BlockSpec((tb, MLSTM_W), row),
        scratch_shapes=[
            pltpu.VMEM((tb + SUBLANES, 2 * MLSTM_W), F32),
            pltpu.VMEM((MLSTM_HEADS, HEAD_DIM, HEAD_DIM), F32),
            pltpu.VMEM((SUBLANES, MLSTM_W), F32),
        ],
        compiler_params=_params(("arbitrary",)),
        name="mlstm",
    )(mlstm_p, gates, conv_w, bias, norm)


def _rwkv_chunk_terms(a_t, b_t, k_t, r_t, b_end, k_end, v):
    row, col = _tri_masks(CHUNK)
    incl = col <= row
    strict = col < row
    eye = (col == row).astype(F32)
    half = HEAD_DIM

    quad = [_mm_nt(jnp.concatenate([a, r], axis=0), jnp.concatenate([b, k], axis=0))
            for a, r, b, k in zip(a_t, r_t, b_t, k_t)]
    yield
    n_mat = [jnp.where(strict, x[:CHUNK, :CHUNK], 0.0) for x in quad]
    a_ak = [jnp.where(strict, x[:CHUNK, CHUNK:], 0.0) for x in quad]
    c_rb = [jnp.where(incl, x[CHUNK:, :CHUNK], 0.0) for x in quad]
    c_rk = [jnp.where(incl, x[CHUNK:, CHUNK:], 0.0) for x in quad]

    inv = [eye + n for n in n_mat]
    power = n_mat
    for _ in range(NEUMANN_STEPS):
        power = [_mm(x, x) for x in power]
        yield
        inv = [m + _mm(m, x) for m, x in zip(inv, power)]
        yield

    z = [_mm(x, y) for x, y in zip(a_ak, v)]
    yield
    w12 = [_mm(m, jnp.concatenate([a, y], axis=1)) for m, a, y in zip(inv, a_t, z)]
    yield
    cw = [_mm(c, w) for c, w in zip(c_rb, w12)]
    yield
    ckv = [_mm(c, y) for c, y in zip(c_rk, v)]
    yield
    gh = [_mm_tn(w, b) for w, b in zip(w12, b_end)]
    yield
    vk = [_mm_tn(y, k) for y, k in zip(v, k_end)]
    yield
    q_eff = [r + x[:, :half] for r, x in zip(r_t, cw)]
    y_loc = [x[:, half:] + y for x, y in zip(cw, ckv)]
    g_mat = [x[:half, :] for x in gh]
    h_mat = [x[half:, :] + y for x, y in zip(gh, vk)]
    return q_eff, y_loc, g_mat, h_mat


def _rwkv_body(p_ref, mu_ref, w0_ref, wup_ref, a0_ref, aup_ref, gup_ref, kk_ref, ka_ref, rk_ref,
               lnw_ref, lnb_ref, o_ref, upad_ref, s_ref, stage_ref, end_ref, *, tb):
    step = pl.program_id(0)

    @pl.when(step == 0)
    def _():
        upad_ref[0:SUBLANES, :] = jnp.zeros((SUBLANES, RWKV_COLS), F32)
        s_ref[...] = jnp.zeros_like(s_ref)
        stage_ref[...] = jnp.zeros_like(stage_ref)
        end_ref[...] = jnp.zeros_like(end_ref)

    refs = (p_ref, mu_ref, w0_ref, wup_ref, a0_ref, aup_ref, gup_ref, kk_ref, ka_ref, rk_ref,
            lnw_ref, lnb_ref, o_ref, upad_ref, s_ref, stage_ref, end_ref)
    for cur in (0, 1):
        pl.when(lax.rem(step, 2) == cur)(functools.partial(_rwkv_step, *refs, tb=tb, cur=cur, prv=1 - cur))


def _rwkv_step(p_ref, mu_ref, w0_ref, wup_ref, a0_ref, aup_ref, gup_ref, kk_ref, ka_ref, rk_ref,
               lnw_ref, lnb_ref, o_ref, upad_ref, s_ref, stage_ref, end_ref, *, tb, cur, prv):
    head_ones = _head_ones(RWKV_W)
    head_sum = lambda x: _split_dot(x, head_ones, terms=2)
    head_lanes = lambda h: slice(h * HEAD_DIM, (h + 1) * HEAD_DIM)
    chunk_starts = range(0, tb, CHUNK)
    units = [(n, c0, h) for n, c0 in enumerate(chunk_starts) for h in range(RWKV_HEADS)]

    def finish_previous():
        staged = {name: stage_ref[prv, n] for n, name in enumerate(RWKV_STAGED)}
        ends = end_ref[prv]
        per_unit = {name: [staged[name][c0:c0 + CHUNK, head_lanes(h)] for _, c0, h in units]
                    for name in RWKV_STAGED[:7]}
        q_eff, y_loc, g_mat, h_mat = yield from _rwkv_chunk_terms(**per_unit)
        state = [s_ref[h] for h in range(RWKV_HEADS)]
        ys = []
        for u, (n, c0, h) in enumerate(units):
            ys.append(_mm_nt(q_eff[u], state[h]) + y_loc[u])
            state[h] = state[h] * ends[n:n + 1, head_lanes(h)] + _mm(state[h], g_mat[u]) + h_mat[u]
            if h == RWKV_HEADS - 1:
                yield
        for h in range(RWKV_HEADS):
            s_ref[h] = state[h]
        y_all = jnp.concatenate([jnp.concatenate(ys[n:n + RWKV_HEADS], axis=1)
                                 for n in range(0, len(ys), RWKV_HEADS)], axis=0)
        centred = y_all - head_sum(y_all) * (1.0 / HEAD_DIM)
        var = head_sum(centred * centred) * (1.0 / HEAD_DIM)
        normed = centred * lax.rsqrt(var + RWKV_GN_EPS) * lnw_ref[...] + lnb_ref[...]
        o_ref[...] = ((normed + staged["bonus"] * staged["v"]) * staged["gate"]).astype(o_ref.dtype)

    def stage_current():
        upad_ref[SUBLANES:SUBLANES + tb, :] = p_ref[...]
        yield
        for r0 in range(0, tb, RWKV_SLAB):
            rows = slice(r0, r0 + RWKV_SLAB)

            def stage(name, value):
                stage_ref[cur, RWKV_STAGED.index(name), rows, :] = value

            u = p_ref[rows, :]
            prev = upad_ref[pl.ds(SUBLANES - 1 + r0, RWKV_SLAB), :]
            u = u + (prev - u) * mu_ref[...]
            yield
            r_all = u[:, 0:RWKV_W]
            k_raw = u[:, RWKV_W:2 * RWKV_W]
            v_all = u[:, 2 * RWKV_W:3 * RWKV_W]
            x_w = u[:, 3 * RWKV_W:3 * RWKV_W + RWKV_W_RANK]
            x_a = u[:, 3 * RWKV_W + RWKV_W_RANK:3 * RWKV_W + RWKV_W_RANK + RWKV_A_RANK]
            x_g = u[:, 3 * RWKV_W + RWKV_W_RANK + RWKV_A_RANK:RWKV_COLS]
            z = w0_ref[...] + _mm(jnp.tanh(x_w), wup_ref[...])
            ld_all = -jnp.exp(-0.5) * jax.nn.sigmoid(z)
            a_all = jax.nn.sigmoid(a0_ref[...] + _mm(x_a, aup_ref[...]))
            yield
            kk_all = k_raw * kk_ref[...]
            k_all = k_raw * (1.0 + (a_all - 1.0) * ka_ref[...])
            kk_all = kk_all * lax.rsqrt(jnp.maximum(head_sum(kk_all * kk_all), 1e-24))
            a_vec = -kk_all
            b_vec = kk_all * a_all
            yield
            lp_all = _cumsum_rows(_chunk_tri(RWKV_SLAB), ld_all)
            stage("v", v_all)
            stage("bonus", head_sum(r_all * k_all * rk_ref[...]))
            stage("gate", _mm(jax.nn.sigmoid(x_g), gup_ref[...]))
            yield
            starts = range(0, RWKV_SLAB, CHUNK)
            lp_last = jnp.concatenate(
                [jnp.broadcast_to(lp_all[c0 + CHUNK - 1:c0 + CHUNK, :], (CHUNK, RWKV_W)) for c0 in starts], axis=0)
            grow = jnp.exp(-lp_all)
            to_end = jnp.exp(lp_last - lp_all)
            stage("a_t", a_vec * jnp.exp(lp_all - ld_all))
            stage("b_t", b_vec * grow)
            stage("k_t", k_all * grow)
            yield
            stage("r_t", r_all * jnp.exp(lp_all))
            stage("b_end", b_vec * to_end)
            stage("k_end", k_all * to_end)
            for c0 in starts:
                n = (r0 + c0) // CHUNK
                end_ref[cur, n:n + 1, :] = jnp.exp(lp_all[c0 + CHUNK - 1:c0 + CHUNK, :])
            yield
        upad_ref[0:SUBLANES, :] = upad_ref[tb:tb + SUBLANES, :]

    _interleave(finish_previous(), stage_current())


def _rwkv(rwkv_p, mu, w0, w_up, a0, a_up, g_up, k_k, k_a, r_k, ln_w, ln_b, layer, tb):
    s = rwkv_p.shape[0]
    blocks = s // tb
    lay3 = lambda i: (layer, 0, 0)
    vec = lambda width: _layer_vec(width, layer)
    return pl.pallas_call(
        functools.partial(_rwkv_body, tb=tb),
        out_shape=jax.ShapeDtypeStruct((s, RWKV_W), BF16),
        grid=(blocks + 1,),
        in_specs=[
            pl.BlockSpec((tb, RWKV_COLS), lambda i: (jnp.minimum(i, blocks - 1), 0)),
            vec(RWKV_COLS), vec(RWKV_W),
            pl.BlockSpec((None, RWKV_W_RANK, RWKV_W), lay3),
            vec(RWKV_W),
            pl.BlockSpec((None, RWKV_A_RANK, RWKV_W), lay3),
            pl.BlockSpec((None, RWKV_G_RANK, RWKV_W), lay3),
            vec(RWKV_W), vec(RWKV_W), vec(RWKV_W), vec(RWKV_W), vec(RWKV_W),
        ],
        out_specs=pl.BlockSpec((tb, RWKV_W), lambda i: (jnp.maximum(i - 1, 0), 0)),
        scratch_shapes=[
            pltpu.VMEM((tb + SUBLANES, RWKV_COLS), F32),
            pltpu.VMEM((RWKV_HEADS, HEAD_DIM, HEAD_DIM), F32),
            pltpu.VMEM((2, len(RWKV_STAGED), tb, RWKV_W), F32),
            pltpu.VMEM((2, tb // CHUNK, RWKV_W), F32),
        ],
        compiler_params=_params(("arbitrary",)),
        name="rwkv7",
    )(rwkv_p, mu, w0, w_up, a0, a_up, g_up, k_k, k_a, r_k, ln_w, ln_b)


def _tiles(s):
    return min(512, s), min(512, s)


def kernel(x, p, positions, ln_ffn1_pre, ln_ffn1_post, w_ffn1_in, w_ffn1_out, ln_mix_pre, w_in, attn_sinks, mlstm_conv, mlstm_i_bias, mlstm_f_bias, mlstm_norm, rwkv_mu, rwkv_w0, rwkv_w_up, rwkv_a0, rwkv_a_up, rwkv_g_up, rwkv_k_k, rwkv_k_a, rwkv_r_k, rwkv_ln_w, rwkv_ln_b, w_out, ln_mix_post, ln_ffn2_pre, ln_ffn2_post, w_ffn2_in, w_ffn2_out, ln_ple_pre, w_ple_gate, w_ple_proj, ln_ple_post):
    batch, seq, _ = x.shape
    assert batch == 1 and seq % WINDOW == 0
    depth = w_in.shape[0]
    tm, tb = _tiles(seq)

    bf = lambda w: w.astype(BF16)
    w_ffn1_in, w_ffn1_out, w_ffn2_in, w_ffn2_out = map(bf, (w_ffn1_in, w_ffn1_out, w_ffn2_in, w_ffn2_out))
    w_out, w_ple_gate, w_ple_proj = map(bf, (w_out, w_ple_gate, w_ple_proj))
    rwkv_w_up, rwkv_a_up, rwkv_g_up = map(bf, (rwkv_w_up, rwkv_a_up, rwkv_g_up))
    m0 = ATTN_COLS
    g0 = m0 + MLSTM_MAIN
    r0 = g0 + MLSTM_GATES
    w_attn, w_mlstm, w_gates, w_rwkv = map(
        bf, (w_in[:, :, :m0], w_in[:, :, m0:g0], w_in[:, :, g0:r0], w_in[:, :, r0:]))
    vec = lambda a: a.reshape(depth, 1, -1)
    gate_bias = vec(jnp.concatenate([mlstm_i_bias, mlstm_f_bias], axis=-1))
    (ln_ffn1_pre, ln_ffn1_post, ln_mix_pre, ln_mix_post, ln_ffn2_pre, ln_ffn2_post, ln_ple_pre,
     ln_ple_post, mlstm_norm, rwkv_mu, rwkv_w0, rwkv_a0, rwkv_k_k, rwkv_k_a, rwkv_r_k, rwkv_ln_w,
     rwkv_ln_b) = map(vec, (
         ln_ffn1_pre, ln_ffn1_post, ln_mix_pre, ln_mix_post, ln_ffn2_pre, ln_ffn2_post, ln_ple_pre,
         ln_ple_post, mlstm_norm, rwkv_mu, rwkv_w0, rwkv_a0, rwkv_k_k, rwkv_k_a, rwkv_r_k, rwkv_ln_w,
         rwkv_ln_b))

    cos, sin = _rope_tables(positions.reshape(seq, 1), tm)
    xs = x.reshape(seq, D_MODEL)
    for l in range(depth):
        xs, attn_p, mlstm_p, gates, rwkv_p = _pre_mix(
            xs, ln_ffn1_pre, ln_ffn1_post, w_ffn1_in, w_ffn1_out, ln_mix_pre, w_attn, w_mlstm, w_gates, w_rwkv, l, tm)
        y_attn = _attention(attn_sinks, attn_p, cos, sin, l)
        y_mlstm = _mlstm(mlstm_p, gates, mlstm_conv, gate_bias, mlstm_norm, l, tb)
        y_rwkv = _rwkv(rwkv_p, rwkv_mu, rwkv_w0, rwkv_w_up, rwkv_a0, rwkv_a_up, rwkv_g_up,
                       rwkv_k_k, rwkv_k_a, rwkv_r_k, rwkv_ln_w, rwkv_ln_b, l, tb)
        xs = _post_mix(y_attn, y_mlstm, y_rwkv, xs, p.reshape(depth, seq, D_PLE), w_out, ln_mix_post,
                       ln_ffn2_pre, ln_ffn2_post, w_ffn2_in, w_ffn2_out, ln_ple_pre, ln_ple_post,
                       w_ple_gate, w_ple_proj, l, tm)
    return xs.reshape(batch, seq, D_MODEL)
```

```python
import functools

import jax
import jax.numpy as jnp
from jax import lax
from jax.experimental import pallas as pl
from jax.experimental.pallas import tpu as pltpu

F32 = jnp.float32
BF16 = jnp.bfloat16

D_MODEL = 1024
HEAD_DIM = 64
D_FF = 2816
D_PLE = 256
ATTN_Q_HEADS = 8
ATTN_KV_HEADS = 2
ATTN_GROUP = ATTN_Q_HEADS // ATTN_KV_HEADS
WINDOW = 128
ROPE_THETA = 500000.0
ROPE_DIM = HEAD_DIM // 4
ROPE_HALF = ROPE_DIM // 2
MLSTM_HEADS = 4
MLSTM_CONV = 4
GATE_CAP = 15.0
RWKV_HEADS = 4
RWKV_W_RANK = 64
RWKV_A_RANK = 64
RWKV_G_RANK = 128
RWKV_GN_EPS = 64e-5
NORM_EPS = 1e-6
NEG_INF = -1e30

ATTN_W = ATTN_Q_HEADS * HEAD_DIM
KV_W = ATTN_KV_HEADS * HEAD_DIM
MLSTM_W = MLSTM_HEADS * HEAD_DIM
RWKV_W = RWKV_HEADS * HEAD_DIM
ATTN_COLS = ATTN_W + 2 * KV_W
MLSTM_MAIN = 4 * MLSTM_W
MLSTM_GATES = 2 * MLSTM_HEADS
RWKV_COLS = 3 * RWKV_W + RWKV_W_RANK + RWKV_A_RANK + RWKV_G_RANK

CHUNK = 64
NEUMANN_STEPS = 5
DENSE_HALF_MIN = 256
RWKV_SLAB = 128
RWKV_STAGED = ("a_t", "b_t", "k_t", "r_t", "b_end", "k_end", "v", "bonus", "gate")
MXU_TILE = 256
FF_SPLITS = (0, 6 * MXU_TILE, D_FF)
SUBLANES = 8
V7X_VMEM_LIMIT = 56 * 1024 * 1024


def _params(semantics):
    return pltpu.CompilerParams(dimension_semantics=semantics, vmem_limit_bytes=V7X_VMEM_LIMIT)


def _resident(block_shape, index_map):
    return pl.BlockSpec(block_shape, index_map, pipeline_mode=pl.Buffered(1))


def _layer_vec(width, layer):
    return pl.BlockSpec((None, 1, width), lambda i: (layer, 0, 0))


def _rms(x, gain):
    return x * lax.rsqrt(jnp.mean(x * x, axis=-1, keepdims=True) + NORM_EPS) * gain


def _mm(a, b):
    return jnp.dot(a.astype(BF16), b.astype(BF16), preferred_element_type=F32)


def _mm_nt(a, b):
    return lax.dot_general(a.astype(BF16), b.astype(BF16), (((1,), (1,)), ((), ())),
                           preferred_element_type=F32)


def _mm_tn(a, b):
    return lax.dot_general(a.astype(BF16), b.astype(BF16), (((0,), (0,)), ((), ())),
                           preferred_element_type=F32)


def _bf16_terms(x, terms):
    out, rest = [], x
    for n in range(terms):
        part = rest.astype(BF16)
        out.append(part)
        if n + 1 < terms:
            rest = rest - part.astype(F32)
    return out


def _cumsum_rows(tri, x):
    return sum(jnp.dot(tri, term, preferred_element_type=F32) for term in _bf16_terms(x, 3))


def _split_dot(x, ones, terms=3):
    return sum(jnp.dot(term, ones, preferred_element_type=F32) for term in _bf16_terms(x, terms))


def _head_ones(width):
    row, col = _tri_masks(width)
    shift = HEAD_DIM.bit_length() - 1
    return ((row >> shift) == (col >> shift)).astype(BF16)


def _gate_spread():
    lanes = MLSTM_GATES * HEAD_DIM
    row = lax.broadcasted_iota(jnp.int32, (MLSTM_GATES, lanes), 0)
    col = lax.broadcasted_iota(jnp.int32, (MLSTM_GATES, lanes), 1)
    return (row == (col >> (HEAD_DIM.bit_length() - 1))).astype(BF16)


def _interleave(*gens):
    results = [None] * len(gens)
    live = dict(enumerate(gens))
    while live:
        for n in list(live):
            try:
                next(live[n])
            except StopIteration as stop:
                results[n] = stop.value
                del live[n]
    return results


def _tri_masks(n):
    row = lax.broadcasted_iota(jnp.int32, (n, n), 0)
    col = lax.broadcasted_iota(jnp.int32, (n, n), 1)
    return row, col


def _chunk_tri(n):
    row, col = _tri_masks(n)
    shift = CHUNK.bit_length() - 1
    return (((row >> shift) == (col >> shift)) & (col <= row)).astype(BF16)


def _macaron_half_step(x, gpre_ref, gpost_ref, win_ref, wout_ref):
    xn = _rms(x, gpre_ref[...]).astype(BF16)
    yield
    splits = list(zip(FF_SPLITS[:-1], FF_SPLITS[1:]))
    gate_up = []
    acc = None
    for n in range(len(splits) + 1):
        if n < len(splits):
            lo, hi = splits[n]
            gate_up.append((jnp.dot(xn, win_ref[:, lo:hi], preferred_element_type=F32),
                            jnp.dot(xn, win_ref[:, D_FF + lo:D_FF + hi], preferred_element_type=F32)))
        if n > 0:
            lo, hi = splits[n - 1]
            gate, up = gate_up[n - 1]
            act = (gate * jax.nn.sigmoid(gate) * up).astype(BF16)
            yield
            part = jnp.dot(act, wout_ref[lo:hi, :], preferred_element_type=F32)
            acc = part if acc is None else acc + part
        yield
    return x + 0.5 * _rms(acc, gpost_ref[...])


def _two_row_halves(rows, tile):
    if rows < 2 * DENSE_HALF_MIN:
        return _interleave(tile(0, rows))
    parts = rows // DENSE_HALF_MIN
    size = rows // parts

    def delayed(gen, phases):
        for _ in range(phases):
            yield
        yield from gen

    return _interleave(*[delayed(tile(n * size, size), n) for n in range(parts)])


def _pre_mix_body(x_ref, gpre_ref, gpost_ref, win_ref, wout_ref, gmix_ref, wproj_ref,
                  x_out_ref, *out_refs):
    def tile(r0, n):
        rows = slice(r0, r0 + n)
        x = yield from _macaron_half_step(x_ref[rows, :], gpre_ref, gpost_ref, win_ref, wout_ref)
        x_out_ref[rows, :] = x
        h = _rms(x, gmix_ref[...]).astype(BF16)
        yield
        lo = 0
        for out_ref in out_refs:
            width = out_ref.shape[1]
            out_ref[rows, :] = jnp.dot(h, wproj_ref[:, lo:lo + width], preferred_element_type=F32)
            lo += width

    _two_row_halves(x_ref.shape[0], tile)


PROJ_WIDTHS = (MLSTM_MAIN, RWKV_COLS, ATTN_COLS, MLSTM_GATES)


def _pre_mix(x, gpre, gpost, w_ffn_in, w_ffn_out, gmix, w_proj, layer, tm):
    s = x.shape[0]
    row = lambda i: (i, 0)
    lay3 = lambda i: (layer, 0, 0)
    widths = PROJ_WIDTHS
    return pl.pallas_call(
        _pre_mix_body,
        out_shape=[jax.ShapeDtypeStruct((s, D_MODEL), F32)] + [jax.ShapeDtypeStruct((s, w), F32) for w in widths],
        grid=(s // tm,),
        in_specs=[
            pl.BlockSpec((tm, D_MODEL), row),
            _layer_vec(D_MODEL, layer),
            _layer_vec(D_MODEL, layer),
            _resident((None, D_MODEL, 2 * D_FF), lay3),
            _resident((None, D_FF, D_MODEL), lay3),
            _layer_vec(D_MODEL, layer),
            _resident((None, D_MODEL, sum(widths)), lay3),
        ],
        out_specs=[pl.BlockSpec((tm, D_MODEL), row)] + [pl.BlockSpec((tm, w), row) for w in widths],
        compiler_params=_params(("parallel",)),
        name="pre_mix",
    )(x, gpre, gpost, w_ffn_in, w_ffn_out, gmix, w_proj)


def _post_mix_body(ya_ref, ym_ref, yr_ref, x_ref, p_ref, wo_ref, gmix_ref, gpre_ref, gpost_ref, win_ref, wout_ref,
                   gple_pre_ref, gple_post_ref, wgate_ref, wemb_ref, o_ref):
    def tile(r0, n):
        rows = slice(r0, r0 + n)
        mix = jnp.dot(ya_ref[rows, :], wo_ref[0:ATTN_W, :], preferred_element_type=F32)
        mix += jnp.dot(ym_ref[rows, :], wo_ref[ATTN_W:ATTN_W + MLSTM_W, :], preferred_element_type=F32)
        mix += jnp.dot(yr_ref[rows, :], wo_ref[ATTN_W + MLSTM_W:, :], preferred_element_type=F32)
        emb = jnp.dot(p_ref[rows, :].astype(BF16), wemb_ref[...], preferred_element_type=F32)
        yield
        x = x_ref[rows, :] + _rms(mix, gmix_ref[...])
        x = yield from _macaron_half_step(x, gpre_ref, gpost_ref, win_ref, wout_ref)
        h = _rms(x, gple_pre_ref[...]).astype(BF16)
        yield
        gate = jax.nn.sigmoid(jnp.dot(h, wgate_ref[...], preferred_element_type=F32))
        yield
        o_ref[rows, :] = x + _rms(gate * emb, gple_post_ref[...])

    _two_row_halves(x_ref.shape[0], tile)


def _post_mix(ya, ym, yr, x, p, w_out, gmix, gpre, gpost, w_ffn_in, w_ffn_out, gple_pre, gple_post,
              w_gate, w_emb, layer, tm):
    s = x.shape[0]
    row = lambda i: (i, 0)
    lay3 = lambda i: (layer, 0, 0)
    vec = _layer_vec(D_MODEL, layer)
    return pl.pallas_call(
        _post_mix_body,
        out_shape=jax.ShapeDtypeStruct((s, D_MODEL), F32),
        grid=(s // tm,),
        in_specs=[
            pl.BlockSpec((tm, ATTN_W), row),
            pl.BlockSpec((tm, MLSTM_W), row),
            pl.BlockSpec((tm, RWKV_W), row),
            pl.BlockSpec((tm, D_MODEL), row),
            pl.BlockSpec((None, tm, D_PLE), lambda i: (layer, i, 0)),
            _resident((None, D_MODEL, D_MODEL), lay3),
            vec, vec, vec,
            _resident((None, D_MODEL, 2 * D_FF), lay3),
            _resident((None, D_FF, D_MODEL), lay3),
            vec, vec,
            _resident((None, D_MODEL, D_MODEL), lay3),
            _resident((None, D_PLE, D_MODEL), lay3),
        ],
        out_specs=pl.BlockSpec((tm, D_MODEL), row),
        compiler_params=_params(("parallel",)),
        name="post_mix",
    )(ya, ym, yr, x, p, w_out, gmix, gpre, gpost, w_ffn_in, w_ffn_out, gple_pre, gple_post, w_gate, w_emb)


def _rope_body(pos_ref, invf_ref, cos_ref, sin_ref):
    ang = pos_ref[...].astype(F32) * invf_ref[...]
    sin = jnp.sin(ang)
    dim = lax.broadcasted_iota(jnp.int32, ang.shape, 1) & (HEAD_DIM - 1)
    cos_ref[...] = jnp.cos(ang)
    sin_ref[...] = jnp.where(dim < ROPE_HALF, -sin, jnp.where(dim < ROPE_DIM, sin, 0.0))


def _rope_tables(positions, tm):
    s = positions.shape[0]
    lane = jnp.arange(2 * HEAD_DIM) % HEAD_DIM
    freq = ROPE_THETA ** (-jnp.arange(0, ROPE_DIM, 2, dtype=F32) / ROPE_DIM)
    invf = jnp.where(lane < ROPE_DIM, freq[lane % ROPE_HALF], 0.0).astype(F32)[None, :]
    row = lambda i: (i, 0)
    out = jax.ShapeDtypeStruct((s, 2 * HEAD_DIM), F32)
    return pl.pallas_call(
        _rope_body,
        out_shape=[out, out],
        grid=(s // tm,),
        in_specs=[pl.BlockSpec((tm, 1), row), pl.BlockSpec((1, 2 * HEAD_DIM), lambda i: (0, 0))],
        out_specs=[pl.BlockSpec((tm, 2 * HEAD_DIM), row)] * 2,
        compiler_params=_params(("parallel",)),
        name="rope_tables",
    )(positions, invf)


ATTN_BLOCKS = 8
GROUP_W = ATTN_GROUP * HEAD_DIM
KEYS = 2 * WINDOW


def _rope_partner():
    row, col = _tri_masks(2 * HEAD_DIM)
    dim = col & (HEAD_DIM - 1)
    first = (dim < ROPE_HALF) & (row == col + ROPE_HALF)
    second = (dim >= ROPE_HALF) & (dim < ROPE_DIM) & (row == col - ROPE_HALF)
    return (first | second).astype(BF16)


def _attn_body(sinks_ref, p_ref, cos_ref, sin_ref, o_ref, ktprev_ref, vprev_ref, *, nblk):
    step = pl.program_id(0)

    @pl.when(step == 0)
    def _():
        ktprev_ref[...] = jnp.zeros_like(ktprev_ref)
        vprev_ref[...] = jnp.zeros_like(vprev_ref)

    cos, sin = cos_ref[...], sin_ref[...]
    partner = _rope_partner()
    period = 2 * HEAD_DIM

    def rope(x):
        slabs = [x[:, c:c + period] for c in range(0, x.shape[1], period)]
        return jnp.concatenate(
            [s * cos + jnp.dot(s.astype(BF16), partner, preferred_element_type=F32) * sin for s in slabs], axis=1)

    q = (rope(p_ref[:, 0:ATTN_W]) * (HEAD_DIM ** -0.5)).astype(BF16)
    k_cur = rope(p_ref[:, ATTN_W:ATTN_W + KV_W])
    v_f32 = p_ref[:, ATTN_W + KV_W:ATTN_COLS]
    kt = jnp.concatenate([ktprev_ref[...], k_cur.T.astype(BF16)], axis=1)
    vv = jnp.concatenate([vprev_ref[0], v_f32.astype(BF16)], axis=0)
    vs = jnp.concatenate([vprev_ref[1], pltpu.roll(v_f32, HEAD_DIM, 1).astype(BF16)], axis=0)
    ktprev_ref[...] = kt[:, nblk * WINDOW:]
    vprev_ref[0] = vv[nblk * WINDOW:, :]
    vprev_ref[1] = vs[nblk * WINDOW:, :]

    t = lax.broadcasted_iota(jnp.int32, (WINDOW, ATTN_GROUP * KEYS), 0)
    lane = lax.broadcasted_iota(jnp.int32, (WINDOW, ATTN_GROUP * KEYS), 1)
    c = lane & (KEYS - 1)
    head = lane >> (KEYS.bit_length() - 1)
    cur_ok = (c >= WINDOW) & (c - WINDOW <= t)
    prev_ok = (c < WINDOW) & (c > t)

    def bias(g, has_prev):
        b = jnp.where(cur_ok | (prev_ok & has_prev), 0.0, NEG_INF)
        sink = jnp.zeros_like(b)
        for n in range(ATTN_GROUP):
            sink = jnp.where(head == n, sinks_ref[g * ATTN_GROUP + n], sink)
        return jnp.where(c == 0, sink, b)

    bias_first = [bias(g, step > 0) for g in range(ATTN_KV_HEADS)]
    bias_rest = [bias(g, True) for g in range(ATTN_KV_HEADS)]

    key_lane = lax.broadcasted_iota(jnp.int32, (HEAD_DIM, KEYS), 1)
    zero_kt = jnp.zeros((HEAD_DIM, KEYS), BF16)
    vrow = lax.broadcasted_iota(jnp.int32, (KEYS, KV_W), 0)
    v_low = lax.broadcasted_iota(jnp.int32, (KEYS, KV_W), 1) < HEAD_DIM
    o_low = lax.broadcasted_iota(jnp.int32, (WINDOW, KV_W), 1) < HEAD_DIM
    one = jnp.ones((KEYS, KV_W), BF16)

    def blocks():
        units = [(b, g) for b in range(nblk) for g in range(ATTN_KV_HEADS)]
        kbd, v_even, v_odd = [], [], []
        for b, g in units:
            ktg = kt[g * HEAD_DIM:(g + 1) * HEAD_DIM, b * WINDOW:b * WINDOW + KEYS]
            ktg = jnp.where(key_lane == 0, jnp.zeros_like(ktg), ktg)
            kbd.append(jnp.concatenate(
                [jnp.concatenate([ktg if m == n else zero_kt for m in range(ATTN_GROUP)], axis=1)
                 for n in range(ATTN_GROUP)], axis=0))
            window = slice(b * WINDOW, b * WINDOW + KEYS)
            in_low, in_high = (vv, vs) if g == 0 else (vs, vv)
            v_even.append(jnp.where(v_low, jnp.where(vrow == 0, jnp.zeros_like(one), in_low[window, :]), one))
            v_odd.append(jnp.where(v_low, one, jnp.where(vrow == 0, jnp.zeros_like(one), in_high[window, :])))
        yield
        scores = [jnp.dot(q[b * WINDOW:(b + 1) * WINDOW, g * GROUP_W:(g + 1) * GROUP_W], kb,
                          preferred_element_type=F32) + (bias_first[g] if b == 0 else bias_rest[g])
                  for (b, g), kb in zip(units, kbd)]
        yield
        probs = []
        for s in scores:
            parts = []
            for n in range(ATTN_GROUP):
                sn = s[:, n * KEYS:(n + 1) * KEYS]
                parts.append(jnp.exp(sn - jnp.max(sn, axis=-1, keepdims=True)).astype(BF16))
            probs.append(parts)
        yield
        outs = [[jnp.dot(pn, even if n % 2 == 0 else odd, preferred_element_type=F32) for n, pn in enumerate(parts)]
                for parts, even, odd in zip(probs, v_even, v_odd)]
        yield
        for (b, g), heads in zip(units, outs):
            normed = [od * pltpu.roll(1.0 / od, HEAD_DIM, 1) for od in heads]
            pairs = [jnp.where(o_low, normed[n], normed[n + 1]) for n in range(0, ATTN_GROUP, 2)]
            o_ref[b * WINDOW:(b + 1) * WINDOW, g * GROUP_W:(g + 1) * GROUP_W] = (
                jnp.concatenate(pairs, axis=1).astype(o_ref.dtype))

    _interleave(blocks())


def _attention(sinks, attn_p, cos, sin, layer):
    s = attn_p.shape[0]
    nblk = min(ATTN_BLOCKS, s // WINDOW)
    tq = nblk * WINDOW
    row = lambda i: (i, 0)
    return pl.pallas_call(
        functools.partial(_attn_body, nblk=nblk),
        out_shape=jax.ShapeDtypeStruct((s, ATTN_W), BF16),
        grid=(s // tq,),
        in_specs=[
            pl.BlockSpec(memory_space=pltpu.SMEM),
            pl.BlockSpec((tq, ATTN_COLS), row),
            pl.BlockSpec((tq, 2 * HEAD_DIM), row),
            pl.BlockSpec((tq, 2 * HEAD_DIM), row),
        ],
        out_specs=pl.BlockSpec((tq, ATTN_W), row),
        scratch_shapes=[pltpu.VMEM((KV_W, WINDOW), BF16), pltpu.VMEM((2, WINDOW, KV_W), BF16)],
        compiler_params=_params(("arbitrary",)),
        name="swa_attention",
    )(sinks[layer], attn_p, cos, sin)


def _mlstm_body(p_ref, gates_ref, conv_ref, bias_ref, norm_ref, o_ref, xpad_ref, c_ref, n_ref, *, tb):
    step = pl.program_id(0)

    @pl.when(step == 0)
    def _():
        xpad_ref[0:SUBLANES, :] = jnp.zeros((SUBLANES, 2 * MLSTM_W), F32)
        c_ref[...] = jnp.zeros_like(c_ref)
        n_ref[...] = jnp.zeros_like(n_ref)

    xpad_ref[SUBLANES:SUBLANES + tb, :] = p_ref[:, 0:2 * MLSTM_W]
    conv = None
    for tap in range(MLSTM_CONV):
        shifted = xpad_ref[pl.ds(SUBLANES - (MLSTM_CONV - 1) + tap, tb), :]
        term = shifted * conv_ref[tap:tap + 1, :]
        conv = term if conv is None else conv + term
    xpad_ref[0:SUBLANES, :] = xpad_ref[tb:tb + SUBLANES, :]
    qk = conv * jax.nn.sigmoid(conv)
    q_all = qk[:, 0:MLSTM_W] * (HEAD_DIM ** -0.5)
    k_all = qk[:, MLSTM_W:2 * MLSTM_W]

    pre = GATE_CAP * jnp.tanh((gates_ref[...] + bias_ref[...]) / GATE_CAP)
    logsig = jnp.minimum(pre, 0.0) - jnp.log(1.0 + jnp.exp(-jnp.abs(pre)))

    g_cum = _cumsum_rows(_chunk_tri(tb), logsig)

    gate_col = lax.broadcasted_iota(jnp.int32, (tb, MLSTM_GATES), 1)
    gate_lanes = _split_dot(jnp.where(gate_col < MLSTM_HEADS, pre, g_cum), _gate_spread())
    i_all, g_all = gate_lanes[:, 0:MLSTM_W], gate_lanes[:, MLSTM_W:2 * MLSTM_W]
    head_ones = _head_ones(MLSTM_W)

    chunk_starts = range(0, tb, CHUNK)
    chunk_of = lambda x: [x[c0:c0 + CHUNK, :] for c0 in chunk_starts]
    hshift = HEAD_DIM.bit_length() - 1
    lane = lax.broadcasted_iota(jnp.int32, (CHUNK, MLSTM_W), 1)
    time = lax.broadcasted_iota(jnp.int32, (CHUNK, MLSTM_W), 0)
    key_of_lane = lane & (HEAD_DIM - 1)
    causal = key_of_lane <= time
    eye = key_of_lane == time
    brow, bcol = _tri_masks(MLSTM_W)
    same_head = (brow >> hshift) == (bcol >> hshift)
    head_block = same_head.astype(BF16)
    stack = lambda x: jnp.where(same_head, jnp.concatenate([x] * MLSTM_HEADS, axis=0), 0.0)

    def to_row(lane_bcast):
        return jnp.sum(jnp.where(eye, lane_bcast, 0.0), axis=0, keepdims=True)

    q, k, v = chunk_of(q_all), chunk_of(k_all), chunk_of(p_ref[:, 2 * MLSTM_W:3 * MLSTM_W])
    g_chunk, i_chunk = chunk_of(g_all), chunk_of(i_all)
    g_tot = [g[CHUNK - 1:CHUNK, :] for g in g_chunk]
    w_in = [jnp.exp(t - g + i) for t, g, i in zip(g_tot, g_chunk, i_chunk)]
    e_tot = [jnp.exp(t) for t in g_tot]
    e_g = [jnp.exp(g) for g in g_chunk]
    n_loc = [jnp.sum(w * x, axis=0, keepdims=True) for w, x in zip(w_in, k)]
    decay = [jnp.exp(jnp.where(causal, g - to_row(g) + to_row(i), NEG_INF)) for g, i in zip(g_chunk, i_chunk)]
    s_mat = [_mm_nt(a, stack(b)) * d for a, b, d in zip(q, k, decay)]
    num_den = [_mm(s, jnp.concatenate([stack(x).astype(BF16), head_block], axis=1))
               for s, x in zip(s_mat, v)]
    c_loc = [jnp.where(same_head, _mm_tn(w * x, b), 0.0) for w, x, b in zip(w_in, v, k)]

    c_state, n_state = c_ref[...], n_ref[0:1, :]
    c_in, n_in = [], []
    for n in range(len(chunk_starts)):
        c_in.append(c_state)
        n_in.append(n_state)
        c_state = e_tot[n] * c_state + c_loc[n]
        n_state = e_tot[n] * n_state + n_loc[n]
    c_ref[...] = c_state
    n_ref[0:1, :] = n_state

    inter = [_mm_nt(a, jnp.concatenate([c, stack(jnp.broadcast_to(x, (CHUNK, MLSTM_W)))], axis=0))
             for a, c, x in zip(q, c_in, n_in)]
    for c0, nd, e, x in zip(chunk_starts, num_den, e_g, inter):
        hid = (nd[:, :MLSTM_W] + e * x[:, :MLSTM_W]) / jnp.maximum(jnp.abs(nd[:, MLSTM_W:] + e * x[:, MLSTM_W:]), 1.0)
        mean_sq = _split_dot(hid * hid, head_ones, terms=2) * (1.0 / HEAD_DIM)
        hid = hid * lax.rsqrt(mean_sq + NORM_EPS) * norm_ref[...]
        o_gate = p_ref[c0:c0 + CHUNK, 3 * MLSTM_W:4 * MLSTM_W]
        o_ref[c0:c0 + CHUNK, :] = (jax.nn.sigmoid(o_gate) * hid).astype(o_ref.dtype)


def _mlstm(mlstm_p, gates, conv_w, bias, norm, layer, tb):
    s = mlstm_p.shape[0]
    row = lambda i: (i, 0)
    return pl.pallas_call(
        functools.partial(_mlstm_body, tb=tb),
        out_shape=jax.ShapeDtypeStruct((s, MLSTM_W), BF16),
        grid=(s // tb,),
        in_specs=[
            pl.BlockSpec((tb, MLSTM_MAIN), row),
            pl.BlockSpec((tb, MLSTM_GATES), row),
            pl.BlockSpec((None, MLSTM_CONV, 2 * MLSTM_W), lambda i: (layer, 0, 0)),
            _layer_vec(MLSTM_GATES, layer),
            _layer_vec(MLSTM_W, layer),
        ],
        out_specs=pl.BlockSpec((tb, MLSTM_W), row),
        scratch_shapes=[
            pltpu.VMEM((tb + SUBLANES, 2 * MLSTM_W), F32),
            pltpu.VMEM((MLSTM_W, MLSTM_W), F32),
            pltpu.VMEM((SUBLANES, MLSTM_W), F32),
        ],
        compiler_params=_params(("arbitrary",)),
        name="mlstm",
    )(mlstm_p, gates, conv_w, bias, norm)


def _rwkv_chunk_terms(a_t, b_t, k_t, r_t, b_end, k_end, v):
    row, col = _tri_masks(CHUNK)
    incl = col <= row
    strict = col < row
    eye = (col == row).astype(F32)
    half = HEAD_DIM

    quad = [_mm_nt(jnp.concatenate([a, r], axis=0), jnp.concatenate([b, k], axis=0))
            for a, r, b, k in zip(a_t, r_t, b_t, k_t)]
    yield
    n_mat = [jnp.where(strict, x[:CHUNK, :CHUNK], 0.0) for x in quad]
    a_ak = [jnp.where(strict, x[:CHUNK, CHUNK:], 0.0) for x in quad]
    c_rb = [jnp.where(incl, x[CHUNK:, :CHUNK], 0.0) for x in quad]
    c_rk = [jnp.where(incl, x[CHUNK:, CHUNK:], 0.0) for x in quad]

    inv = [eye + n for n in n_mat]
    power = n_mat
    for _ in range(NEUMANN_STEPS):
        power = [_mm(x, x) for x in power]
        yield
        inv = [m + _mm(m, x) for m, x in zip(inv, power)]
        yield

    z = [_mm(x, y) for x, y in zip(a_ak, v)]
    yield
    w12 = [_mm(m, jnp.concatenate([a, y], axis=1)) for m, a, y in zip(inv, a_t, z)]
    yield
    cw = [_mm(c, w) for c, w in zip(c_rb, w12)]
    yield
    ckv = [_mm(c, y) for c, y in zip(c_rk, v)]
    yield
    gh = [_mm_tn(w, b) for w, b in zip(w12, b_end)]
    yield
    vk = [_mm_tn(y, k) for y, k in zip(v, k_end)]
    yield
    q_eff = [r + x[:, :half] for r, x in zip(r_t, cw)]
    y_loc = [x[:, half:] + y for x, y in zip(cw, ckv)]
    g_mat = [x[:half, :] for x in gh]
    h_mat = [x[half:, :] + y for x, y in zip(gh, vk)]
    return q_eff, y_loc, g_mat, h_mat


def _rwkv_body(p_ref, mu_ref, w0_ref, wup_ref, a0_ref, aup_ref, gup_ref, kk_ref, ka_ref, rk_ref,
               lnw_ref, lnb_ref, o_ref, upad_ref, s_ref, stage_ref, end_ref, *, tb):
    step = pl.program_id(0)

    @pl.when(step == 0)
    def _():
        upad_ref[0:SUBLANES, :] = jnp.zeros((SUBLANES, RWKV_COLS), F32)
        s_ref[...] = jnp.zeros_like(s_ref)
        stage_ref[...] = jnp.zeros_like(stage_ref)
        end_ref[...] = jnp.zeros_like(end_ref)

    refs = (p_ref, mu_ref, w0_ref, wup_ref, a0_ref, aup_ref, gup_ref, kk_ref, ka_ref, rk_ref,
            lnw_ref, lnb_ref, o_ref, upad_ref, s_ref, stage_ref, end_ref)
    for cur in (0, 1):
        pl.when(lax.rem(step, 2) == cur)(functools.partial(_rwkv_step, *refs, tb=tb, cur=cur, prv=1 - cur))


def _rwkv_step(p_ref, mu_ref, w0_ref, wup_ref, a0_ref, aup_ref, gup_ref, kk_ref, ka_ref, rk_ref,
               lnw_ref, lnb_ref, o_ref, upad_ref, s_ref, stage_ref, end_ref, *, tb, cur, prv):
    head_ones = _head_ones(RWKV_W)
    head_sum = lambda x: _split_dot(x, head_ones, terms=2)
    head_lanes = lambda h: slice(h * HEAD_DIM, (h + 1) * HEAD_DIM)
    chunk_starts = range(0, tb, CHUNK)
    units = [(n, c0, h) for n, c0 in enumerate(chunk_starts) for h in range(RWKV_HEADS)]

    def finish_previous():
        staged = {name: stage_ref[prv, n] for n, name in enumerate(RWKV_STAGED)}
        ends = end_ref[prv]
        per_unit = {name: [staged[name][c0:c0 + CHUNK, head_lanes(h)] for _, c0, h in units]
                    for name in RWKV_STAGED[:7]}
        q_eff, y_loc, g_mat, h_mat = yield from _rwkv_chunk_terms(**per_unit)
        state = [s_ref[h] for h in range(RWKV_HEADS)]
        ys = []
        for u, (n, c0, h) in enumerate(units):
            ys.append(_mm_nt(q_eff[u], state[h]) + y_loc[u])
            state[h] = state[h] * ends[n:n + 1, head_lanes(h)] + _mm(state[h], g_mat[u]) + h_mat[u]
            if h == RWKV_HEADS - 1:
                yield
        for h in range(RWKV_HEADS):
            s_ref[h] = state[h]
        y_all = jnp.concatenate([jnp.concatenate(ys[n:n + RWKV_HEADS], axis=1)
                                 for n in range(0, len(ys), RWKV_HEADS)], axis=0)
        centred = y_all - head_sum(y_all) * (1.0 / HEAD_DIM)
        var = head_sum(centred * centred) * (1.0 / HEAD_DIM)
        normed = centred * lax.rsqrt(var + RWKV_GN_EPS) * lnw_ref[...] + lnb_ref[...]
        o_ref[...] = ((normed + staged["bonus"] * staged["v"]) * staged["gate"]).astype(o_ref.dtype)

    def stage_current():
        upad_ref[SUBLANES:SUBLANES + tb, :] = p_ref[...]
        yield
        for r0 in range(0, tb, RWKV_SLAB):
            rows = slice(r0, r0 + RWKV_SLAB)

            def stage(name, value):
                stage_ref[cur, RWKV_STAGED.index(name), rows, :] = value

            u = p_ref[rows, :]
            prev = upad_ref[pl.ds(SUBLANES - 1 + r0, RWKV_SLAB), :]
            u = u + (prev - u) * mu_ref[...]
            yield
            r_all = u[:, 0:RWKV_W]
            k_raw = u[:, RWKV_W:2 * RWKV_W]
            v_all = u[:, 2 * RWKV_W:3 * RWKV_W]
            x_w = u[:, 3 * RWKV_W:3 * RWKV_W + RWKV_W_RANK]
            x_a = u[:, 3 * RWKV_W + RWKV_W_RANK:3 * RWKV_W + RWKV_W_RANK + RWKV_A_RANK]
            x_g = u[:, 3 * RWKV_W + RWKV_W_RANK + RWKV_A_RANK:RWKV_COLS]
            z = w0_ref[...] + _mm(jnp.tanh(x_w), wup_ref[...])
            ld_all = -jnp.exp(-0.5) * jax.nn.sigmoid(z)
            a_all = jax.nn.sigmoid(a0_ref[...] + _mm(x_a, aup_ref[...]))
            yield
            kk_all = k_raw * kk_ref[...]
            k_all = k_raw * (1.0 + (a_all - 1.0) * ka_ref[...])
            kk_all = kk_all * lax.rsqrt(jnp.maximum(head_sum(kk_all * kk_all), 1e-24))
            a_vec = -kk_all
            b_vec = kk_all * a_all
            yield
            lp_all = _cumsum_rows(_chunk_tri(RWKV_SLAB), ld_all)
            stage("v", v_all)
            stage("bonus", head_sum(r_all * k_all * rk_ref[...]))
            stage("gate", _mm(jax.nn.sigmoid(x_g), gup_ref[...]))
            yield
            starts = range(0, RWKV_SLAB, CHUNK)
            lp_last = jnp.concatenate(
                [jnp.broadcast_to(lp_all[c0 + CHUNK - 1:c0 + CHUNK, :], (CHUNK, RWKV_W)) for c0 in starts], axis=0)
            grow = jnp.exp(-lp_all)
            to_end = jnp.exp(lp_last - lp_all)
            stage("a_t", a_vec * jnp.exp(lp_all - ld_all))
            stage("b_t", b_vec * grow)
            stage("k_t", k_all * grow)
            yield
            stage("r_t", r_all * jnp.exp(lp_all))
            stage("b_end", b_vec * to_end)
            stage("k_end", k_all * to_end)
            for c0 in starts:
                n = (r0 + c0) // CHUNK
                end_ref[cur, n:n + 1, :] = jnp.exp(lp_all[c0 + CHUNK - 1:c0 + CHUNK, :])
            yield
        upad_ref[0:SUBLANES, :] = upad_ref[tb:tb + SUBLANES, :]

    _interleave(finish_previous(), stage_current())


def _rwkv(rwkv_p, mu, w0, w_up, a0, a_up, g_up, k_k, k_a, r_k, ln_w, ln_b, layer, tb):
    s = rwkv_p.shape[0]
    blocks = s // tb
    lay3 = lambda i: (layer, 0, 0)
    vec = lambda width: _layer_vec(width, layer)
    return pl.pallas_call(
        functools.partial(_rwkv_body, tb=tb),
        out_shape=jax.ShapeDtypeStruct((s, RWKV_W), BF16),
        grid=(blocks + 1,),
        in_specs=[
            pl.BlockSpec((tb, RWKV_COLS), lambda i: (jnp.minimum(i, blocks - 1), 0)),
            vec(RWKV_COLS), vec(RWKV_W),
            pl.BlockSpec((None, RWKV_W_RANK, RWKV_W), lay3),
            vec(RWKV_W),
            pl.BlockSpec((None, RWKV_A_RANK, RWKV_W), lay3),
            pl.BlockSpec((None, RWKV_G_RANK, RWKV_W), lay3),
            vec(RWKV_W), vec(RWKV_W), vec(RWKV_W), vec(RWKV_W), vec(RWKV_W),
        ],
        out_specs=pl.BlockSpec((tb, RWKV_W), lambda i: (jnp.maximum(i - 1, 0), 0)),
        scratch_shapes=[
            pltpu.VMEM((tb + SUBLANES, RWKV_COLS), F32),
            pltpu.VMEM((RWKV_HEADS, HEAD_DIM, HEAD_DIM), F32),
            pltpu.VMEM((2, len(RWKV_STAGED), tb, RWKV_W), F32),
            pltpu.VMEM((2, tb // CHUNK, RWKV_W), F32),
        ],
        compiler_params=_params(("arbitrary",)),
        name="rwkv7",
    )(rwkv_p, mu, w0, w_up, a0, a_up, g_up, k_k, k_a, r_k, ln_w, ln_b)


def _tiles(s):
    return min(512, s), min(512, s)


def kernel(x, p, positions, ln_ffn1_pre, ln_ffn1_post, w_ffn1_in, w_ffn1_out, ln_mix_pre, w_in, attn_sinks, mlstm_conv, mlstm_i_bias, mlstm_f_bias, mlstm_norm, rwkv_mu, rwkv_w0, rwkv_w_up, rwkv_a0, rwkv_a_up, rwkv_g_up, rwkv_k_k, rwkv_k_a, rwkv_r_k, rwkv_ln_w, rwkv_ln_b, w_out, ln_mix_post, ln_ffn2_pre, ln_ffn2_post, w_ffn2_in, w_ffn2_out, ln_ple_pre, w_ple_gate, w_ple_proj, ln_ple_post):
    batch, seq, _ = x.shape
    assert batch == 1 and seq % WINDOW == 0
    depth = w_in.shape[0]
    tm, tb = _tiles(seq)

    bf = lambda w: w.astype(BF16)
    w_ffn1_in, w_ffn1_out, w_ffn2_in, w_ffn2_out = map(bf, (w_ffn1_in, w_ffn1_out, w_ffn2_in, w_ffn2_out))
    w_out, w_ple_gate, w_ple_proj = map(bf, (w_out, w_ple_gate, w_ple_proj))
    rwkv_w_up, rwkv_a_up, rwkv_g_up = map(bf, (rwkv_w_up, rwkv_a_up, rwkv_g_up))
    m0 = ATTN_COLS
    g0 = m0 + MLSTM_MAIN
    r0 = g0 + MLSTM_GATES
    w_proj = bf(jnp.concatenate([w_in[:, :, m0:g0], w_in[:, :, r0:], w_in[:, :, :m0], w_in[:, :, g0:r0]], axis=-1))
    vec = lambda a: a.reshape(depth, 1, -1)
    gate_bias = vec(jnp.concatenate([mlstm_i_bias, mlstm_f_bias], axis=-1))
    (ln_ffn1_pre, ln_ffn1_post, ln_mix_pre, ln_mix_post, ln_ffn2_pre, ln_ffn2_post, ln_ple_pre,
     ln_ple_post, mlstm_norm, rwkv_mu, rwkv_w0, rwkv_a0, rwkv_k_k, rwkv_k_a, rwkv_r_k, rwkv_ln_w,
     rwkv_ln_b) = map(vec, (
         ln_ffn1_pre, ln_ffn1_post, ln_mix_pre, ln_mix_post, ln_ffn2_pre, ln_ffn2_post, ln_ple_pre,
         ln_ple_post, mlstm_norm, rwkv_mu, rwkv_w0, rwkv_a0, rwkv_k_k, rwkv_k_a, rwkv_r_k, rwkv_ln_w,
         rwkv_ln_b))

    cos, sin = _rope_tables(positions.reshape(seq, 1), tm)
    xs = x.reshape(seq, D_MODEL)
    for l in range(depth):
        xs, mlstm_p, rwkv_p, attn_p, gates = _pre_mix(
            xs, ln_ffn1_pre, ln_ffn1_post, w_ffn1_in, w_ffn1_out, ln_mix_pre, w_proj, l, tm)
        y_attn = _attention(attn_sinks, attn_p, cos, sin, l)
        y_mlstm = _mlstm(mlstm_p, gates, mlstm_conv, gate_bias, mlstm_norm, l, tb)
        y_rwkv = _rwkv(rwkv_p, rwkv_mu, rwkv_w0, rwkv_w_up, rwkv_a0, rwkv_a_up, rwkv_g_up,
                       rwkv_k_k, rwkv_k_a, rwkv_r_k, rwkv_ln_w, rwkv_ln_b, l, tb)
        xs = _post_mix(y_attn, y_mlstm, y_rwkv, xs, p.reshape(depth, seq, D_PLE), w_out, ln_mix_post,
                       ln_ffn2_pre, ln_ffn2_post, w_ffn2_in, w_ffn2_out, ln_ple_pre, ln_ple_post,
                       w_ple_gate, w_ple_proj, l, tm)
    return xs.reshape(batch, seq, D_MODEL)
```

```python
import functools

import jax
import jax.numpy as jnp
from jax import lax
from jax.experimental import pallas as pl
from jax.experimental.pallas import tpu as pltpu

F32 = jnp.float32
BF16 = jnp.bfloat16

D_MODEL = 1024
HEAD_DIM = 64
D_FF = 2816
D_PLE = 256
ATTN_Q_HEADS = 8
ATTN_KV_HEADS = 2
ATTN_GROUP = ATTN_Q_HEADS // ATTN_KV_HEADS
WINDOW = 128
ROPE_THETA = 500000.0
ROPE_DIM = HEAD_DIM // 4
ROPE_HALF = ROPE_DIM // 2
MLSTM_HEADS = 4
MLSTM_CONV = 4
GATE_CAP = 15.0
RWKV_HEADS = 4
RWKV_W_RANK = 64
RWKV_A_RANK = 64
RWKV_G_RANK = 128
RWKV_GN_EPS = 64e-5
NORM_EPS = 1e-6
NEG_INF = -1e30

ATTN_W = ATTN_Q_HEADS * HEAD_DIM
KV_W = ATTN_KV_HEADS * HEAD_DIM
MLSTM_W = MLSTM_HEADS * HEAD_DIM
RWKV_W = RWKV_HEADS * HEAD_DIM
ATTN_COLS = ATTN_W + 2 * KV_W
MLSTM_MAIN = 4 * MLSTM_W
MLSTM_GATES = 2 * MLSTM_HEADS
RWKV_COLS = 3 * RWKV_W + RWKV_W_RANK + RWKV_A_RANK + RWKV_G_RANK

CHUNK = 64
NEUMANN_STEPS = 5
DENSE_HALF_MIN = 256
RWKV_SLAB = 128
RWKV_STAGED = ("a_t", "b_t", "k_t", "r_t", "b_end", "k_end", "v", "bonus", "gate")
MXU_TILE = 256
FF_SPLITS = (0, 6 * MXU_TILE, D_FF)
SUBLANES = 8
V7X_VMEM_LIMIT = 56 * 1024 * 1024


def _params(semantics):
    return pltpu.CompilerParams(dimension_semantics=semantics, vmem_limit_bytes=V7X_VMEM_LIMIT)


def _resident(block_shape, index_map):
    return pl.BlockSpec(block_shape, index_map, pipeline_mode=pl.Buffered(1))


def _layer_vec(width, layer):
    return pl.BlockSpec((None, 1, width), lambda i: (layer, 0, 0))


def _rms(x, gain):
    return x * lax.rsqrt(jnp.mean(x * x, axis=-1, keepdims=True) + NORM_EPS) * gain


def _mm(a, b):
    return jnp.dot(a.astype(BF16), b.astype(BF16), preferred_element_type=F32)


def _mm_nt(a, b):
    return lax.dot_general(a.astype(BF16), b.astype(BF16), (((1,), (1,)), ((), ())),
                           preferred_element_type=F32)


def _mm_tn(a, b):
    return lax.dot_general(a.astype(BF16), b.astype(BF16), (((0,), (0,)), ((), ())),
                           preferred_element_type=F32)


def _bf16_terms(x, terms):
    out, rest = [], x
    for n in range(terms):
        part = rest.astype(BF16)
        out.append(part)
        if n + 1 < terms:
            rest = rest - part.astype(F32)
    return out


def _cumsum_rows(tri, x):
    return sum(jnp.dot(tri, term, preferred_element_type=F32) for term in _bf16_terms(x, 3))


def _split_dot(x, ones, terms=3):
    return sum(jnp.dot(term, ones, preferred_element_type=F32) for term in _bf16_terms(x, terms))


def _head_ones(width):
    row, col = _tri_masks(width)
    shift = HEAD_DIM.bit_length() - 1
    return ((row >> shift) == (col >> shift)).astype(BF16)


def _gate_spread():
    lanes = MLSTM_GATES * HEAD_DIM
    row = lax.broadcasted_iota(jnp.int32, (MLSTM_GATES, lanes), 0)
    col = lax.broadcasted_iota(jnp.int32, (MLSTM_GATES, lanes), 1)
    return (row == (col >> (HEAD_DIM.bit_length() - 1))).astype(BF16)


def _interleave(*gens):
    results = [None] * len(gens)
    live = dict(enumerate(gens))
    while live:
        for n in list(live):
            try:
                next(live[n])
            except StopIteration as stop:
                results[n] = stop.value
                del live[n]
    return results


def _tri_masks(n):
    row = lax.broadcasted_iota(jnp.int32, (n, n), 0)
    col = lax.broadcasted_iota(jnp.int32, (n, n), 1)
    return row, col


def _chunk_tri(n):
    row, col = _tri_masks(n)
    shift = CHUNK.bit_length() - 1
    return (((row >> shift) == (col >> shift)) & (col <= row)).astype(BF16)


def _macaron_half_step(x, gpre_ref, gpost_ref, win_ref, wout_ref):
    xn = _rms(x, gpre_ref[...]).astype(BF16)
    yield
    splits = list(zip(FF_SPLITS[:-1], FF_SPLITS[1:]))
    gate_up = []
    acc = None
    for n in range(len(splits) + 1):
        if n < len(splits):
            lo, hi = splits[n]
            gate_up.append((jnp.dot(xn, win_ref[:, lo:hi], preferred_element_type=F32),
                            jnp.dot(xn, win_ref[:, D_FF + lo:D_FF + hi], preferred_element_type=F32)))
        if n > 0:
            lo, hi = splits[n - 1]
            gate, up = gate_up[n - 1]
            act = (gate * jax.nn.sigmoid(gate) * up).astype(BF16)
            yield
            part = jnp.dot(act, wout_ref[lo:hi, :], preferred_element_type=F32)
            acc = part if acc is None else acc + part
        yield
    return x + 0.5 * _rms(acc, gpost_ref[...])


def _two_row_halves(rows, tile):
    if rows < 2 * DENSE_HALF_MIN:
        return _interleave(tile(0, rows))
    parts = rows // DENSE_HALF_MIN
    size = rows // parts

    def delayed(gen, phases):
        for _ in range(phases):
            yield
        yield from gen

    return _interleave(*[delayed(tile(n * size, size), n) for n in range(parts)])


def _pre_mix_body(x_ref, gpre_ref, gpost_ref, win_ref, wout_ref, gmix_ref, wproj_ref, conv_ref, mu_ref,
                  x_out_ref, om_ref, or_ref, oa_ref, og_ref, qk_pad_ref, u_pad_ref):
    @pl.when(pl.program_id(0) == 0)
    def _():
        qk_pad_ref[0:SUBLANES, :] = jnp.zeros((SUBLANES, 2 * MLSTM_W), F32)
        u_pad_ref[0:SUBLANES, :] = jnp.zeros((SUBLANES, RWKV_COLS), F32)

    def tile(r0, n):
        rows = slice(r0, r0 + n)
        x = yield from _macaron_half_step(x_ref[rows, :], gpre_ref, gpost_ref, win_ref, wout_ref)
        x_out_ref[rows, :] = x
        h = _rms(x, gmix_ref[...]).astype(BF16)
        yield
        starts = [sum(PROJ_WIDTHS[:g]) for g in range(len(PROJ_WIDTHS))]
        proj = [jnp.dot(h, wproj_ref[:, lo:lo + w], preferred_element_type=F32) for lo, w in zip(starts, PROJ_WIDTHS)]
        p_mlstm, p_rwkv, p_attn, p_gates = proj
        oa_ref[rows, :] = p_attn
        og_ref[rows, :] = p_gates
        yield
        qk_pad_ref[SUBLANES:SUBLANES + n, :] = p_mlstm[:, 0:2 * MLSTM_W]
        conv = None
        for tap in range(MLSTM_CONV):
            shifted = qk_pad_ref[pl.ds(SUBLANES - (MLSTM_CONV - 1) + tap, n), :]
            term = shifted * conv_ref[tap:tap + 1, :]
            conv = term if conv is None else conv + term
        qk_pad_ref[0:SUBLANES, :] = qk_pad_ref[n:n + SUBLANES, :]
        qk = conv * jax.nn.sigmoid(conv)
        om_ref[rows, 0:MLSTM_W] = qk[:, 0:MLSTM_W] * (HEAD_DIM ** -0.5)
        om_ref[rows, MLSTM_W:2 * MLSTM_W] = qk[:, MLSTM_W:2 * MLSTM_W]
        om_ref[rows, 2 * MLSTM_W:] = p_mlstm[:, 2 * MLSTM_W:]
        yield
        u_pad_ref[SUBLANES:SUBLANES + n, :] = p_rwkv
        prev = u_pad_ref[pl.ds(SUBLANES - 1, n), :]
        u_pad_ref[0:SUBLANES, :] = u_pad_ref[n:n + SUBLANES, :]
        or_ref[rows, :] = p_rwkv + (prev - p_rwkv) * mu_ref[...]

    _two_row_halves(x_ref.shape[0], tile)


PROJ_WIDTHS = (MLSTM_MAIN, RWKV_COLS, ATTN_COLS, MLSTM_GATES)


def _pre_mix(x, gpre, gpost, w_ffn_in, w_ffn_out, gmix, w_proj, conv_w, rwkv_mu, layer, tm):
    s = x.shape[0]
    row = lambda i: (i, 0)
    lay3 = lambda i: (layer, 0, 0)
    widths = PROJ_WIDTHS
    part = tm if tm < 2 * DENSE_HALF_MIN else DENSE_HALF_MIN
    return pl.pallas_call(
        _pre_mix_body,
        out_shape=[jax.ShapeDtypeStruct((s, D_MODEL), F32)] + [jax.ShapeDtypeStruct((s, w), F32) for w in widths],
        grid=(s // tm,),
        in_specs=[
            pl.BlockSpec((tm, D_MODEL), row),
            _layer_vec(D_MODEL, layer),
            _layer_vec(D_MODEL, layer),
            _resident((None, D_MODEL, 2 * D_FF), lay3),
            _resident((None, D_FF, D_MODEL), lay3),
            _layer_vec(D_MODEL, layer),
            _resident((None, D_MODEL, sum(widths)), lay3),
            pl.BlockSpec((None, MLSTM_CONV, 2 * MLSTM_W), lay3),
            _layer_vec(RWKV_COLS, layer),
        ],
        out_specs=[pl.BlockSpec((tm, D_MODEL), row)] + [pl.BlockSpec((tm, w), row) for w in widths],
        scratch_shapes=[
            pltpu.VMEM((part + SUBLANES, 2 * MLSTM_W), F32),
            pltpu.VMEM((part + SUBLANES, RWKV_COLS), F32),
        ],
        compiler_params=_params(("arbitrary",)),
        name="pre_mix",
    )(x, gpre, gpost, w_ffn_in, w_ffn_out, gmix, w_proj, conv_w, rwkv_mu)


def _post_mix_body(ya_ref, ym_ref, yr_ref, x_ref, p_ref, wo_ref, gmix_ref, gpre_ref, gpost_ref, win_ref, wout_ref,
                   gple_pre_ref, gple_post_ref, wgate_ref, wemb_ref, o_ref):
    def tile(r0, n):
        rows = slice(r0, r0 + n)
        mix = jnp.dot(ya_ref[rows, :], wo_ref[0:ATTN_W, :], preferred_element_type=F32)
        mix += jnp.dot(ym_ref[rows, :], wo_ref[ATTN_W:ATTN_W + MLSTM_W, :], preferred_element_type=F32)
        mix += jnp.dot(yr_ref[rows, :], wo_ref[ATTN_W + MLSTM_W:, :], preferred_element_type=F32)
        emb = jnp.dot(p_ref[rows, :].astype(BF16), wemb_ref[...], preferred_element_type=F32)
        yield
        x = x_ref[rows, :] + _rms(mix, gmix_ref[...])
        x = yield from _macaron_half_step(x, gpre_ref, gpost_ref, win_ref, wout_ref)
        h = _rms(x, gple_pre_ref[...]).astype(BF16)
        yield
        gate = jax.nn.sigmoid(jnp.dot(h, wgate_ref[...], preferred_element_type=F32))
        yield
        o_ref[rows, :] = x + _rms(gate * emb, gple_post_ref[...])

    _two_row_halves(x_ref.shape[0], tile)


def _post_mix(ya, ym, yr, x, p, w_out, gmix, gpre, gpost, w_ffn_in, w_ffn_out, gple_pre, gple_post,
              w_gate, w_emb, layer, tm):
    s = x.shape[0]
    row = lambda i: (i, 0)
    lay3 = lambda i: (layer, 0, 0)
    vec = _layer_vec(D_MODEL, layer)
    return pl.pallas_call(
        _post_mix_body,
        out_shape=jax.ShapeDtypeStruct((s, D_MODEL), F32),
        grid=(s // tm,),
        in_specs=[
            pl.BlockSpec((tm, ATTN_W), row),
            pl.BlockSpec((tm, MLSTM_W), row),
            pl.BlockSpec((tm, RWKV_W), row),
            pl.BlockSpec((tm, D_MODEL), row),
            pl.BlockSpec((None, tm, D_PLE), lambda i: (layer, i, 0)),
            _resident((None, D_MODEL, D_MODEL), lay3),
            vec, vec, vec,
            _resident((None, D_MODEL, 2 * D_FF), lay3),
            _resident((None, D_FF, D_MODEL), lay3),
            vec, vec,
            _resident((None, D_MODEL, D_MODEL), lay3),
            _resident((None, D_PLE, D_MODEL), lay3),
        ],
        out_specs=pl.BlockSpec((tm, D_MODEL), row),
        compiler_params=_params(("parallel",)),
        name="post_mix",
    )(ya, ym, yr, x, p, w_out, gmix, gpre, gpost, w_ffn_in, w_ffn_out, gple_pre, gple_post, w_gate, w_emb)


def _rope_body(pos_ref, invf_ref, cos_ref, sin_ref):
    ang = pos_ref[...].astype(F32) * invf_ref[...]
    sin = jnp.sin(ang)
    dim = lax.broadcasted_iota(jnp.int32, ang.shape, 1) & (HEAD_DIM - 1)
    cos_ref[...] = jnp.cos(ang)
    sin_ref[...] = jnp.where(dim < ROPE_HALF, -sin, jnp.where(dim < ROPE_DIM, sin, 0.0))


def _rope_tables(positions, tm):
    s = positions.shape[0]
    lane = jnp.arange(2 * HEAD_DIM) % HEAD_DIM
    freq = ROPE_THETA ** (-jnp.arange(0, ROPE_DIM, 2, dtype=F32) / ROPE_DIM)
    invf = jnp.where(lane < ROPE_DIM, freq[lane % ROPE_HALF], 0.0).astype(F32)[None, :]
    row = lambda i: (i, 0)
    out = jax.ShapeDtypeStruct((s, 2 * HEAD_DIM), F32)
    return pl.pallas_call(
        _rope_body,
        out_shape=[out, out],
        grid=(s // tm,),
        in_specs=[pl.BlockSpec((tm, 1), row), pl.BlockSpec((1, 2 * HEAD_DIM), lambda i: (0, 0))],
        out_specs=[pl.BlockSpec((tm, 2 * HEAD_DIM), row)] * 2,
        compiler_params=_params(("parallel",)),
        name="rope_tables",
    )(positions, invf)


ATTN_BLOCKS = 8
GROUP_W = ATTN_GROUP * HEAD_DIM
KEYS = 2 * WINDOW


def _rope_partner():
    row, col = _tri_masks(2 * HEAD_DIM)
    dim = col & (HEAD_DIM - 1)
    first = (dim < ROPE_HALF) & (row == col + ROPE_HALF)
    second = (dim >= ROPE_HALF) & (dim < ROPE_DIM) & (row == col - ROPE_HALF)
    return (first | second).astype(BF16)


def _attn_body(sinks_ref, p_ref, cos_ref, sin_ref, o_ref, ktprev_ref, vprev_ref, *, nblk):
    step = pl.program_id(0)

    @pl.when(step == 0)
    def _():
        ktprev_ref[...] = jnp.zeros_like(ktprev_ref)
        vprev_ref[...] = jnp.zeros_like(vprev_ref)

    cos, sin = cos_ref[...], sin_ref[...]
    partner = _rope_partner()
    period = 2 * HEAD_DIM

    def rope(x):
        slabs = [x[:, c:c + period] for c in range(0, x.shape[1], period)]
        return jnp.concatenate(
            [s * cos + jnp.dot(s.astype(BF16), partner, preferred_element_type=F32) * sin for s in slabs], axis=1)

    q = (rope(p_ref[:, 0:ATTN_W]) * (HEAD_DIM ** -0.5)).astype(BF16)
    k_cur = rope(p_ref[:, ATTN_W:ATTN_W + KV_W])
    v_f32 = p_ref[:, ATTN_W + KV_W:ATTN_COLS]
    kt = jnp.concatenate([ktprev_ref[...], k_cur.T.astype(BF16)], axis=1)
    vv = jnp.concatenate([vprev_ref[0], v_f32.astype(BF16)], axis=0)
    vs = jnp.concatenate([vprev_ref[1], pltpu.roll(v_f32, HEAD_DIM, 1).astype(BF16)], axis=0)
    ktprev_ref[...] = kt[:, nblk * WINDOW:]
    vprev_ref[0] = vv[nblk * WINDOW:, :]
    vprev_ref[1] = vs[nblk * WINDOW:, :]

    t = lax.broadcasted_iota(jnp.int32, (WINDOW, ATTN_GROUP * KEYS), 0)
    lane = lax.broadcasted_iota(jnp.int32, (WINDOW, ATTN_GROUP * KEYS), 1)
    c = lane & (KEYS - 1)
    head = lane >> (KEYS.bit_length() - 1)
    cur_ok = (c >= WINDOW) & (c - WINDOW <= t)
    prev_ok = (c < WINDOW) & (c > t)

    def bias(g, has_prev):
        b = jnp.where(cur_ok | (prev_ok & has_prev), 0.0, NEG_INF)
        sink = jnp.zeros_like(b)
        for n in range(ATTN_GROUP):
            sink = jnp.where(head == n, sinks_ref[g * ATTN_GROUP + n], sink)
        return jnp.where(c == 0, sink, b)

    bias_first = [bias(g, step > 0) for g in range(ATTN_KV_HEADS)]
    bias_rest = [bias(g, True) for g in range(ATTN_KV_HEADS)]

    key_lane = lax.broadcasted_iota(jnp.int32, (HEAD_DIM, KEYS), 1)
    zero_kt = jnp.zeros((HEAD_DIM, KEYS), BF16)
    vrow = lax.broadcasted_iota(jnp.int32, (KEYS, KV_W), 0)
    v_low = lax.broadcasted_iota(jnp.int32, (KEYS, KV_W), 1) < HEAD_DIM
    o_low = lax.broadcasted_iota(jnp.int32, (WINDOW, KV_W), 1) < HEAD_DIM
    one = jnp.ones((KEYS, KV_W), BF16)

    def blocks():
        units = [(b, g) for b in range(nblk) for g in range(ATTN_KV_HEADS)]
        kbd, v_even, v_odd = [], [], []
        for b, g in units:
            ktg = kt[g * HEAD_DIM:(g + 1) * HEAD_DIM, b * WINDOW:b * WINDOW + KEYS]
            ktg = jnp.where(key_lane == 0, jnp.zeros_like(ktg), ktg)
            kbd.append(jnp.concatenate(
                [jnp.concatenate([ktg if m == n else zero_kt for m in range(ATTN_GROUP)], axis=1)
                 for n in range(ATTN_GROUP)], axis=0))
            window = slice(b * WINDOW, b * WINDOW + KEYS)
            in_low, in_high = (vv, vs) if g == 0 else (vs, vv)
            v_even.append(jnp.where(v_low, jnp.where(vrow == 0, jnp.zeros_like(one), in_low[window, :]), one))
            v_odd.append(jnp.where(v_low, one, jnp.where(vrow == 0, jnp.zeros_like(one), in_high[window, :])))
        yield
        scores = [jnp.dot(q[b * WINDOW:(b + 1) * WINDOW, g * GROUP_W:(g + 1) * GROUP_W], kb,
                          preferred_element_type=F32) + (bias_first[g] if b == 0 else bias_rest[g])
                  for (b, g), kb in zip(units, kbd)]
        yield
        probs = []
        for s in scores:
            parts = []
            for n in range(ATTN_GROUP):
                sn = s[:, n * KEYS:(n + 1) * KEYS]
                parts.append(jnp.exp(sn - jnp.max(sn, axis=-1, keepdims=True)).astype(BF16))
            probs.append(parts)
        yield
        outs = [[jnp.dot(pn, even if n % 2 == 0 else odd, preferred_element_type=F32) for n, pn in enumerate(parts)]
                for parts, even, odd in zip(probs, v_even, v_odd)]
        yield
        for (b, g), heads in zip(units, outs):
            normed = [od * pltpu.roll(1.0 / od, HEAD_DIM, 1) for od in heads]
            pairs = [jnp.where(o_low, normed[n], normed[n + 1]) for n in range(0, ATTN_GROUP, 2)]
            o_ref[b * WINDOW:(b + 1) * WINDOW, g * GROUP_W:(g + 1) * GROUP_W] = (
                jnp.concatenate(pairs, axis=1).astype(o_ref.dtype))

    _interleave(blocks())


def _attention(sinks, attn_p, cos, sin, layer):
    s = attn_p.shape[0]
    nblk = min(ATTN_BLOCKS, s // WINDOW)
    tq = nblk * WINDOW
    row = lambda i: (i, 0)
    return pl.pallas_call(
        functools.partial(_attn_body, nblk=nblk),
        out_shape=jax.ShapeDtypeStruct((s, ATTN_W), BF16),
        grid=(s // tq,),
        in_specs=[
            pl.BlockSpec(memory_space=pltpu.SMEM),
            pl.BlockSpec((tq, ATTN_COLS), row),
            pl.BlockSpec((tq, 2 * HEAD_DIM), row),
            pl.BlockSpec((tq, 2 * HEAD_DIM), row),
        ],
        out_specs=pl.BlockSpec((tq, ATTN_W), row),
        scratch_shapes=[pltpu.VMEM((KV_W, WINDOW), BF16), pltpu.VMEM((2, WINDOW, KV_W), BF16)],
        compiler_params=_params(("arbitrary",)),
        name="swa_attention",
    )(sinks[layer], attn_p, cos, sin)


def _mlstm_body(p_ref, gates_ref, bias_ref, norm_ref, o_ref, c_ref, n_ref, *, tb):
    step = pl.program_id(0)

    @pl.when(step == 0)
    def _():
        c_ref[...] = jnp.zeros_like(c_ref)
        n_ref[...] = jnp.zeros_like(n_ref)

    q_all = p_ref[:, 0:MLSTM_W]
    k_all = p_ref[:, MLSTM_W:2 * MLSTM_W]

    pre = GATE_CAP * jnp.tanh((gates_ref[...] + bias_ref[...]) / GATE_CAP)
    logsig = jnp.minimum(pre, 0.0) - jnp.log(1.0 + jnp.exp(-jnp.abs(pre)))

    g_cum = _cumsum_rows(_chunk_tri(tb), logsig)

    gate_col = lax.broadcasted_iota(jnp.int32, (tb, MLSTM_GATES), 1)
    gate_lanes = _split_dot(jnp.where(gate_col < MLSTM_HEADS, pre, g_cum), _gate_spread())
    i_all, g_all = gate_lanes[:, 0:MLSTM_W], gate_lanes[:, MLSTM_W:2 * MLSTM_W]
    head_ones = _head_ones(MLSTM_W)

    chunk_starts = range(0, tb, CHUNK)
    chunk_of = lambda x: [x[c0:c0 + CHUNK, :] for c0 in chunk_starts]
    hshift = HEAD_DIM.bit_length() - 1
    lane = lax.broadcasted_iota(jnp.int32, (CHUNK, MLSTM_W), 1)
    time = lax.broadcasted_iota(jnp.int32, (CHUNK, MLSTM_W), 0)
    key_of_lane = lane & (HEAD_DIM - 1)
    causal = key_of_lane <= time
    eye = key_of_lane == time
    brow, bcol = _tri_masks(MLSTM_W)
    same_head = (brow >> hshift) == (bcol >> hshift)
    head_block = same_head.astype(BF16)
    stack = lambda x: jnp.where(same_head, jnp.concatenate([x] * MLSTM_HEADS, axis=0), 0.0)

    def to_row(lane_bcast):
        return jnp.sum(jnp.where(eye, lane_bcast, 0.0), axis=0, keepdims=True)

    q, k, v = chunk_of(q_all), chunk_of(k_all), chunk_of(p_ref[:, 2 * MLSTM_W:3 * MLSTM_W])
    g_chunk, i_chunk = chunk_of(g_all), chunk_of(i_all)
    g_tot = [g[CHUNK - 1:CHUNK, :] for g in g_chunk]
    w_in = [jnp.exp(t - g + i) for t, g, i in zip(g_tot, g_chunk, i_chunk)]
    e_tot = [jnp.exp(t) for t in g_tot]
    e_g = [jnp.exp(g) for g in g_chunk]
    n_loc = [jnp.sum(w * x, axis=0, keepdims=True) for w, x in zip(w_in, k)]
    decay = [jnp.exp(jnp.where(causal, g - to_row(g) + to_row(i), NEG_INF)) for g, i in zip(g_chunk, i_chunk)]
    s_mat = [_mm_nt(a, stack(b)) * d for a, b, d in zip(q, k, decay)]
    num_den = [_mm(s, jnp.concatenate([stack(x).astype(BF16), head_block], axis=1))
               for s, x in zip(s_mat, v)]
    c_loc = [jnp.where(same_head, _mm_tn(w * x, b), 0.0) for w, x, b in zip(w_in, v, k)]

    c_state, n_state = c_ref[...], n_ref[0:1, :]
    c_in, n_in = [], []
    for n in range(len(chunk_starts)):
        c_in.append(c_state)
        n_in.append(n_state)
        c_state = e_tot[n] * c_state + c_loc[n]
        n_state = e_tot[n] * n_state + n_loc[n]
    c_ref[...] = c_state
    n_ref[0:1, :] = n_state

    inter = [_mm_nt(a, jnp.concatenate([c, stack(jnp.broadcast_to(x, (CHUNK, MLSTM_W)))], axis=0))
             for a, c, x in zip(q, c_in, n_in)]
    for c0, nd, e, x in zip(chunk_starts, num_den, e_g, inter):
        hid = (nd[:, :MLSTM_W] + e * x[:, :MLSTM_W]) / jnp.maximum(jnp.abs(nd[:, MLSTM_W:] + e * x[:, MLSTM_W:]), 1.0)
        mean_sq = _split_dot(hid * hid, head_ones, terms=2) * (1.0 / HEAD_DIM)
        hid = hid * lax.rsqrt(mean_sq + NORM_EPS) * norm_ref[...]
        o_gate = p_ref[c0:c0 + CHUNK, 3 * MLSTM_W:4 * MLSTM_W]
        o_ref[c0:c0 + CHUNK, :] = (jax.nn.sigmoid(o_gate) * hid).astype(o_ref.dtype)


def _mlstm(mlstm_p, gates, bias, norm, layer, tb):
    s = mlstm_p.shape[0]
    row = lambda i: (i, 0)
    return pl.pallas_call(
        functools.partial(_mlstm_body, tb=tb),
        out_shape=jax.ShapeDtypeStruct((s, MLSTM_W), BF16),
        grid=(s // tb,),
        in_specs=[
            pl.BlockSpec((tb, MLSTM_MAIN), row),
            pl.BlockSpec((tb, MLSTM_GATES), row),
            _layer_vec(MLSTM_GATES, layer),
            _layer_vec(MLSTM_W, layer),
        ],
        out_specs=pl.BlockSpec((tb, MLSTM_W), row),
        scratch_shapes=[
            pltpu.VMEM((MLSTM_W, MLSTM_W), F32),
            pltpu.VMEM((SUBLANES, MLSTM_W), F32),
        ],
        compiler_params=_params(("arbitrary",)),
        name="mlstm",
    )(mlstm_p, gates, bias, norm)


def _rwkv_chunk_terms(a_t, b_t, k_t, r_t, b_end, k_end, v):
    row, col = _tri_masks(CHUNK)
    incl = col <= row
    strict = col < row
    eye = (col == row).astype(F32)
    half = HEAD_DIM

    quad = [_mm_nt(jnp.concatenate([a, r], axis=0), jnp.concatenate([b, k], axis=0))
            for a, r, b, k in zip(a_t, r_t, b_t, k_t)]
    yield
    n_mat = [jnp.where(strict, x[:CHUNK, :CHUNK], 0.0) for x in quad]
    a_ak = [jnp.where(strict, x[:CHUNK, CHUNK:], 0.0) for x in quad]
    c_rb = [jnp.where(incl, x[CHUNK:, :CHUNK], 0.0) for x in quad]
    c_rk = [jnp.where(incl, x[CHUNK:, CHUNK:], 0.0) for x in quad]

    inv = [eye + n for n in n_mat]
    power = n_mat
    for _ in range(NEUMANN_STEPS):
        power = [_mm(x, x) for x in power]
        yield
        inv = [m + _mm(m, x) for m, x in zip(inv, power)]
        yield

    z = [_mm(x, y) for x, y in zip(a_ak, v)]
    yield
    w12 = [_mm(m, jnp.concatenate([a, y], axis=1)) for m, a, y in zip(inv, a_t, z)]
    yield
    cw = [_mm(c, w) for c, w in zip(c_rb, w12)]
    yield
    ckv = [_mm(c, y) for c, y in zip(c_rk, v)]
    yield
    gh = [_mm_tn(w, b) for w, b in zip(w12, b_end)]
    yield
    vk = [_mm_tn(y, k) for y, k in zip(v, k_end)]
    yield
    q_eff = [r + x[:, :half] for r, x in zip(r_t, cw)]
    y_loc = [x[:, half:] + y for x, y in zip(cw, ckv)]
    g_mat = [x[:half, :] for x in gh]
    h_mat = [x[half:, :] + y for x, y in zip(gh, vk)]
    return q_eff, y_loc, g_mat, h_mat


def _rwkv_body(p_ref, w0_ref, wup_ref, a0_ref, aup_ref, gup_ref, kk_ref, ka_ref, rk_ref,
               lnw_ref, lnb_ref, o_ref, s_ref, stage_ref, end_ref, *, tb):
    step = pl.program_id(0)

    @pl.when(step == 0)
    def _():
        s_ref[...] = jnp.zeros_like(s_ref)
        stage_ref[...] = jnp.zeros_like(stage_ref)
        end_ref[...] = jnp.zeros_like(end_ref)

    refs = (p_ref, w0_ref, wup_ref, a0_ref, aup_ref, gup_ref, kk_ref, ka_ref, rk_ref,
            lnw_ref, lnb_ref, o_ref, s_ref, stage_ref, end_ref)
    for cur in (0, 1):
        pl.when(lax.rem(step, 2) == cur)(functools.partial(_rwkv_step, *refs, tb=tb, cur=cur, prv=1 - cur))


def _rwkv_step(p_ref, w0_ref, wup_ref, a0_ref, aup_ref, gup_ref, kk_ref, ka_ref, rk_ref,
               lnw_ref, lnb_ref, o_ref, s_ref, stage_ref, end_ref, *, tb, cur, prv):
    head_ones = _head_ones(RWKV_W)
    head_sum = lambda x: _split_dot(x, head_ones, terms=2)
    head_lanes = lambda h: slice(h * HEAD_DIM, (h + 1) * HEAD_DIM)
    chunk_starts = range(0, tb, CHUNK)
    units = [(n, c0, h) for n, c0 in enumerate(chunk_starts) for h in range(RWKV_HEADS)]

    def finish_previous():
        staged = {name: stage_ref[prv, n] for n, name in enumerate(RWKV_STAGED)}
        ends = end_ref[prv]
        per_unit = {name: [staged[name][c0:c0 + CHUNK, head_lanes(h)] for _, c0, h in units]
                    for name in RWKV_STAGED[:7]}
        q_eff, y_loc, g_mat, h_mat = yield from _rwkv_chunk_terms(**per_unit)
        state = [s_ref[h] for h in range(RWKV_HEADS)]
        ys = []
        for u, (n, c0, h) in enumerate(units):
            ys.append(_mm_nt(q_eff[u], state[h]) + y_loc[u])
            state[h] = state[h] * ends[n:n + 1, head_lanes(h)] + _mm(state[h], g_mat[u]) + h_mat[u]
            if h == RWKV_HEADS - 1:
                yield
        for h in range(RWKV_HEADS):
            s_ref[h] = state[h]
        y_all = jnp.concatenate([jnp.concatenate(ys[n:n + RWKV_HEADS], axis=1)
                                 for n in range(0, len(ys), RWKV_HEADS)], axis=0)
        centred = y_all - head_sum(y_all) * (1.0 / HEAD_DIM)
        var = head_sum(centred * centred) * (1.0 / HEAD_DIM)
        normed = centred * lax.rsqrt(var + RWKV_GN_EPS) * lnw_ref[...] + lnb_ref[...]
        o_ref[...] = ((normed + staged["bonus"] * staged["v"]) * staged["gate"]).astype(o_ref.dtype)

    def stage_current():
        for r0 in range(0, tb, RWKV_SLAB):
            rows = slice(r0, r0 + RWKV_SLAB)

            def stage(name, value):
                stage_ref[cur, RWKV_STAGED.index(name), rows, :] = value

            u = p_ref[rows, :]
            yield
            r_all = u[:, 0:RWKV_W]
            k_raw = u[:, RWKV_W:2 * RWKV_W]
            v_all = u[:, 2 * RWKV_W:3 * RWKV_W]
            x_w = u[:, 3 * RWKV_W:3 * RWKV_W + RWKV_W_RANK]
            x_a = u[:, 3 * RWKV_W + RWKV_W_RANK:3 * RWKV_W + RWKV_W_RANK + RWKV_A_RANK]
            x_g = u[:, 3 * RWKV_W + RWKV_W_RANK + RWKV_A_RANK:RWKV_COLS]
            z = w0_ref[...] + _mm(jnp.tanh(x_w), wup_ref[...])
            ld_all = -jnp.exp(-0.5) * jax.nn.sigmoid(z)
            a_all = jax.nn.sigmoid(a0_ref[...] + _mm(x_a, aup_ref[...]))
            yield
            kk_all = k_raw * kk_ref[...]
            k_all = k_raw * (1.0 + (a_all - 1.0) * ka_ref[...])
            kk_all = kk_all * lax.rsqrt(jnp.maximum(head_sum(kk_all * kk_all), 1e-24))
            a_vec = -kk_all
            b_vec = kk_all * a_all
            yield
            lp_all = _cumsum_rows(_chunk_tri(RWKV_SLAB), ld_all)
            stage("v", v_all)
            stage("bonus", head_sum(r_all * k_all * rk_ref[...]))
            stage("gate", _mm(jax.nn.sigmoid(x_g), gup_ref[...]))
            yield
            starts = range(0, RWKV_SLAB, CHUNK)
            lp_last = jnp.concatenate(
                [jnp.broadcast_to(lp_all[c0 + CHUNK - 1:c0 + CHUNK, :], (CHUNK, RWKV_W)) for c0 in starts], axis=0)
            grow = jnp.exp(-lp_all)
            to_end = jnp.exp(lp_last - lp_all)
            stage("a_t", a_vec * jnp.exp(lp_all - ld_all))
            stage("b_t", b_vec * grow)
            stage("k_t", k_all * grow)
            yield
            stage("r_t", r_all * jnp.exp(lp_all))
            stage("b_end", b_vec * to_end)
            stage("k_end", k_all * to_end)
            for c0 in starts:
                n = (r0 + c0) // CHUNK
                end_ref[cur, n:n + 1, :] = jnp.exp(lp_all[c0 + CHUNK - 1:c0 + CHUNK, :])
            yield

    _interleave(finish_previous(), stage_current())


def _rwkv(rwkv_p, w0, w_up, a0, a_up, g_up, k_k, k_a, r_k, ln_w, ln_b, layer, tb):
    s = rwkv_p.shape[0]
    blocks = s // tb
    lay3 = lambda i: (layer, 0, 0)
    vec = lambda width: _layer_vec(width, layer)
    return pl.pallas_call(
        functools.partial(_rwkv_body, tb=tb),
        out_shape=jax.ShapeDtypeStruct((s, RWKV_W), BF16),
        grid=(blocks + 1,),
        in_specs=[
            pl.BlockSpec((tb, RWKV_COLS), lambda i: (jnp.minimum(i, blocks - 1), 0)),
            vec(RWKV_W),
            pl.BlockSpec((None, RWKV_W_RANK, RWKV_W), lay3),
            vec(RWKV_W),
            pl.BlockSpec((None, RWKV_A_RANK, RWKV_W), lay3),
            pl.BlockSpec((None, RWKV_G_RANK, RWKV_W), lay3),
            vec(RWKV_W), vec(RWKV_W), vec(RWKV_W), vec(RWKV_W), vec(RWKV_W),
        ],
        out_specs=pl.BlockSpec((tb, RWKV_W), lambda i: (jnp.maximum(i - 1, 0), 0)),
        scratch_shapes=[
            pltpu.VMEM((RWKV_HEADS, HEAD_DIM, HEAD_DIM), F32),
            pltpu.VMEM((2, len(RWKV_STAGED), tb, RWKV_W), F32),
            pltpu.VMEM((2, tb // CHUNK, RWKV_W), F32),
        ],
        compiler_params=_params(("arbitrary",)),
        name="rwkv7",
    )(rwkv_p, w0, w_up, a0, a_up, g_up, k_k, k_a, r_k, ln_w, ln_b)


def _tiles(s):
    return min(512, s), min(512, s)


def kernel(x, p, positions, ln_ffn1_pre, ln_ffn1_post, w_ffn1_in, w_ffn1_out, ln_mix_pre, w_in, attn_sinks, mlstm_conv, mlstm_i_bias, mlstm_f_bias, mlstm_norm, rwkv_mu, rwkv_w0, rwkv_w_up, rwkv_a0, rwkv_a_up, rwkv_g_up, rwkv_k_k, rwkv_k_a, rwkv_r_k, rwkv_ln_w, rwkv_ln_b, w_out, ln_mix_post, ln_ffn2_pre, ln_ffn2_post, w_ffn2_in, w_ffn2_out, ln_ple_pre, w_ple_gate, w_ple_proj, ln_ple_post):
    batch, seq, _ = x.shape
    assert batch == 1 and seq % WINDOW == 0
    depth = w_in.shape[0]
    tm, tb = _tiles(seq)

    bf = lambda w: w.astype(BF16)
    w_ffn1_in, w_ffn1_out, w_ffn2_in, w_ffn2_out = map(bf, (w_ffn1_in, w_ffn1_out, w_ffn2_in, w_ffn2_out))
    w_out, w_ple_gate, w_ple_proj = map(bf, (w_out, w_ple_gate, w_ple_proj))
    rwkv_w_up, rwkv_a_up, rwkv_g_up = map(bf, (rwkv_w_up, rwkv_a_up, rwkv_g_up))
    m0 = ATTN_COLS
    g0 = m0 + MLSTM_MAIN
    r0 = g0 + MLSTM_GATES
    w_proj = bf(jnp.concatenate([w_in[:, :, m0:g0], w_in[:, :, r0:], w_in[:, :, :m0], w_in[:, :, g0:r0]], axis=-1))
    vec = lambda a: a.reshape(depth, 1, -1)
    gate_bias = vec(jnp.concatenate([mlstm_i_bias, mlstm_f_bias], axis=-1))
    (ln_ffn1_pre, ln_ffn1_post, ln_mix_pre, ln_mix_post, ln_ffn2_pre, ln_ffn2_post, ln_ple_pre,
     ln_ple_post, mlstm_norm, rwkv_mu, rwkv_w0, rwkv_a0, rwkv_k_k, rwkv_k_a, rwkv_r_k, rwkv_ln_w,
     rwkv_ln_b) = map(vec, (
         ln_ffn1_pre, ln_ffn1_post, ln_mix_pre, ln_mix_post, ln_ffn2_pre, ln_ffn2_post, ln_ple_pre,
         ln_ple_post, mlstm_norm, rwkv_mu, rwkv_w0, rwkv_a0, rwkv_k_k, rwkv_k_a, rwkv_r_k, rwkv_ln_w,
         rwkv_ln_b))

    cos, sin = _rope_tables(positions.reshape(seq, 1), tm)
    xs = x.reshape(seq, D_MODEL)
    for l in range(depth):
        xs, mlstm_p, rwkv_p, attn_p, gates = _pre_mix(
            xs, ln_ffn1_pre, ln_ffn1_post, w_ffn1_in, w_ffn1_out, ln_mix_pre, w_proj, mlstm_conv, rwkv_mu, l, tm)
        y_attn = _attention(attn_sinks, attn_p, cos, sin, l)
        y_mlstm = _mlstm(mlstm_p, gates, gate_bias, mlstm_norm, l, tb)
        y_rwkv = _rwkv(rwkv_p, rwkv_w0, rwkv_w_up, rwkv_a0, rwkv_a_up, rwkv_g_up,
                       rwkv_k_k, rwkv_k_a, rwkv_r_k, rwkv_ln_w, rwkv_ln_b, l, tb)
        xs = _post_mix(y_attn, y_mlstm, y_rwkv, xs, p.reshape(depth, seq, D_PLE), w_out, ln_mix_post,
                       ln_ffn2_pre, ln_ffn2_post, w_ffn2_in, w_ffn2_out, ln_ple_pre, ln_ple_post,
                       w_ple_gate, w_ple_proj, l, tm)
    return xs.reshape(batch, seq, D_MODEL)
```

```python
import functools

import jax
import jax.numpy as jnp
from jax import lax
from jax.experimental import pallas as pl
from jax.experimental.pallas import tpu as pltpu

F32 = jnp.float32
BF16 = jnp.bfloat16

D_MODEL = 1024
HEAD_DIM = 64
D_FF = 2816
D_PLE = 256
ATTN_Q_HEADS = 8
ATTN_KV_HEADS = 2
ATTN_GROUP = ATTN_Q_HEADS // ATTN_KV_HEADS
WINDOW = 128
ROPE_THETA = 500000.0
ROPE_DIM = HEAD_DIM // 4
ROPE_HALF = ROPE_DIM // 2
MLSTM_HEADS = 4
MLSTM_CONV = 4
GATE_CAP = 15.0
RWKV_HEADS = 4
RWKV_W_RANK = 64
RWKV_A_RANK = 64
RWKV_G_RANK = 128
RWKV_GN_EPS = 64e-5
NORM_EPS = 1e-6
NEG_INF = -1e30

ATTN_W = ATTN_Q_HEADS * HEAD_DIM
KV_W = ATTN_KV_HEADS * HEAD_DIM
MLSTM_W = MLSTM_HEADS * HEAD_DIM
RWKV_W = RWKV_HEADS * HEAD_DIM
ATTN_COLS = ATTN_W + 2 * KV_W
MLSTM_MAIN = 4 * MLSTM_W
MLSTM_GATES = 2 * MLSTM_HEADS
RWKV_COLS = 3 * RWKV_W + RWKV_W_RANK + RWKV_A_RANK + RWKV_G_RANK

CHUNK = 64
NEUMANN_STEPS = 5
PRE_MIX_PART = 128
POST_MIX_PART = 256
RWKV_SLAB = 128
RWKV_FORWARDED = ("bonus", "v", "gate")
RWKV_STAGED = ("a_t", "b_t", "k_t", "r_t", "b_end", "k_end", "v", "bonus", "gate")
MXU_TILE = 256
FF_SPLITS = (0, 6 * MXU_TILE, D_FF)
SUBLANES = 8
V7X_VMEM_LIMIT = 56 * 1024 * 1024


def _params(semantics):
    return pltpu.CompilerParams(dimension_semantics=semantics, vmem_limit_bytes=V7X_VMEM_LIMIT)


def _resident(block_shape, index_map):
    return pl.BlockSpec(block_shape, index_map, pipeline_mode=pl.Buffered(1))


def _layer_vec(width, layer):
    return pl.BlockSpec((None, 1, width), lambda i: (layer, 0, 0))


def _rms(x, gain):
    return x * lax.rsqrt(jnp.mean(x * x, axis=-1, keepdims=True) + NORM_EPS) * gain


def _mm(a, b):
    return jnp.dot(a.astype(BF16), b.astype(BF16), preferred_element_type=F32)


def _mm_nt(a, b):
    return lax.dot_general(a.astype(BF16), b.astype(BF16), (((1,), (1,)), ((), ())),
                           preferred_element_type=F32)


def _mm_tn(a, b):
    return lax.dot_general(a.astype(BF16), b.astype(BF16), (((0,), (0,)), ((), ())),
                           preferred_element_type=F32)


def _bf16_terms(x, terms):
    out, rest = [], x
    for n in range(terms):
        part = rest.astype(BF16)
        out.append(part)
        if n + 1 < terms:
            rest = rest - part.astype(F32)
    return out


def _cumsum_rows(tri, x):
    return sum(jnp.dot(tri, term, preferred_element_type=F32) for term in _bf16_terms(x, 3))


def _split_dot(x, ones, terms=3):
    return sum(jnp.dot(term, ones, preferred_element_type=F32) for term in _bf16_terms(x, terms))


def _head_ones(width):
    row, col = _tri_masks(width)
    shift = HEAD_DIM.bit_length() - 1
    return ((row >> shift) == (col >> shift)).astype(BF16)


def _gate_spread():
    lanes = MLSTM_GATES * HEAD_DIM
    row = lax.broadcasted_iota(jnp.int32, (MLSTM_GATES, lanes), 0)
    col = lax.broadcasted_iota(jnp.int32, (MLSTM_GATES, lanes), 1)
    return (row == (col >> (HEAD_DIM.bit_length() - 1))).astype(BF16)


def _interleave(*gens):
    results = [None] * len(gens)
    live = dict(enumerate(gens))
    while live:
        for n in list(live):
            try:
                next(live[n])
            except StopIteration as stop:
                results[n] = stop.value
                del live[n]
    return results


def _tri_masks(n):
    row = lax.broadcasted_iota(jnp.int32, (n, n), 0)
    col = lax.broadcasted_iota(jnp.int32, (n, n), 1)
    return row, col


def _chunk_tri(n):
    row, col = _tri_masks(n)
    shift = CHUNK.bit_length() - 1
    return (((row >> shift) == (col >> shift)) & (col <= row)).astype(BF16)


def _macaron_half_step(x, gpre_ref, gpost_ref, win_ref, wout_ref):
    xn = _rms(x, gpre_ref[...]).astype(BF16)
    yield
    splits = list(zip(FF_SPLITS[:-1], FF_SPLITS[1:]))
    gate_up = []
    acc = None
    for n in range(len(splits) + 1):
        if n < len(splits):
            lo, hi = splits[n]
            gate_up.append((jnp.dot(xn, win_ref[:, lo:hi], preferred_element_type=F32),
                            jnp.dot(xn, win_ref[:, D_FF + lo:D_FF + hi], preferred_element_type=F32)))
        if n > 0:
            lo, hi = splits[n - 1]
            gate, up = gate_up[n - 1]
            act = (gate * jax.nn.sigmoid(gate) * up).astype(BF16)
            yield
            part = jnp.dot(act, wout_ref[lo:hi, :], preferred_element_type=F32)
            acc = part if acc is None else acc + part
        yield
    return x + 0.5 * _rms(acc, gpost_ref[...])


def _staggered_row_parts(rows, size, tile):
    if rows < 2 * size:
        return _interleave(tile(0, rows))
    parts = rows // size

    def delayed(gen, phases):
        for _ in range(phases):
            yield
        yield from gen

    return _interleave(*[delayed(tile(n * size, size), n) for n in range(parts)])


def _pre_mix_body(x_ref, gpre_ref, gpost_ref, win_ref, wout_ref, gmix_ref, wproj_ref, conv_ref, mu_ref,
                  x_out_ref, om_ref, or_ref, oa_ref, og_ref, qk_pad_ref, u_pad_ref):
    @pl.when(pl.program_id(0) == 0)
    def _():
        qk_pad_ref[0:SUBLANES, :] = jnp.zeros((SUBLANES, 2 * MLSTM_W), F32)
        u_pad_ref[0:SUBLANES, :] = jnp.zeros((SUBLANES, RWKV_COLS), F32)

    def tile(r0, n):
        rows = slice(r0, r0 + n)
        x = yield from _macaron_half_step(x_ref[rows, :], gpre_ref, gpost_ref, win_ref, wout_ref)
        x_out_ref[rows, :] = x
        h = _rms(x, gmix_ref[...]).astype(BF16)
        yield
        starts = [sum(PROJ_WIDTHS[:g]) for g in range(len(PROJ_WIDTHS))]
        proj = [jnp.dot(h, wproj_ref[:, lo:lo + w], preferred_element_type=F32) for lo, w in zip(starts, PROJ_WIDTHS)]
        p_mlstm, p_rwkv, p_attn, p_gates = proj
        oa_ref[rows, :] = p_attn
        og_ref[rows, :] = p_gates
        yield
        qk_pad_ref[SUBLANES:SUBLANES + n, :] = p_mlstm[:, 0:2 * MLSTM_W]
        conv = None
        for tap in range(MLSTM_CONV):
            shifted = qk_pad_ref[pl.ds(SUBLANES - (MLSTM_CONV - 1) + tap, n), :]
            term = shifted * conv_ref[tap:tap + 1, :]
            conv = term if conv is None else conv + term
        qk_pad_ref[0:SUBLANES, :] = qk_pad_ref[n:n + SUBLANES, :]
        qk = conv * jax.nn.sigmoid(conv)
        om_ref[rows, 0:MLSTM_W] = qk[:, 0:MLSTM_W] * (HEAD_DIM ** -0.5)
        om_ref[rows, MLSTM_W:2 * MLSTM_W] = qk[:, MLSTM_W:2 * MLSTM_W]
        om_ref[rows, 2 * MLSTM_W:] = p_mlstm[:, 2 * MLSTM_W:]
        yield
        u_pad_ref[SUBLANES:SUBLANES + n, :] = p_rwkv
        prev = u_pad_ref[pl.ds(SUBLANES - 1, n), :]
        u_pad_ref[0:SUBLANES, :] = u_pad_ref[n:n + SUBLANES, :]
        or_ref[rows, :] = p_rwkv + (prev - p_rwkv) * mu_ref[...]

    _staggered_row_parts(x_ref.shape[0], PRE_MIX_PART, tile)


PROJ_WIDTHS = (MLSTM_MAIN, RWKV_COLS, ATTN_COLS, MLSTM_GATES)


def _pre_mix(x, gpre, gpost, w_ffn_in, w_ffn_out, gmix, w_proj, conv_w, rwkv_mu, layer, tm):
    s = x.shape[0]
    row = lambda i: (i, 0)
    lay3 = lambda i: (layer, 0, 0)
    widths = PROJ_WIDTHS
    part = tm if tm < 2 * PRE_MIX_PART else PRE_MIX_PART
    return pl.pallas_call(
        _pre_mix_body,
        out_shape=[jax.ShapeDtypeStruct((s, D_MODEL), F32)] + [jax.ShapeDtypeStruct((s, w), F32) for w in widths],
        grid=(s // tm,),
        in_specs=[
            pl.BlockSpec((tm, D_MODEL), row),
            _layer_vec(D_MODEL, layer),
            _layer_vec(D_MODEL, layer),
            _resident((None, D_MODEL, 2 * D_FF), lay3),
            _resident((None, D_FF, D_MODEL), lay3),
            _layer_vec(D_MODEL, layer),
            _resident((None, D_MODEL, sum(widths)), lay3),
            pl.BlockSpec((None, MLSTM_CONV, 2 * MLSTM_W), lay3),
            _layer_vec(RWKV_COLS, layer),
        ],
        out_specs=[pl.BlockSpec((tm, D_MODEL), row)] + [pl.BlockSpec((tm, w), row) for w in widths],
        scratch_shapes=[
            pltpu.VMEM((part + SUBLANES, 2 * MLSTM_W), F32),
            pltpu.VMEM((part + SUBLANES, RWKV_COLS), F32),
        ],
        compiler_params=_params(("arbitrary",)),
        name="pre_mix",
    )(x, gpre, gpost, w_ffn_in, w_ffn_out, gmix, w_proj, conv_w, rwkv_mu)


def _post_mix_body(ya_ref, ym_ref, yr_ref, x_ref, p_ref, wo_ref, gmix_ref, gpre_ref, gpost_ref, win_ref, wout_ref,
                   gple_pre_ref, gple_post_ref, wgate_ref, wemb_ref, o_ref):
    def tile(r0, n):
        rows = slice(r0, r0 + n)
        mix = jnp.dot(ya_ref[rows, :], wo_ref[0:ATTN_W, :], preferred_element_type=F32)
        mix += jnp.dot(ym_ref[rows, :], wo_ref[ATTN_W:ATTN_W + MLSTM_W, :], preferred_element_type=F32)
        mix += jnp.dot(yr_ref[rows, :], wo_ref[ATTN_W + MLSTM_W:, :], preferred_element_type=F32)
        emb = jnp.dot(p_ref[rows, :].astype(BF16), wemb_ref[...], preferred_element_type=F32)
        yield
        x = x_ref[rows, :] + _rms(mix, gmix_ref[...])
        x = yield from _macaron_half_step(x, gpre_ref, gpost_ref, win_ref, wout_ref)
        h = _rms(x, gple_pre_ref[...]).astype(BF16)
        yield
        gate = jax.nn.sigmoid(jnp.dot(h, wgate_ref[...], preferred_element_type=F32))
        yield
        o_ref[rows, :] = x + _rms(gate * emb, gple_post_ref[...])

    _staggered_row_parts(x_ref.shape[0], POST_MIX_PART, tile)


def _post_mix(ya, ym, yr, x, p, w_out, gmix, gpre, gpost, w_ffn_in, w_ffn_out, gple_pre, gple_post,
              w_gate, w_emb, layer, tm):
    s = x.shape[0]
    row = lambda i: (i, 0)
    lay3 = lambda i: (layer, 0, 0)
    vec = _layer_vec(D_MODEL, layer)
    return pl.pallas_call(
        _post_mix_body,
        out_shape=jax.ShapeDtypeStruct((s, D_MODEL), F32),
        grid=(s // tm,),
        in_specs=[
            pl.BlockSpec((tm, ATTN_W), row),
            pl.BlockSpec((tm, MLSTM_W), row),
            pl.BlockSpec((tm, RWKV_W), row),
            pl.BlockSpec((tm, D_MODEL), row),
            pl.BlockSpec((None, tm, D_PLE), lambda i: (layer, i, 0)),
            _resident((None, D_MODEL, D_MODEL), lay3),
            vec, vec, vec,
            _resident((None, D_MODEL, 2 * D_FF), lay3),
            _resident((None, D_FF, D_MODEL), lay3),
            vec, vec,
            _resident((None, D_MODEL, D_MODEL), lay3),
            _resident((None, D_PLE, D_MODEL), lay3),
        ],
        out_specs=pl.BlockSpec((tm, D_MODEL), row),
        compiler_params=_params(("parallel",)),
        name="post_mix",
    )(ya, ym, yr, x, p, w_out, gmix, gpre, gpost, w_ffn_in, w_ffn_out, gple_pre, gple_post, w_gate, w_emb)


def _rope_body(pos_ref, invf_ref, cos_ref, sin_ref):
    ang = pos_ref[...].astype(F32) * invf_ref[...]
    sin = jnp.sin(ang)
    dim = lax.broadcasted_iota(jnp.int32, ang.shape, 1) & (HEAD_DIM - 1)
    cos_ref[...] = jnp.cos(ang)
    sin_ref[...] = jnp.where(dim < ROPE_HALF, -sin, jnp.where(dim < ROPE_DIM, sin, 0.0))


def _rope_tables(positions, tm):
    s = positions.shape[0]
    lane = jnp.arange(2 * HEAD_DIM) % HEAD_DIM
    freq = ROPE_THETA ** (-jnp.arange(0, ROPE_DIM, 2, dtype=F32) / ROPE_DIM)
    invf = jnp.where(lane < ROPE_DIM, freq[lane % ROPE_HALF], 0.0).astype(F32)[None, :]
    row = lambda i: (i, 0)
    out = jax.ShapeDtypeStruct((s, 2 * HEAD_DIM), F32)
    return pl.pallas_call(
        _rope_body,
        out_shape=[out, out],
        grid=(s // tm,),
        in_specs=[pl.BlockSpec((tm, 1), row), pl.BlockSpec((1, 2 * HEAD_DIM), lambda i: (0, 0))],
        out_specs=[pl.BlockSpec((tm, 2 * HEAD_DIM), row)] * 2,
        compiler_params=_params(("parallel",)),
        name="rope_tables",
    )(positions, invf)


ATTN_BLOCKS = 8
GROUP_W = ATTN_GROUP * HEAD_DIM
KEYS = 2 * WINDOW


def _rope_partner():
    row, col = _tri_masks(2 * HEAD_DIM)
    dim = col & (HEAD_DIM - 1)
    first = (dim < ROPE_HALF) & (row == col + ROPE_HALF)
    second = (dim >= ROPE_HALF) & (dim < ROPE_DIM) & (row == col - ROPE_HALF)
    return (first | second).astype(BF16)


def _attn_body(sinks_ref, p_ref, cos_ref, sin_ref, o_ref, ktprev_ref, vprev_ref, *, nblk):
    step = pl.program_id(0)

    @pl.when(step == 0)
    def _():
        ktprev_ref[...] = jnp.zeros_like(ktprev_ref)
        vprev_ref[...] = jnp.zeros_like(vprev_ref)

    cos, sin = cos_ref[...], sin_ref[...]
    partner = _rope_partner()
    period = 2 * HEAD_DIM

    def rope(x):
        slabs = [x[:, c:c + period] for c in range(0, x.shape[1], period)]
        return jnp.concatenate(
            [s * cos + jnp.dot(s.astype(BF16), partner, preferred_element_type=F32) * sin for s in slabs], axis=1)

    q = (rope(p_ref[:, 0:ATTN_W]) * (HEAD_DIM ** -0.5)).astype(BF16)
    k_cur = rope(p_ref[:, ATTN_W:ATTN_W + KV_W])
    v_f32 = p_ref[:, ATTN_W + KV_W:ATTN_COLS]
    kt = jnp.concatenate([ktprev_ref[...], k_cur.T.astype(BF16)], axis=1)
    vv = jnp.concatenate([vprev_ref[0], v_f32.astype(BF16)], axis=0)
    vs = jnp.concatenate([vprev_ref[1], pltpu.roll(v_f32, HEAD_DIM, 1).astype(BF16)], axis=0)
    ktprev_ref[...] = kt[:, nblk * WINDOW:]
    vprev_ref[0] = vv[nblk * WINDOW:, :]
    vprev_ref[1] = vs[nblk * WINDOW:, :]

    t = lax.broadcasted_iota(jnp.int32, (WINDOW, ATTN_GROUP * KEYS), 0)
    lane = lax.broadcasted_iota(jnp.int32, (WINDOW, ATTN_GROUP * KEYS), 1)
    c = lane & (KEYS - 1)
    head = lane >> (KEYS.bit_length() - 1)
    cur_ok = (c >= WINDOW) & (c - WINDOW <= t)
    prev_ok = (c < WINDOW) & (c > t)

    def bias(g, has_prev):
        b = jnp.where(cur_ok | (prev_ok & has_prev), 0.0, NEG_INF)
        sink = jnp.zeros_like(b)
        for n in range(ATTN_GROUP):
            sink = jnp.where(head == n, sinks_ref[g * ATTN_GROUP + n], sink)
        return jnp.where(c == 0, sink, b)

    bias_first = [bias(g, step > 0) for g in range(ATTN_KV_HEADS)]
    bias_rest = [bias(g, True) for g in range(ATTN_KV_HEADS)]

    key_lane = lax.broadcasted_iota(jnp.int32, (HEAD_DIM, KEYS), 1)
    zero_kt = jnp.zeros((HEAD_DIM, KEYS), BF16)
    vrow = lax.broadcasted_iota(jnp.int32, (KEYS, KV_W), 0)
    v_low = lax.broadcasted_iota(jnp.int32, (KEYS, KV_W), 1) < HEAD_DIM
    o_low = lax.broadcasted_iota(jnp.int32, (WINDOW, KV_W), 1) < HEAD_DIM
    one = jnp.ones((KEYS, KV_W), BF16)

    def blocks():
        units = [(b, g) for b in range(nblk) for g in range(ATTN_KV_HEADS)]
        kbd, v_even, v_odd = [], [], []
        for b, g in units:
            ktg = kt[g * HEAD_DIM:(g + 1) * HEAD_DIM, b * WINDOW:b * WINDOW + KEYS]
            ktg = jnp.where(key_lane == 0, jnp.zeros_like(ktg), ktg)
            kbd.append(jnp.concatenate(
                [jnp.concatenate([ktg if m == n else zero_kt for m in range(ATTN_GROUP)], axis=1)
                 for n in range(ATTN_GROUP)], axis=0))
            window = slice(b * WINDOW, b * WINDOW + KEYS)
            in_low, in_high = (vv, vs) if g == 0 else (vs, vv)
            v_even.append(jnp.where(v_low, jnp.where(vrow == 0, jnp.zeros_like(one), in_low[window, :]), one))
            v_odd.append(jnp.where(v_low, one, jnp.where(vrow == 0, jnp.zeros_like(one), in_high[window, :])))
        yield
        scores = [jnp.dot(q[b * WINDOW:(b + 1) * WINDOW, g * GROUP_W:(g + 1) * GROUP_W], kb,
                          preferred_element_type=F32) + (bias_first[g] if b == 0 else bias_rest[g])
                  for (b, g), kb in zip(units, kbd)]
        yield
        probs = []
        for s in scores:
            parts = []
            for n in range(ATTN_GROUP):
                sn = s[:, n * KEYS:(n + 1) * KEYS]
                parts.append(jnp.exp(sn - jnp.max(sn, axis=-1, keepdims=True)).astype(BF16))
            probs.append(parts)
        yield
        outs = [[jnp.dot(pn, even if n % 2 == 0 else odd, preferred_element_type=F32) for n, pn in enumerate(parts)]
                for parts, even, odd in zip(probs, v_even, v_odd)]
        yield
        for (b, g), heads in zip(units, outs):
            normed = [od * pltpu.roll(1.0 / od, HEAD_DIM, 1) for od in heads]
            pairs = [jnp.where(o_low, normed[n], normed[n + 1]) for n in range(0, ATTN_GROUP, 2)]
            o_ref[b * WINDOW:(b + 1) * WINDOW, g * GROUP_W:(g + 1) * GROUP_W] = (
                jnp.concatenate(pairs, axis=1).astype(o_ref.dtype))

    _interleave(blocks())


def _attention(sinks, attn_p, cos, sin, layer):
    s = attn_p.shape[0]
    nblk = min(ATTN_BLOCKS, s // WINDOW)
    tq = nblk * WINDOW
    row = lambda i: (i, 0)
    return pl.pallas_call(
        functools.partial(_attn_body, nblk=nblk),
        out_shape=jax.ShapeDtypeStruct((s, ATTN_W), BF16),
        grid=(s // tq,),
        in_specs=[
            pl.BlockSpec(memory_space=pltpu.SMEM),
            pl.BlockSpec((tq, ATTN_COLS), row),
            pl.BlockSpec((tq, 2 * HEAD_DIM), row),
            pl.BlockSpec((tq, 2 * HEAD_DIM), row),
        ],
        out_specs=pl.BlockSpec((tq, ATTN_W), row),
        scratch_shapes=[pltpu.VMEM((KV_W, WINDOW), BF16), pltpu.VMEM((2, WINDOW, KV_W), BF16)],
        compiler_params=_params(("arbitrary",)),
        name="swa_attention",
    )(sinks[layer], attn_p, cos, sin)


def _mlstm_body(p_ref, gates_ref, bias_ref, norm_ref, o_ref, c_ref, n_ref, *, tb):
    step = pl.program_id(0)

    @pl.when(step == 0)
    def _():
        c_ref[...] = jnp.zeros_like(c_ref)
        n_ref[...] = jnp.zeros_like(n_ref)

    q_all = p_ref[:, 0:MLSTM_W]
    k_all = p_ref[:, MLSTM_W:2 * MLSTM_W]

    pre = GATE_CAP * jnp.tanh((gates_ref[...] + bias_ref[...]) / GATE_CAP)
    logsig = jnp.minimum(pre, 0.0) - jnp.log(1.0 + jnp.exp(-jnp.abs(pre)))

    g_cum = _cumsum_rows(_chunk_tri(tb), logsig)

    gate_col = lax.broadcasted_iota(jnp.int32, (tb, MLSTM_GATES), 1)
    gate_lanes = _split_dot(jnp.where(gate_col < MLSTM_HEADS, pre, g_cum), _gate_spread())
    i_all, g_all = gate_lanes[:, 0:MLSTM_W], gate_lanes[:, MLSTM_W:2 * MLSTM_W]
    head_ones = _head_ones(MLSTM_W)

    chunk_starts = range(0, tb, CHUNK)
    chunk_of = lambda x: [x[c0:c0 + CHUNK, :] for c0 in chunk_starts]
    hshift = HEAD_DIM.bit_length() - 1
    lane = lax.broadcasted_iota(jnp.int32, (CHUNK, MLSTM_W), 1)
    time = lax.broadcasted_iota(jnp.int32, (CHUNK, MLSTM_W), 0)
    key_of_lane = lane & (HEAD_DIM - 1)
    causal = key_of_lane <= time
    eye = key_of_lane == time
    brow, bcol = _tri_masks(MLSTM_W)
    same_head = (brow >> hshift) == (bcol >> hshift)
    head_block = same_head.astype(BF16)
    stack = lambda x: jnp.where(same_head, jnp.concatenate([x] * MLSTM_HEADS, axis=0), 0.0)

    def to_row(lane_bcast):
        return jnp.sum(jnp.where(eye, lane_bcast, 0.0), axis=0, keepdims=True)

    q, k, v = chunk_of(q_all), chunk_of(k_all), chunk_of(p_ref[:, 2 * MLSTM_W:3 * MLSTM_W])
    g_chunk, i_chunk = chunk_of(g_all), chunk_of(i_all)
    g_tot = [g[CHUNK - 1:CHUNK, :] for g in g_chunk]
    w_in = [jnp.exp(t - g + i) for t, g, i in zip(g_tot, g_chunk, i_chunk)]
    e_tot = [jnp.exp(t) for t in g_tot]
    e_g = [jnp.exp(g) for g in g_chunk]
    n_loc = [jnp.sum(w * x, axis=0, keepdims=True) for w, x in zip(w_in, k)]
    decay = [jnp.exp(jnp.where(causal, g - to_row(g) + to_row(i), NEG_INF)) for g, i in zip(g_chunk, i_chunk)]
    s_mat = [_mm_nt(a, stack(b)) * d for a, b, d in zip(q, k, decay)]
    num_den = [_mm(s, jnp.concatenate([stack(x).astype(BF16), head_block], axis=1))
               for s, x in zip(s_mat, v)]
    c_loc = [jnp.where(same_head, _mm_tn(w * x, b), 0.0) for w, x, b in zip(w_in, v, k)]

    c_state, n_state = c_ref[...], n_ref[0:1, :]
    c_in, n_in = [], []
    for n in range(len(chunk_starts)):
        c_in.append(c_state)
        n_in.append(n_state)
        c_state = e_tot[n] * c_state + c_loc[n]
        n_state = e_tot[n] * n_state + n_loc[n]
    c_ref[...] = c_state
    n_ref[0:1, :] = n_state

    inter = [_mm_nt(a, jnp.concatenate([c, stack(jnp.broadcast_to(x, (CHUNK, MLSTM_W)))], axis=0))
             for a, c, x in zip(q, c_in, n_in)]
    for c0, nd, e, x in zip(chunk_starts, num_den, e_g, inter):
        hid = (nd[:, :MLSTM_W] + e * x[:, :MLSTM_W]) / jnp.maximum(jnp.abs(nd[:, MLSTM_W:] + e * x[:, MLSTM_W:]), 1.0)
        mean_sq = _split_dot(hid * hid, head_ones, terms=2) * (1.0 / HEAD_DIM)
        hid = hid * lax.rsqrt(mean_sq + NORM_EPS) * norm_ref[...]
        o_gate = p_ref[c0:c0 + CHUNK, 3 * MLSTM_W:4 * MLSTM_W]
        o_ref[c0:c0 + CHUNK, :] = (jax.nn.sigmoid(o_gate) * hid).astype(o_ref.dtype)


def _mlstm(mlstm_p, gates, bias, norm, layer, tb):
    s = mlstm_p.shape[0]
    row = lambda i: (i, 0)
    return pl.pallas_call(
        functools.partial(_mlstm_body, tb=tb),
        out_shape=jax.ShapeDtypeStruct((s, MLSTM_W), BF16),
        grid=(s // tb,),
        in_specs=[
            pl.BlockSpec((tb, MLSTM_MAIN), row),
            pl.BlockSpec((tb, MLSTM_GATES), row),
            _layer_vec(MLSTM_GATES, layer),
            _layer_vec(MLSTM_W, layer),
        ],
        out_specs=pl.BlockSpec((tb, MLSTM_W), row),
        scratch_shapes=[
            pltpu.VMEM((MLSTM_W, MLSTM_W), F32),
            pltpu.VMEM((SUBLANES, MLSTM_W), F32),
        ],
        compiler_params=_params(("arbitrary",)),
        name="mlstm",
    )(mlstm_p, gates, bias, norm)


def _rwkv_chunk_terms(a_t, b_t, k_t, r_t, b_end, k_end, v):
    row, col = _tri_masks(CHUNK)
    incl = col <= row
    strict = col < row
    eye = (col == row).astype(F32)
    half = HEAD_DIM

    quad = [_mm_nt(jnp.concatenate([a, r], axis=0), jnp.concatenate([b, k], axis=0))
            for a, r, b, k in zip(a_t, r_t, b_t, k_t)]
    yield
    n_mat = [jnp.where(strict, x[:CHUNK, :CHUNK], 0.0) for x in quad]
    a_ak = [jnp.where(strict, x[:CHUNK, CHUNK:], 0.0) for x in quad]
    c_rb = [jnp.where(incl, x[CHUNK:, :CHUNK], 0.0) for x in quad]
    c_rk = [jnp.where(incl, x[CHUNK:, CHUNK:], 0.0) for x in quad]

    inv = [eye + n for n in n_mat]
    power = n_mat
    for _ in range(NEUMANN_STEPS):
        power = [_mm(x, x) for x in power]
        yield
        inv = [m + _mm(m, x) for m, x in zip(inv, power)]
        yield

    z = [_mm(x, y) for x, y in zip(a_ak, v)]
    yield
    w12 = [_mm(m, jnp.concatenate([a, y], axis=1)) for m, a, y in zip(inv, a_t, z)]
    yield
    cw = [_mm(c, w) for c, w in zip(c_rb, w12)]
    yield
    ckv = [_mm(c, y) for c, y in zip(c_rk, v)]
    yield
    gh = [_mm_tn(w, b) for w, b in zip(w12, b_end)]
    yield
    vk = [_mm_tn(y, k) for y, k in zip(v, k_end)]
    yield
    q_eff = [r + x[:, :half] for r, x in zip(r_t, cw)]
    y_loc = [x[:, half:] + y for x, y in zip(cw, ckv)]
    g_mat = [x[:half, :] for x in gh]
    h_mat = [x[half:, :] + y for x, y in zip(gh, vk)]
    return q_eff, y_loc, g_mat, h_mat


def _rwkv_body(p_ref, w0_ref, wup_ref, a0_ref, aup_ref, gup_ref, kk_ref, ka_ref, rk_ref,
               lnw_ref, lnb_ref, o_ref, s_ref, stage_ref, end_ref, terms_ref, fwd_ref, fwd_end_ref, *, tb):
    step = pl.program_id(0)

    @pl.when(step == 0)
    def _():
        s_ref[...] = jnp.zeros_like(s_ref)
        stage_ref[...] = jnp.zeros_like(stage_ref)
        end_ref[...] = jnp.zeros_like(end_ref)
        terms_ref[...] = jnp.zeros_like(terms_ref)
        fwd_ref[...] = jnp.zeros_like(fwd_ref)
        fwd_end_ref[...] = jnp.zeros_like(fwd_end_ref)

    refs = (p_ref, w0_ref, wup_ref, a0_ref, aup_ref, gup_ref, kk_ref, ka_ref, rk_ref,
            lnw_ref, lnb_ref, o_ref, s_ref, stage_ref, end_ref, terms_ref, fwd_ref, fwd_end_ref)
    for cur in (0, 1):
        pl.when(lax.rem(step, 2) == cur)(functools.partial(_rwkv_step, *refs, tb=tb, cur=cur, prv=1 - cur))


def _rwkv_step(p_ref, w0_ref, wup_ref, a0_ref, aup_ref, gup_ref, kk_ref, ka_ref, rk_ref,
               lnw_ref, lnb_ref, o_ref, s_ref, stage_ref, end_ref, terms_ref, fwd_ref, fwd_end_ref,
               *, tb, cur, prv):
    head_ones = _head_ones(RWKV_W)
    head_sum = lambda x: _split_dot(x, head_ones, terms=2)
    head_lanes = lambda h: slice(h * HEAD_DIM, (h + 1) * HEAD_DIM)
    chunk_starts = range(0, tb, CHUNK)
    units = [(n, c0, h) for n, c0 in enumerate(chunk_starts) for h in range(RWKV_HEADS)]

    unit_rows = lambda u: slice(u * CHUNK, (u + 1) * CHUNK)

    def terms_previous():
        staged = {name: stage_ref[prv, n] for n, name in enumerate(RWKV_STAGED)}
        per_unit = {name: [staged[name][c0:c0 + CHUNK, head_lanes(h)] for _, c0, h in units]
                    for name in RWKV_STAGED[:7]}
        terms = yield from _rwkv_chunk_terms(**per_unit)
        for kind, per_unit_values in enumerate(terms):
            for u, value in enumerate(per_unit_values):
                terms_ref[prv, kind, unit_rows(u), :] = value
            yield
        for n, name in enumerate(RWKV_FORWARDED):
            fwd_ref[prv, n] = staged[name]
        fwd_end_ref[prv] = end_ref[prv]

    def finish_older():
        ends = fwd_end_ref[cur]
        state = [s_ref[h] for h in range(RWKV_HEADS)]
        ys = []
        for u, (n, c0, h) in enumerate(units):
            q_eff, y_loc, g_mat, h_mat = (terms_ref[cur, kind, unit_rows(u), :] for kind in range(4))
            ys.append(_mm_nt(q_eff, state[h]) + y_loc)
            state[h] = state[h] * ends[n:n + 1, head_lanes(h)] + _mm(state[h], g_mat) + h_mat
            if h == RWKV_HEADS - 1:
                yield
        for h in range(RWKV_HEADS):
            s_ref[h] = state[h]
        y_all = jnp.concatenate([jnp.concatenate(ys[n:n + RWKV_HEADS], axis=1)
                                 for n in range(0, len(ys), RWKV_HEADS)], axis=0)
        centred = y_all - head_sum(y_all) * (1.0 / HEAD_DIM)
        yield
        var = head_sum(centred * centred) * (1.0 / HEAD_DIM)
        normed = centred * lax.rsqrt(var + RWKV_GN_EPS) * lnw_ref[...] + lnb_ref[...]
        bonus, v_all, gate = (fwd_ref[cur, n] for n in range(len(RWKV_FORWARDED)))
        o_ref[...] = ((normed + bonus * v_all) * gate).astype(o_ref.dtype)

    def stage_current():
        for r0 in range(0, tb, RWKV_SLAB):
            rows = slice(r0, r0 + RWKV_SLAB)

            def stage(name, value):
                stage_ref[cur, RWKV_STAGED.index(name), rows, :] = value

            u = p_ref[rows, :]
            yield
            r_all = u[:, 0:RWKV_W]
            k_raw = u[:, RWKV_W:2 * RWKV_W]
            v_all = u[:, 2 * RWKV_W:3 * RWKV_W]
            x_w = u[:, 3 * RWKV_W:3 * RWKV_W + RWKV_W_RANK]
            x_a = u[:, 3 * RWKV_W + RWKV_W_RANK:3 * RWKV_W + RWKV_W_RANK + RWKV_A_RANK]
            x_g = u[:, 3 * RWKV_W + RWKV_W_RANK + RWKV_A_RANK:RWKV_COLS]
            z = w0_ref[...] + _mm(jnp.tanh(x_w), wup_ref[...])
            ld_all = -jnp.exp(-0.5) * jax.nn.sigmoid(z)
            a_all = jax.nn.sigmoid(a0_ref[...] + _mm(x_a, aup_ref[...]))
            yield
            kk_all = k_raw * kk_ref[...]
            k_all = k_raw * (1.0 + (a_all - 1.0) * ka_ref[...])
            kk_all = kk_all * lax.rsqrt(jnp.maximum(head_sum(kk_all * kk_all), 1e-24))
            a_vec = -kk_all
            b_vec = kk_all * a_all
            yield
            lp_all = _cumsum_rows(_chunk_tri(RWKV_SLAB), ld_all)
            stage("v", v_all)
            stage("bonus", head_sum(r_all * k_all * rk_ref[...]))
            stage("gate", _mm(jax.nn.sigmoid(x_g), gup_ref[...]))
            yield
            starts = range(0, RWKV_SLAB, CHUNK)
            lp_last = jnp.concatenate(
                [jnp.broadcast_to(lp_all[c0 + CHUNK - 1:c0 + CHUNK, :], (CHUNK, RWKV_W)) for c0 in starts], axis=0)
            grow = jnp.exp(-lp_all)
            to_end = jnp.exp(lp_last - lp_all)
            stage("a_t", a_vec * jnp.exp(lp_all - ld_all))
            stage("b_t", b_vec * grow)
            stage("k_t", k_all * grow)
            yield
            stage("r_t", r_all * jnp.exp(lp_all))
            stage("b_end", b_vec * to_end)
            stage("k_end", k_all * to_end)
            for c0 in starts:
                n = (r0 + c0) // CHUNK
                end_ref[cur, n:n + 1, :] = jnp.exp(lp_all[c0 + CHUNK - 1:c0 + CHUNK, :])
            yield

    _interleave(terms_previous(), finish_older(), stage_current())


def _rwkv(rwkv_p, w0, w_up, a0, a_up, g_up, k_k, k_a, r_k, ln_w, ln_b, layer, tb):
    s = rwkv_p.shape[0]
    blocks = s // tb
    lay3 = lambda i: (layer, 0, 0)
    vec = lambda width: _layer_vec(width, layer)
    return pl.pallas_call(
        functools.partial(_rwkv_body, tb=tb),
        out_shape=jax.ShapeDtypeStruct((s, RWKV_W), BF16),
        grid=(blocks + 2,),
        in_specs=[
            pl.BlockSpec((tb, RWKV_COLS), lambda i: (jnp.minimum(i, blocks - 1), 0)),
            vec(RWKV_W),
            pl.BlockSpec((None, RWKV_W_RANK, RWKV_W), lay3),
            vec(RWKV_W),
            pl.BlockSpec((None, RWKV_A_RANK, RWKV_W), lay3),
            pl.BlockSpec((None, RWKV_G_RANK, RWKV_W), lay3),
            vec(RWKV_W), vec(RWKV_W), vec(RWKV_W), vec(RWKV_W), vec(RWKV_W),
        ],
        out_specs=pl.BlockSpec((tb, RWKV_W), lambda i: (jnp.maximum(i - 2, 0), 0)),
        scratch_shapes=[
            pltpu.VMEM((RWKV_HEADS, HEAD_DIM, HEAD_DIM), F32),
            pltpu.VMEM((2, len(RWKV_STAGED), tb, RWKV_W), F32),
            pltpu.VMEM((2, tb // CHUNK, RWKV_W), F32),
            pltpu.VMEM((2, 4, (tb // CHUNK) * RWKV_HEADS * CHUNK, HEAD_DIM), F32),
            pltpu.VMEM((2, len(RWKV_FORWARDED), tb, RWKV_W), F32),
            pltpu.VMEM((2, tb // CHUNK, RWKV_W), F32),
        ],
        compiler_params=_params(("arbitrary",)),
        name="rwkv7",
    )(rwkv_p, w0, w_up, a0, a_up, g_up, k_k, k_a, r_k, ln_w, ln_b)


def _tiles(s):
    return min(512, s), min(512, s)


def kernel(x, p, positions, ln_ffn1_pre, ln_ffn1_post, w_ffn1_in, w_ffn1_out, ln_mix_pre, w_in, attn_sinks, mlstm_conv, mlstm_i_bias, mlstm_f_bias, mlstm_norm, rwkv_mu, rwkv_w0, rwkv_w_up, rwkv_a0, rwkv_a_up, rwkv_g_up, rwkv_k_k, rwkv_k_a, rwkv_r_k, rwkv_ln_w, rwkv_ln_b, w_out, ln_mix_post, ln_ffn2_pre, ln_ffn2_post, w_ffn2_in, w_ffn2_out, ln_ple_pre, w_ple_gate, w_ple_proj, ln_ple_post):
    batch, seq, _ = x.shape
    assert batch == 1 and seq % WINDOW == 0
    depth = w_in.shape[0]
    tm, tb = _tiles(seq)

    bf = lambda w: w.astype(BF16)
    w_ffn1_in, w_ffn1_out, w_ffn2_in, w_ffn2_out = map(bf, (w_ffn1_in, w_ffn1_out, w_ffn2_in, w_ffn2_out))
    w_out, w_ple_gate, w_ple_proj = map(bf, (w_out, w_ple_gate, w_ple_proj))
    rwkv_w_up, rwkv_a_up, rwkv_g_up = map(bf, (rwkv_w_up, rwkv_a_up, rwkv_g_up))
    m0 = ATTN_COLS
    g0 = m0 + MLSTM_MAIN
    r0 = g0 + MLSTM_GATES
    w_proj = bf(jnp.concatenate([w_in[:, :, m0:g0], w_in[:, :, r0:], w_in[:, :, :m0], w_in[:, :, g0:r0]], axis=-1))
    vec = lambda a: a.reshape(depth, 1, -1)
    gate_bias = vec(jnp.concatenate([mlstm_i_bias, mlstm_f_bias], axis=-1))
    (ln_ffn1_pre, ln_ffn1_post, ln_mix_pre, ln_mix_post, ln_ffn2_pre, ln_ffn2_post, ln_ple_pre,
     ln_ple_post, mlstm_norm, rwkv_mu, rwkv_w0, rwkv_a0, rwkv_k_k, rwkv_k_a, rwkv_r_k, rwkv_ln_w,
     rwkv_ln_b) = map(vec, (
         ln_ffn1_pre, ln_ffn1_post, ln_mix_pre, ln_mix_post, ln_ffn2_pre, ln_ffn2_post, ln_ple_pre,
         ln_ple_post, mlstm_norm, rwkv_mu, rwkv_w0, rwkv_a0, rwkv_k_k, rwkv_k_a, rwkv_r_k, rwkv_ln_w,
         rwkv_ln_b))

    cos, sin = _rope_tables(positions.reshape(seq, 1), tm)
    xs = x.reshape(seq, D_MODEL)
    for l in range(depth):
        xs, mlstm_p, rwkv_p, attn_p, gates = _pre_mix(
            xs, ln_ffn1_pre, ln_ffn1_post, w_ffn1_in, w_ffn1_out, ln_mix_pre, w_proj, mlstm_conv, rwkv_mu, l, tm)
        y_attn = _attention(attn_sinks, attn_p, cos, sin, l)
        y_mlstm = _mlstm(mlstm_p, gates, gate_bias, mlstm_norm, l, tb)
        y_rwkv = _rwkv(rwkv_p, rwkv_w0, rwkv_w_up, rwkv_a0, rwkv_a_up, rwkv_g_up,
                       rwkv_k_k, rwkv_k_a, rwkv_r_k, rwkv_ln_w, rwkv_ln_b, l, tb)
        xs = _post_mix(y_attn, y_mlstm, y_rwkv, xs, p.reshape(depth, seq, D_PLE), w_out, ln_mix_post,
                       ln_ffn2_pre, ln_ffn2_post, w_ffn2_in, w_ffn2_out, ln_ple_pre, ln_ple_post,
                       w_ple_gate, w_ple_proj, l, tm)
    return xs.reshape(batch, seq, D_MODEL)
```

```python
import functools

import jax
import jax.numpy as jnp
from jax import lax
from jax.experimental import pallas as pl
from jax.experimental.pallas import tpu as pltpu

F32 = jnp.float32
BF16 = jnp.bfloat16

D_MODEL = 1024
HEAD_DIM = 64
D_FF = 2816
D_PLE = 256
ATTN_Q_HEADS = 8
ATTN_KV_HEADS = 2
ATTN_GROUP = ATTN_Q_HEADS // ATTN_KV_HEADS
WINDOW = 128
ROPE_THETA = 500000.0
ROPE_DIM = HEAD_DIM // 4
ROPE_HALF = ROPE_DIM // 2
MLSTM_HEADS = 4
MLSTM_CONV = 4
GATE_CAP = 15.0
RWKV_HEADS = 4
RWKV_W_RANK = 64
RWKV_A_RANK = 64
RWKV_G_RANK = 128
RWKV_GN_EPS = 64e-5
NORM_EPS = 1e-6
NEG_INF = -1e30

ATTN_W = ATTN_Q_HEADS * HEAD_DIM
KV_W = ATTN_KV_HEADS * HEAD_DIM
MLSTM_W = MLSTM_HEADS * HEAD_DIM
RWKV_W = RWKV_HEADS * HEAD_DIM
ATTN_COLS = ATTN_W + 2 * KV_W
MLSTM_MAIN = 4 * MLSTM_W
MLSTM_GATES = 2 * MLSTM_HEADS
RWKV_COLS = 3 * RWKV_W + RWKV_W_RANK + RWKV_A_RANK + RWKV_G_RANK

CHUNK = 64
NEUMANN_STEPS = 5
PRE_MIX_PART = 128
POST_MIX_PART = 256
MLSTM_PART = 256
RWKV_SLAB = 128
RWKV_FORWARDED = ("bonus", "v", "gate")
RWKV_STAGED = ("a_t", "b_t", "k_t", "r_t", "b_end", "k_end", "v", "bonus", "gate")
MXU_TILE = 256
FF_SPLITS = (0, 6 * MXU_TILE, D_FF)
SUBLANES = 8
V7X_VMEM_LIMIT = 56 * 1024 * 1024


def _params(semantics):
    return pltpu.CompilerParams(dimension_semantics=semantics, vmem_limit_bytes=V7X_VMEM_LIMIT)


def _resident(block_shape, index_map):
    return pl.BlockSpec(block_shape, index_map, pipeline_mode=pl.Buffered(1))


def _layer_vec(width, layer):
    return pl.BlockSpec((None, 1, width), lambda i: (layer, 0, 0))


def _rms(x, gain):
    return x * lax.rsqrt(jnp.mean(x * x, axis=-1, keepdims=True) + NORM_EPS) * gain


def _mm(a, b):
    return jnp.dot(a.astype(BF16), b.astype(BF16), preferred_element_type=F32)


def _mm_nt(a, b):
    return lax.dot_general(a.astype(BF16), b.astype(BF16), (((1,), (1,)), ((), ())),
                           preferred_element_type=F32)


def _mm_tn(a, b):
    return lax.dot_general(a.astype(BF16), b.astype(BF16), (((0,), (0,)), ((), ())),
                           preferred_element_type=F32)


def _bf16_terms(x, terms):
    out, rest = [], x
    for n in range(terms):
        part = rest.astype(BF16)
        out.append(part)
        if n + 1 < terms:
            rest = rest - part.astype(F32)
    return out


def _cumsum_rows(tri, x):
    return sum(jnp.dot(tri, term, preferred_element_type=F32) for term in _bf16_terms(x, 3))


def _split_dot(x, ones, terms=3):
    return sum(jnp.dot(term, ones, preferred_element_type=F32) for term in _bf16_terms(x, terms))


def _head_ones(width):
    row, col = _tri_masks(width)
    shift = HEAD_DIM.bit_length() - 1
    return ((row >> shift) == (col >> shift)).astype(BF16)


def _gate_spread():
    lanes = MLSTM_GATES * HEAD_DIM
    row = lax.broadcasted_iota(jnp.int32, (MLSTM_GATES, lanes), 0)
    col = lax.broadcasted_iota(jnp.int32, (MLSTM_GATES, lanes), 1)
    return (row == (col >> (HEAD_DIM.bit_length() - 1))).astype(BF16)


def _interleave(*gens):
    results = [None] * len(gens)
    live = dict(enumerate(gens))
    while live:
        for n in list(live):
            try:
                next(live[n])
            except StopIteration as stop:
                results[n] = stop.value
                del live[n]
    return results


def _tri_masks(n):
    row = lax.broadcasted_iota(jnp.int32, (n, n), 0)
    col = lax.broadcasted_iota(jnp.int32, (n, n), 1)
    return row, col


def _chunk_tri(n):
    row, col = _tri_masks(n)
    shift = CHUNK.bit_length() - 1
    return (((row >> shift) == (col >> shift)) & (col <= row)).astype(BF16)


def _macaron_half_step(x, gpre_ref, gpost_ref, win_ref, wout_ref):
    xn = _rms(x, gpre_ref[...]).astype(BF16)
    yield
    splits = list(zip(FF_SPLITS[:-1], FF_SPLITS[1:]))
    gate_up = []
    acc = None
    for n in range(len(splits) + 1):
        if n < len(splits):
            lo, hi = splits[n]
            gate_up.append((jnp.dot(xn, win_ref[:, lo:hi], preferred_element_type=F32),
                            jnp.dot(xn, win_ref[:, D_FF + lo:D_FF + hi], preferred_element_type=F32)))
        if n > 0:
            lo, hi = splits[n - 1]
            gate, up = gate_up[n - 1]
            act = (gate * jax.nn.sigmoid(gate) * up).astype(BF16)
            yield
            part = jnp.dot(act, wout_ref[lo:hi, :], preferred_element_type=F32)
            acc = part if acc is None else acc + part
        yield
    return x + 0.5 * _rms(acc, gpost_ref[...])


def _staggered_row_parts(rows, size, tile):
    if rows < 2 * size:
        return _interleave(tile(0, rows))
    parts = rows // size

    def delayed(gen, phases):
        for _ in range(phases):
            yield
        yield from gen

    return _interleave(*[delayed(tile(n * size, size), n) for n in range(parts)])


def _pre_mix_body(x_ref, gpre_ref, gpost_ref, win_ref, wout_ref, gmix_ref, wproj_ref, conv_ref, mu_ref,
                  x_out_ref, om_ref, or_ref, oa_ref, og_ref, qk_pad_ref, u_pad_ref):
    @pl.when(pl.program_id(0) == 0)
    def _():
        qk_pad_ref[0:SUBLANES, :] = jnp.zeros((SUBLANES, 2 * MLSTM_W), F32)
        u_pad_ref[0:SUBLANES, :] = jnp.zeros((SUBLANES, RWKV_COLS), F32)

    def tile(r0, n):
        rows = slice(r0, r0 + n)
        x = yield from _macaron_half_step(x_ref[rows, :], gpre_ref, gpost_ref, win_ref, wout_ref)
        x_out_ref[rows, :] = x
        h = _rms(x, gmix_ref[...]).astype(BF16)
        yield
        starts = [sum(PROJ_WIDTHS[:g]) for g in range(len(PROJ_WIDTHS))]
        proj = [jnp.dot(h, wproj_ref[:, lo:lo + w], preferred_element_type=F32) for lo, w in zip(starts, PROJ_WIDTHS)]
        p_mlstm, p_rwkv, p_attn, p_gates = proj
        oa_ref[rows, :] = p_attn
        og_ref[rows, :] = p_gates
        yield
        qk_pad_ref[SUBLANES:SUBLANES + n, :] = p_mlstm[:, 0:2 * MLSTM_W]
        conv = None
        for tap in range(MLSTM_CONV):
            shifted = qk_pad_ref[pl.ds(SUBLANES - (MLSTM_CONV - 1) + tap, n), :]
            term = shifted * conv_ref[tap:tap + 1, :]
            conv = term if conv is None else conv + term
        qk_pad_ref[0:SUBLANES, :] = qk_pad_ref[n:n + SUBLANES, :]
        qk = conv * jax.nn.sigmoid(conv)
        om_ref[rows, 0:MLSTM_W] = qk[:, 0:MLSTM_W] * (HEAD_DIM ** -0.5)
        om_ref[rows, MLSTM_W:2 * MLSTM_W] = qk[:, MLSTM_W:2 * MLSTM_W]
        om_ref[rows, 2 * MLSTM_W:] = p_mlstm[:, 2 * MLSTM_W:]
        yield
        u_pad_ref[SUBLANES:SUBLANES + n, :] = p_rwkv
        prev = u_pad_ref[pl.ds(SUBLANES - 1, n), :]
        u_pad_ref[0:SUBLANES, :] = u_pad_ref[n:n + SUBLANES, :]
        or_ref[rows, :] = p_rwkv + (prev - p_rwkv) * mu_ref[...]

    _staggered_row_parts(x_ref.shape[0], PRE_MIX_PART, tile)


PROJ_WIDTHS = (MLSTM_MAIN, RWKV_COLS, ATTN_COLS, MLSTM_GATES)


def _pre_mix(x, gpre, gpost, w_ffn_in, w_ffn_out, gmix, w_proj, conv_w, rwkv_mu, layer, tm):
    s = x.shape[0]
    row = lambda i: (i, 0)
    lay3 = lambda i: (layer, 0, 0)
    widths = PROJ_WIDTHS
    part = tm if tm < 2 * PRE_MIX_PART else PRE_MIX_PART
    return pl.pallas_call(
        _pre_mix_body,
        out_shape=[jax.ShapeDtypeStruct((s, D_MODEL), F32)] + [jax.ShapeDtypeStruct((s, w), F32) for w in widths],
        grid=(s // tm,),
        in_specs=[
            pl.BlockSpec((tm, D_MODEL), row),
            _layer_vec(D_MODEL, layer),
            _layer_vec(D_MODEL, layer),
            _resident((None, D_MODEL, 2 * D_FF), lay3),
            _resident((None, D_FF, D_MODEL), lay3),
            _layer_vec(D_MODEL, layer),
            _resident((None, D_MODEL, sum(widths)), lay3),
            pl.BlockSpec((None, MLSTM_CONV, 2 * MLSTM_W), lay3),
            _layer_vec(RWKV_COLS, layer),
        ],
        out_specs=[pl.BlockSpec((tm, D_MODEL), row)] + [pl.BlockSpec((tm, w), row) for w in widths],
        scratch_shapes=[
            pltpu.VMEM((part + SUBLANES, 2 * MLSTM_W), F32),
            pltpu.VMEM((part + SUBLANES, RWKV_COLS), F32),
        ],
        compiler_params=_params(("arbitrary",)),
        name="pre_mix",
    )(x, gpre, gpost, w_ffn_in, w_ffn_out, gmix, w_proj, conv_w, rwkv_mu)


def _post_mix_body(ya_ref, ym_ref, yr_ref, x_ref, p_ref, wo_ref, gmix_ref, gpre_ref, gpost_ref, win_ref, wout_ref,
                   gple_pre_ref, gple_post_ref, wgate_ref, wemb_ref, o_ref):
    def tile(r0, n):
        rows = slice(r0, r0 + n)
        mix = jnp.dot(ya_ref[rows, :], wo_ref[0:ATTN_W, :], preferred_element_type=F32)
        mix += jnp.dot(ym_ref[rows, :], wo_ref[ATTN_W:ATTN_W + MLSTM_W, :], preferred_element_type=F32)
        mix += jnp.dot(yr_ref[rows, :], wo_ref[ATTN_W + MLSTM_W:, :], preferred_element_type=F32)
        emb = jnp.dot(p_ref[rows, :].astype(BF16), wemb_ref[...], preferred_element_type=F32)
        yield
        x = x_ref[rows, :] + _rms(mix, gmix_ref[...])
        x = yield from _macaron_half_step(x, gpre_ref, gpost_ref, win_ref, wout_ref)
        h = _rms(x, gple_pre_ref[...]).astype(BF16)
        yield
        gate = jax.nn.sigmoid(jnp.dot(h, wgate_ref[...], preferred_element_type=F32))
        yield
        o_ref[rows, :] = x + _rms(gate * emb, gple_post_ref[...])

    _staggered_row_parts(x_ref.shape[0], POST_MIX_PART, tile)


def _post_mix(ya, ym, yr, x, p, w_out, gmix, gpre, gpost, w_ffn_in, w_ffn_out, gple_pre, gple_post,
              w_gate, w_emb, layer, tm):
    s = x.shape[0]
    row = lambda i: (i, 0)
    lay3 = lambda i: (layer, 0, 0)
    vec = _layer_vec(D_MODEL, layer)
    return pl.pallas_call(
        _post_mix_body,
        out_shape=jax.ShapeDtypeStruct((s, D_MODEL), F32),
        grid=(s // tm,),
        in_specs=[
            pl.BlockSpec((tm, ATTN_W), row),
            pl.BlockSpec((tm, MLSTM_W), row),
            pl.BlockSpec((tm, RWKV_W), row),
            pl.BlockSpec((tm, D_MODEL), row),
            pl.BlockSpec((None, tm, D_PLE), lambda i: (layer, i, 0)),
            _resident((None, D_MODEL, D_MODEL), lay3),
            vec, vec, vec,
            _resident((None, D_MODEL, 2 * D_FF), lay3),
            _resident((None, D_FF, D_MODEL), lay3),
            vec, vec,
            _resident((None, D_MODEL, D_MODEL), lay3),
            _resident((None, D_PLE, D_MODEL), lay3),
        ],
        out_specs=pl.BlockSpec((tm, D_MODEL), row),
        compiler_params=_params(("parallel",)),
        name="post_mix",
    )(ya, ym, yr, x, p, w_out, gmix, gpre, gpost, w_ffn_in, w_ffn_out, gple_pre, gple_post, w_gate, w_emb)


def _rope_body(pos_ref, invf_ref, cos_ref, sin_ref):
    ang = pos_ref[...].astype(F32) * invf_ref[...]
    sin = jnp.sin(ang)
    dim = lax.broadcasted_iota(jnp.int32, ang.shape, 1) & (HEAD_DIM - 1)
    cos_ref[...] = jnp.cos(ang)
    sin_ref[...] = jnp.where(dim < ROPE_HALF, -sin, jnp.where(dim < ROPE_DIM, sin, 0.0))


def _rope_tables(positions, tm):
    s = positions.shape[0]
    lane = jnp.arange(2 * HEAD_DIM) % HEAD_DIM
    freq = ROPE_THETA ** (-jnp.arange(0, ROPE_DIM, 2, dtype=F32) / ROPE_DIM)
    invf = jnp.where(lane < ROPE_DIM, freq[lane % ROPE_HALF], 0.0).astype(F32)[None, :]
    row = lambda i: (i, 0)
    out = jax.ShapeDtypeStruct((s, 2 * HEAD_DIM), F32)
    return pl.pallas_call(
        _rope_body,
        out_shape=[out, out],
        grid=(s // tm,),
        in_specs=[pl.BlockSpec((tm, 1), row), pl.BlockSpec((1, 2 * HEAD_DIM), lambda i: (0, 0))],
        out_specs=[pl.BlockSpec((tm, 2 * HEAD_DIM), row)] * 2,
        compiler_params=_params(("parallel",)),
        name="rope_tables",
    )(positions, invf)


ATTN_BLOCKS = 8
GROUP_W = ATTN_GROUP * HEAD_DIM
KEYS = 2 * WINDOW


def _rope_partner():
    row, col = _tri_masks(2 * HEAD_DIM)
    dim = col & (HEAD_DIM - 1)
    first = (dim < ROPE_HALF) & (row == col + ROPE_HALF)
    second = (dim >= ROPE_HALF) & (dim < ROPE_DIM) & (row == col - ROPE_HALF)
    return (first | second).astype(BF16)


def _attn_body(sinks_ref, p_ref, cos_ref, sin_ref, o_ref, ktprev_ref, vprev_ref, *, nblk):
    step = pl.program_id(0)

    @pl.when(step == 0)
    def _():
        ktprev_ref[...] = jnp.zeros_like(ktprev_ref)
        vprev_ref[...] = jnp.zeros_like(vprev_ref)

    cos, sin = cos_ref[...], sin_ref[...]
    partner = _rope_partner()
    period = 2 * HEAD_DIM

    def rope(x):
        slabs = [x[:, c:c + period] for c in range(0, x.shape[1], period)]
        return jnp.concatenate(
            [s * cos + jnp.dot(s.astype(BF16), partner, preferred_element_type=F32) * sin for s in slabs], axis=1)

    q = (rope(p_ref[:, 0:ATTN_W]) * (HEAD_DIM ** -0.5)).astype(BF16)
    k_cur = rope(p_ref[:, ATTN_W:ATTN_W + KV_W])
    v_f32 = p_ref[:, ATTN_W + KV_W:ATTN_COLS]
    kt = jnp.concatenate([ktprev_ref[...], k_cur.T.astype(BF16)], axis=1)
    vv = jnp.concatenate([vprev_ref[0], v_f32.astype(BF16)], axis=0)
    vs = jnp.concatenate([vprev_ref[1], pltpu.roll(v_f32, HEAD_DIM, 1).astype(BF16)], axis=0)
    ktprev_ref[...] = kt[:, nblk * WINDOW:]
    vprev_ref[0] = vv[nblk * WINDOW:, :]
    vprev_ref[1] = vs[nblk * WINDOW:, :]

    t = lax.broadcasted_iota(jnp.int32, (WINDOW, ATTN_GROUP * KEYS), 0)
    lane = lax.broadcasted_iota(jnp.int32, (WINDOW, ATTN_GROUP * KEYS), 1)
    c = lane & (KEYS - 1)
    head = lane >> (KEYS.bit_length() - 1)
    cur_ok = (c >= WINDOW) & (c - WINDOW <= t)
    prev_ok = (c < WINDOW) & (c > t)

    def bias(g, has_prev):
        b = jnp.where(cur_ok | (prev_ok & has_prev), 0.0, NEG_INF)
        sink = jnp.zeros_like(b)
        for n in range(ATTN_GROUP):
            sink = jnp.where(head == n, sinks_ref[g * ATTN_GROUP + n], sink)
        return jnp.where(c == 0, sink, b)

    bias_first = [bias(g, step > 0) for g in range(ATTN_KV_HEADS)]
    bias_rest = [bias(g, True) for g in range(ATTN_KV_HEADS)]

    key_lane = lax.broadcasted_iota(jnp.int32, (HEAD_DIM, KEYS), 1)
    zero_kt = jnp.zeros((HEAD_DIM, KEYS), BF16)
    vrow = lax.broadcasted_iota(jnp.int32, (KEYS, KV_W), 0)
    v_low = lax.broadcasted_iota(jnp.int32, (KEYS, KV_W), 1) < HEAD_DIM
    o_low = lax.broadcasted_iota(jnp.int32, (WINDOW, KV_W), 1) < HEAD_DIM
    one = jnp.ones((KEYS, KV_W), BF16)

    def blocks():
        units = [(b, g) for b in range(nblk) for g in range(ATTN_KV_HEADS)]
        kbd, v_even, v_odd = [], [], []
        for b, g in units:
            ktg = kt[g * HEAD_DIM:(g + 1) * HEAD_DIM, b * WINDOW:b * WINDOW + KEYS]
            ktg = jnp.where(key_lane == 0, jnp.zeros_like(ktg), ktg)
            kbd.append(jnp.concatenate(
                [jnp.concatenate([ktg if m == n else zero_kt for m in range(ATTN_GROUP)], axis=1)
                 for n in range(ATTN_GROUP)], axis=0))
            window = slice(b * WINDOW, b * WINDOW + KEYS)
            in_low, in_high = (vv, vs) if g == 0 else (vs, vv)
            v_even.append(jnp.where(v_low, jnp.where(vrow == 0, jnp.zeros_like(one), in_low[window, :]), one))
            v_odd.append(jnp.where(v_low, one, jnp.where(vrow == 0, jnp.zeros_like(one), in_high[window, :])))
        yield
        scores = [jnp.dot(q[b * WINDOW:(b + 1) * WINDOW, g * GROUP_W:(g + 1) * GROUP_W], kb,
                          preferred_element_type=F32) + (bias_first[g] if b == 0 else bias_rest[g])
                  for (b, g), kb in zip(units, kbd)]
        yield
        probs = []
        for s in scores:
            parts = []
            for n in range(ATTN_GROUP):
                sn = s[:, n * KEYS:(n + 1) * KEYS]
                parts.append(jnp.exp(sn - jnp.max(sn, axis=-1, keepdims=True)).astype(BF16))
            probs.append(parts)
        yield
        outs = [[jnp.dot(pn, even if n % 2 == 0 else odd, preferred_element_type=F32) for n, pn in enumerate(parts)]
                for parts, even, odd in zip(probs, v_even, v_odd)]
        yield
        for (b, g), heads in zip(units, outs):
            normed = [od * pltpu.roll(1.0 / od, HEAD_DIM, 1) for od in heads]
            pairs = [jnp.where(o_low, normed[n], normed[n + 1]) for n in range(0, ATTN_GROUP, 2)]
            o_ref[b * WINDOW:(b + 1) * WINDOW, g * GROUP_W:(g + 1) * GROUP_W] = (
                jnp.concatenate(pairs, axis=1).astype(o_ref.dtype))

    _interleave(blocks())


def _attention(sinks, attn_p, cos, sin, layer):
    s = attn_p.shape[0]
    nblk = min(ATTN_BLOCKS, s // WINDOW)
    tq = nblk * WINDOW
    row = lambda i: (i, 0)
    return pl.pallas_call(
        functools.partial(_attn_body, nblk=nblk),
        out_shape=jax.ShapeDtypeStruct((s, ATTN_W), BF16),
        grid=(s // tq,),
        in_specs=[
            pl.BlockSpec(memory_space=pltpu.SMEM),
            pl.BlockSpec((tq, ATTN_COLS), row),
            pl.BlockSpec((tq, 2 * HEAD_DIM), row),
            pl.BlockSpec((tq, 2 * HEAD_DIM), row),
        ],
        out_specs=pl.BlockSpec((tq, ATTN_W), row),
        scratch_shapes=[pltpu.VMEM((KV_W, WINDOW), BF16), pltpu.VMEM((2, WINDOW, KV_W), BF16)],
        compiler_params=_params(("arbitrary",)),
        name="swa_attention",
    )(sinks[layer], attn_p, cos, sin)


def _mlstm_body(p_ref, gates_ref, bias_ref, norm_ref, o_ref, c_ref, n_ref, *, tb):
    step = pl.program_id(0)

    @pl.when(step == 0)
    def _():
        c_ref[...] = jnp.zeros_like(c_ref)
        n_ref[...] = jnp.zeros_like(n_ref)

    def gate_lanes(r0, n):
        pre = GATE_CAP * jnp.tanh((gates_ref[r0:r0 + n, :] + bias_ref[...]) / GATE_CAP)
        yield
        logsig = jnp.minimum(pre, 0.0) - jnp.log(1.0 + jnp.exp(-jnp.abs(pre)))
        yield
        g_cum = _cumsum_rows(_chunk_tri(n), logsig)
        yield
        gate_col = lax.broadcasted_iota(jnp.int32, (n, MLSTM_GATES), 1)
        lanes = _split_dot(jnp.where(gate_col < MLSTM_HEADS, pre, g_cum), _gate_spread())
        return lanes[:, 0:MLSTM_W], lanes[:, MLSTM_W:2 * MLSTM_W]

    head_ones = _head_ones(MLSTM_W)

    hshift = HEAD_DIM.bit_length() - 1
    lane = lax.broadcasted_iota(jnp.int32, (CHUNK, MLSTM_W), 1)
    time = lax.broadcasted_iota(jnp.int32, (CHUNK, MLSTM_W), 0)
    key_of_lane = lane & (HEAD_DIM - 1)
    causal = key_of_lane <= time
    eye = key_of_lane == time
    brow, bcol = _tri_masks(MLSTM_W)
    same_head = (brow >> hshift) == (bcol >> hshift)
    head_block = same_head.astype(BF16)
    stack = lambda x: jnp.where(same_head, jnp.concatenate([x] * MLSTM_HEADS, axis=0), 0.0)

    def to_row(lane_bcast):
        return jnp.sum(jnp.where(eye, lane_bcast, 0.0), axis=0, keepdims=True)

    state = dict(c=c_ref[...], n=n_ref[0:1, :])

    def chunks(r0, n, i_all, g_all):
        chunk_starts = range(0, n, CHUNK)
        local = lambda x: [x[c0:c0 + CHUNK, :] for c0 in chunk_starts]
        block = lambda lo, hi: [p_ref[r0 + c0:r0 + c0 + CHUNK, lo:hi] for c0 in chunk_starts]
        q, k, v = block(0, MLSTM_W), block(MLSTM_W, 2 * MLSTM_W), block(2 * MLSTM_W, 3 * MLSTM_W)
        qk = [_mm_nt(a, stack(b)) for a, b in zip(q, k)]
        yield
        g_chunk, i_chunk = local(g_all), local(i_all)
        g_tot = [g[CHUNK - 1:CHUNK, :] for g in g_chunk]
        w_in = [jnp.exp(t - g + i) for t, g, i in zip(g_tot, g_chunk, i_chunk)]
        e_tot = [jnp.exp(t) for t in g_tot]
        e_g = [jnp.exp(g) for g in g_chunk]
        n_loc = [jnp.sum(w * x, axis=0, keepdims=True) for w, x in zip(w_in, k)]
        yield
        decay = [jnp.exp(jnp.where(causal, g - to_row(g) + to_row(i), NEG_INF)) for g, i in zip(g_chunk, i_chunk)]
        s_mat = [x * d for x, d in zip(qk, decay)]
        yield
        num_den = [_mm(s, jnp.concatenate([stack(x).astype(BF16), head_block], axis=1))
                   for s, x in zip(s_mat, v)]
        yield
        c_loc = [jnp.where(same_head, _mm_tn(w * x, b), 0.0) for w, x, b in zip(w_in, v, k)]
        yield
        c_in, n_in = [], []
        for m in range(len(chunk_starts)):
            c_in.append(state["c"])
            n_in.append(state["n"])
            state["c"] = e_tot[m] * state["c"] + c_loc[m]
            state["n"] = e_tot[m] * state["n"] + n_loc[m]
        inter = [_mm_nt(a, jnp.concatenate([c, stack(jnp.broadcast_to(x, (CHUNK, MLSTM_W)))], axis=0))
                 for a, c, x in zip(q, c_in, n_in)]
        yield
        for c0, nd, e, x in zip(chunk_starts, num_den, e_g, inter):
            hid = ((nd[:, :MLSTM_W] + e * x[:, :MLSTM_W])
                   / jnp.maximum(jnp.abs(nd[:, MLSTM_W:] + e * x[:, MLSTM_W:]), 1.0))
            mean_sq = _split_dot(hid * hid, head_ones, terms=2) * (1.0 / HEAD_DIM)
            hid = hid * lax.rsqrt(mean_sq + NORM_EPS) * norm_ref[...]
            rows = slice(r0 + c0, r0 + c0 + CHUNK)
            o_ref[rows, :] = (jax.nn.sigmoid(p_ref[rows, 3 * MLSTM_W:4 * MLSTM_W]) * hid).astype(o_ref.dtype)
            yield

    size = tb if tb < 2 * MLSTM_PART else MLSTM_PART
    starts = list(range(0, tb, size))
    gates = _interleave(gate_lanes(starts[0], size))[0]
    for n, r0 in enumerate(starts):
        traces = [chunks(r0, size, *gates)]
        if n + 1 < len(starts):
            traces.append(gate_lanes(starts[n + 1], size))
        results = _interleave(*traces)
        gates = results[-1]
    c_ref[...] = state["c"]
    n_ref[0:1, :] = state["n"]


def _mlstm(mlstm_p, gates, bias, norm, layer, tb):
    s = mlstm_p.shape[0]
    row = lambda i: (i, 0)
    return pl.pallas_call(
        functools.partial(_mlstm_body, tb=tb),
        out_shape=jax.ShapeDtypeStruct((s, MLSTM_W), BF16),
        grid=(s // tb,),
        in_specs=[
            pl.BlockSpec((tb, MLSTM_MAIN), row),
            pl.BlockSpec((tb, MLSTM_GATES), row),
            _layer_vec(MLSTM_GATES, layer),
            _layer_vec(MLSTM_W, layer),
        ],
        out_specs=pl.BlockSpec((tb, MLSTM_W), row),
        scratch_shapes=[
            pltpu.VMEM((MLSTM_W, MLSTM_W), F32),
            pltpu.VMEM((SUBLANES, MLSTM_W), F32),
        ],
        compiler_params=_params(("arbitrary",)),
        name="mlstm",
    )(mlstm_p, gates, bias, norm)


def _rwkv_chunk_terms(a_t, b_t, k_t, r_t, b_end, k_end, v):
    row, col = _tri_masks(CHUNK)
    incl = col <= row
    strict = col < row
    eye = (col == row).astype(F32)
    half = HEAD_DIM

    quad = [_mm_nt(jnp.concatenate([a, r], axis=0), jnp.concatenate([b, k], axis=0))
            for a, r, b, k in zip(a_t, r_t, b_t, k_t)]
    yield
    n_mat = [jnp.where(strict, x[:CHUNK, :CHUNK], 0.0) for x in quad]
    a_ak = [jnp.where(strict, x[:CHUNK, CHUNK:], 0.0) for x in quad]
    c_rb = [jnp.where(incl, x[CHUNK:, :CHUNK], 0.0) for x in quad]
    c_rk = [jnp.where(incl, x[CHUNK:, CHUNK:], 0.0) for x in quad]

    inv = [eye + n for n in n_mat]
    power = n_mat
    for _ in range(NEUMANN_STEPS):
        power = [_mm(x, x) for x in power]
        yield
        inv = [m + _mm(m, x) for m, x in zip(inv, power)]
        yield

    z = [_mm(x, y) for x, y in zip(a_ak, v)]
    yield
    w12 = [_mm(m, jnp.concatenate([a, y], axis=1)) for m, a, y in zip(inv, a_t, z)]
    yield
    cw = [_mm(c, w) for c, w in zip(c_rb, w12)]
    yield
    ckv = [_mm(c, y) for c, y in zip(c_rk, v)]
    yield
    gh = [_mm_tn(w, b) for w, b in zip(w12, b_end)]
    yield
    vk = [_mm_tn(y, k) for y, k in zip(v, k_end)]
    yield
    q_eff = [r + x[:, :half] for r, x in zip(r_t, cw)]
    y_loc = [x[:, half:] + y for x, y in zip(cw, ckv)]
    g_mat = [x[:half, :] for x in gh]
    h_mat = [x[half:, :] + y for x, y in zip(gh, vk)]
    return q_eff, y_loc, g_mat, h_mat


def _rwkv_body(p_ref, w0_ref, wup_ref, a0_ref, aup_ref, gup_ref, kk_ref, ka_ref, rk_ref,
               lnw_ref, lnb_ref, o_ref, s_ref, stage_ref, end_ref, terms_ref, fwd_ref, fwd_end_ref, *, tb):
    step = pl.program_id(0)

    @pl.when(step == 0)
    def _():
        s_ref[...] = jnp.zeros_like(s_ref)
        stage_ref[...] = jnp.zeros_like(stage_ref)
        end_ref[...] = jnp.zeros_like(end_ref)
        terms_ref[...] = jnp.zeros_like(terms_ref)
        fwd_ref[...] = jnp.zeros_like(fwd_ref)
        fwd_end_ref[...] = jnp.zeros_like(fwd_end_ref)

    refs = (p_ref, w0_ref, wup_ref, a0_ref, aup_ref, gup_ref, kk_ref, ka_ref, rk_ref,
            lnw_ref, lnb_ref, o_ref, s_ref, stage_ref, end_ref, terms_ref, fwd_ref, fwd_end_ref)
    for cur in (0, 1):
        pl.when(lax.rem(step, 2) == cur)(functools.partial(_rwkv_step, *refs, tb=tb, cur=cur, prv=1 - cur))


def _rwkv_step(p_ref, w0_ref, wup_ref, a0_ref, aup_ref, gup_ref, kk_ref, ka_ref, rk_ref,
               lnw_ref, lnb_ref, o_ref, s_ref, stage_ref, end_ref, terms_ref, fwd_ref, fwd_end_ref,
               *, tb, cur, prv):
    head_ones = _head_ones(RWKV_W)
    head_sum = lambda x: _split_dot(x, head_ones, terms=2)
    head_lanes = lambda h: slice(h * HEAD_DIM, (h + 1) * HEAD_DIM)
    chunk_starts = range(0, tb, CHUNK)
    units = [(n, c0, h) for n, c0 in enumerate(chunk_starts) for h in range(RWKV_HEADS)]

    unit_rows = lambda u: slice(u * CHUNK, (u + 1) * CHUNK)

    def terms_previous():
        staged = {name: stage_ref[prv, n] for n, name in enumerate(RWKV_STAGED)}
        per_unit = {name: [staged[name][c0:c0 + CHUNK, head_lanes(h)] for _, c0, h in units]
                    for name in RWKV_STAGED[:7]}
        terms = yield from _rwkv_chunk_terms(**per_unit)
        for kind, per_unit_values in enumerate(terms):
            for u, value in enumerate(per_unit_values):
                terms_ref[prv, kind, unit_rows(u), :] = value
            yield
        for n, name in enumerate(RWKV_FORWARDED):
            fwd_ref[prv, n] = staged[name]
        fwd_end_ref[prv] = end_ref[prv]

    def finish_older():
        ends = fwd_end_ref[cur]
        state = [s_ref[h] for h in range(RWKV_HEADS)]
        ys = []
        for u, (n, c0, h) in enumerate(units):
            q_eff, y_loc, g_mat, h_mat = (terms_ref[cur, kind, unit_rows(u), :] for kind in range(4))
            ys.append(_mm_nt(q_eff, state[h]) + y_loc)
            state[h] = state[h] * ends[n:n + 1, head_lanes(h)] + _mm(state[h], g_mat) + h_mat
            if h == RWKV_HEADS - 1:
                yield
        for h in range(RWKV_HEADS):
            s_ref[h] = state[h]
        y_all = jnp.concatenate([jnp.concatenate(ys[n:n + RWKV_HEADS], axis=1)
                                 for n in range(0, len(ys), RWKV_HEADS)], axis=0)
        centred = y_all - head_sum(y_all) * (1.0 / HEAD_DIM)
        yield
        var = head_sum(centred * centred) * (1.0 / HEAD_DIM)
        normed = centred * lax.rsqrt(var + RWKV_GN_EPS) * lnw_ref[...] + lnb_ref[...]
        bonus, v_all, gate = (fwd_ref[cur, n] for n in range(len(RWKV_FORWARDED)))
        o_ref[...] = ((normed + bonus * v_all) * gate).astype(o_ref.dtype)

    def stage_current():
        for r0 in range(0, tb, RWKV_SLAB):
            rows = slice(r0, r0 + RWKV_SLAB)

            def stage(name, value):
                stage_ref[cur, RWKV_STAGED.index(name), rows, :] = value

            u = p_ref[rows, :]
            yield
            r_all = u[:, 0:RWKV_W]
            k_raw = u[:, RWKV_W:2 * RWKV_W]
            v_all = u[:, 2 * RWKV_W:3 * RWKV_W]
            x_w = u[:, 3 * RWKV_W:3 * RWKV_W + RWKV_W_RANK]
            x_a = u[:, 3 * RWKV_W + RWKV_W_RANK:3 * RWKV_W + RWKV_W_RANK + RWKV_A_RANK]
            x_g = u[:, 3 * RWKV_W + RWKV_W_RANK + RWKV_A_RANK:RWKV_COLS]
            z = w0_ref[...] + _mm(jnp.tanh(x_w), wup_ref[...])
            ld_all = -jnp.exp(-0.5) * jax.nn.sigmoid(z)
            a_all = jax.nn.sigmoid(a0_ref[...] + _mm(x_a, aup_ref[...]))
            yield
            kk_all = k_raw * kk_ref[...]
            k_all = k_raw * (1.0 + (a_all - 1.0) * ka_ref[...])
            kk_all = kk_all * lax.rsqrt(jnp.maximum(head_sum(kk_all * kk_all), 1e-24))
            a_vec = -kk_all
            b_vec = kk_all * a_all
            yield
            lp_all = _cumsum_rows(_chunk_tri(RWKV_SLAB), ld_all)
            stage("v", v_all)
            stage("bonus", head_sum(r_all * k_all * rk_ref[...]))
            stage("gate", _mm(jax.nn.sigmoid(x_g), gup_ref[...]))
            yield
            starts = range(0, RWKV_SLAB, CHUNK)
            lp_last = jnp.concatenate(
                [jnp.broadcast_to(lp_all[c0 + CHUNK - 1:c0 + CHUNK, :], (CHUNK, RWKV_W)) for c0 in starts], axis=0)
            grow = jnp.exp(-lp_all)
            to_end = jnp.exp(lp_last - lp_all)
            stage("a_t", a_vec * jnp.exp(lp_all - ld_all))
            stage("b_t", b_vec * grow)
            stage("k_t", k_all * grow)
            yield
            stage("r_t", r_all * jnp.exp(lp_all))
            stage("b_end", b_vec * to_end)
            stage("k_end", k_all * to_end)
            for c0 in starts:
                n = (r0 + c0) // CHUNK
                end_ref[cur, n:n + 1, :] = jnp.exp(lp_all[c0 + CHUNK - 1:c0 + CHUNK, :])
            yield

    _interleave(terms_previous(), finish_older(), stage_current())


def _rwkv(rwkv_p, w0, w_up, a0, a_up, g_up, k_k, k_a, r_k, ln_w, ln_b, layer, tb):
    s = rwkv_p.shape[0]
    blocks = s // tb
    lay3 = lambda i: (layer, 0, 0)
    vec = lambda width: _layer_vec(width, layer)
    return pl.pallas_call(
        functools.partial(_rwkv_body, tb=tb),
        out_shape=jax.ShapeDtypeStruct((s, RWKV_W), BF16),
        grid=(blocks + 2,),
        in_specs=[
            pl.BlockSpec((tb, RWKV_COLS), lambda i: (jnp.minimum(i, blocks - 1), 0)),
            vec(RWKV_W),
            pl.BlockSpec((None, RWKV_W_RANK, RWKV_W), lay3),
            vec(RWKV_W),
            pl.BlockSpec((None, RWKV_A_RANK, RWKV_W), lay3),
            pl.BlockSpec((None, RWKV_G_RANK, RWKV_W), lay3),
            vec(RWKV_W), vec(RWKV_W), vec(RWKV_W), vec(RWKV_W), vec(RWKV_W),
        ],
        out_specs=pl.BlockSpec((tb, RWKV_W), lambda i: (jnp.maximum(i - 2, 0), 0)),
        scratch_shapes=[
            pltpu.VMEM((RWKV_HEADS, HEAD_DIM, HEAD_DIM), F32),
            pltpu.VMEM((2, len(RWKV_STAGED), tb, RWKV_W), F32),
            pltpu.VMEM((2, tb // CHUNK, RWKV_W), F32),
            pltpu.VMEM((2, 4, (tb // CHUNK) * RWKV_HEADS * CHUNK, HEAD_DIM), F32),
            pltpu.VMEM((2, len(RWKV_FORWARDED), tb, RWKV_W), F32),
            pltpu.VMEM((2, tb // CHUNK, RWKV_W), F32),
        ],
        compiler_params=_params(("arbitrary",)),
        name="rwkv7",
    )(rwkv_p, w0, w_up, a0, a_up, g_up, k_k, k_a, r_k, ln_w, ln_b)


def _tiles(s):
    return min(512, s), min(512, s)


def kernel(x, p, positions, ln_ffn1_pre, ln_ffn1_post, w_ffn1_in, w_ffn1_out, ln_mix_pre, w_in, attn_sinks, mlstm_conv, mlstm_i_bias, mlstm_f_bias, mlstm_norm, rwkv_mu, rwkv_w0, rwkv_w_up, rwkv_a0, rwkv_a_up, rwkv_g_up, rwkv_k_k, rwkv_k_a, rwkv_r_k, rwkv_ln_w, rwkv_ln_b, w_out, ln_mix_post, ln_ffn2_pre, ln_ffn2_post, w_ffn2_in, w_ffn2_out, ln_ple_pre, w_ple_gate, w_ple_proj, ln_ple_post):
    batch, seq, _ = x.shape
    assert batch == 1 and seq % WINDOW == 0
    depth = w_in.shape[0]
    tm, tb = _tiles(seq)

    bf = lambda w: w.astype(BF16)
    w_ffn1_in, w_ffn1_out, w_ffn2_in, w_ffn2_out = map(bf, (w_ffn1_in, w_ffn1_out, w_ffn2_in, w_ffn2_out))
    w_out, w_ple_gate, w_ple_proj = map(bf, (w_out, w_ple_gate, w_ple_proj))
    rwkv_w_up, rwkv_a_up, rwkv_g_up = map(bf, (rwkv_w_up, rwkv_a_up, rwkv_g_up))
    m0 = ATTN_COLS
    g0 = m0 + MLSTM_MAIN
    r0 = g0 + MLSTM_GATES
    w_proj = bf(jnp.concatenate([w_in[:, :, m0:g0], w_in[:, :, r0:], w_in[:, :, :m0], w_in[:, :, g0:r0]], axis=-1))
    vec = lambda a: a.reshape(depth, 1, -1)
    gate_bias = vec(jnp.concatenate([mlstm_i_bias, mlstm_f_bias], axis=-1))
    (ln_ffn1_pre, ln_ffn1_post, ln_mix_pre, ln_mix_post, ln_ffn2_pre, ln_ffn2_post, ln_ple_pre,
     ln_ple_post, mlstm_norm, rwkv_mu, rwkv_w0, rwkv_a0, rwkv_k_k, rwkv_k_a, rwkv_r_k, rwkv_ln_w,
     rwkv_ln_b) = map(vec, (
         ln_ffn1_pre, ln_ffn1_post, ln_mix_pre, ln_mix_post, ln_ffn2_pre, ln_ffn2_post, ln_ple_pre,
         ln_ple_post, mlstm_norm, rwkv_mu, rwkv_w0, rwkv_a0, rwkv_k_k, rwkv_k_a, rwkv_r_k, rwkv_ln_w,
         rwkv_ln_b))

    cos, sin = _rope_tables(positions.reshape(seq, 1), tm)
    xs = x.reshape(seq, D_MODEL)
    for l in range(depth):
        xs, mlstm_p, rwkv_p, attn_p, gates = _pre_mix(
            xs, ln_ffn1_pre, ln_ffn1_post, w_ffn1_in, w_ffn1_out, ln_mix_pre, w_proj, mlstm_conv, rwkv_mu, l, tm)
        y_attn = _attention(attn_sinks, attn_p, cos, sin, l)
        y_mlstm = _mlstm(mlstm_p, gates, gate_bias, mlstm_norm, l, tb)
        y_rwkv = _rwkv(rwkv_p, rwkv_w0, rwkv_w_up, rwkv_a0, rwkv_a_up, rwkv_g_up,
                       rwkv_k_k, rwkv_k_a, rwkv_r_k, rwkv_ln_w, rwkv_ln_b, l, tb)
        xs = _post_mix(y_attn, y_mlstm, y_rwkv, xs, p.reshape(depth, seq, D_PLE), w_out, ln_mix_post,
                       ln_ffn2_pre, ln_ffn2_post, w_ffn2_in, w_ffn2_out, ln_ple_pre, ln_ple_post,
                       w_ple_gate, w_ple_proj, l, tm)
    return xs.reshape(batch, seq, D_MODEL)
```

```python
import functools

import jax
import jax.numpy as jnp
from jax import lax
from jax.experimental import pallas as pl
from jax.experimental.pallas import tpu as pltpu

F32 = jnp.float32
BF16 = jnp.bfloat16

D_MODEL = 1024
HEAD_DIM = 64
D_FF = 2816
D_PLE = 256
ATTN_Q_HEADS = 8
ATTN_KV_HEADS = 2
ATTN_GROUP = ATTN_Q_HEADS // ATTN_KV_HEADS
WINDOW = 128
ROPE_THETA = 500000.0
ROPE_DIM = HEAD_DIM // 4
ROPE_HALF = ROPE_DIM // 2
MLSTM_HEADS = 4
MLSTM_CONV = 4
GATE_CAP = 15.0
RWKV_HEADS = 4
RWKV_W_RANK = 64
RWKV_A_RANK = 64
RWKV_G_RANK = 128
RWKV_GN_EPS = 64e-5
NORM_EPS = 1e-6
NEG_INF = -1e30

ATTN_W = ATTN_Q_HEADS * HEAD_DIM
KV_W = ATTN_KV_HEADS * HEAD_DIM
MLSTM_W = MLSTM_HEADS * HEAD_DIM
RWKV_W = RWKV_HEADS * HEAD_DIM
ATTN_COLS = ATTN_W + 2 * KV_W
MLSTM_MAIN = 4 * MLSTM_W
MLSTM_GATES = 2 * MLSTM_HEADS
RWKV_COLS = 3 * RWKV_W + RWKV_W_RANK + RWKV_A_RANK + RWKV_G_RANK

CHUNK = 64
NEUMANN_STEPS = 5
PRE_MIX_PART = 128
POST_MIX_PART = 256
MLSTM_PART = 256
RWKV_SLAB = 128
RWKV_FORWARDED = ("bonus", "v", "gate")
RWKV_STAGED = ("a_t", "b_t", "k_t", "r_t", "b_end", "k_end", "v", "bonus", "gate")
MXU_TILE = 256
FF_SPLITS = (0, 6 * MXU_TILE, D_FF)
SUBLANES = 8
V7X_VMEM_LIMIT = 56 * 1024 * 1024


def _params(semantics):
    return pltpu.CompilerParams(dimension_semantics=semantics, vmem_limit_bytes=V7X_VMEM_LIMIT)


def _resident(block_shape, index_map):
    return pl.BlockSpec(block_shape, index_map, pipeline_mode=pl.Buffered(1))


def _layer_vec(width, layer):
    return pl.BlockSpec((None, 1, width), lambda i: (layer, 0, 0))


def _rms(x, gain):
    return x * lax.rsqrt(jnp.mean(x * x, axis=-1, keepdims=True) + NORM_EPS) * gain


def _mm(a, b):
    return jnp.dot(a.astype(BF16), b.astype(BF16), preferred_element_type=F32)


def _mm_nt(a, b):
    return lax.dot_general(a.astype(BF16), b.astype(BF16), (((1,), (1,)), ((), ())),
                           preferred_element_type=F32)


def _mm_tn(a, b):
    return lax.dot_general(a.astype(BF16), b.astype(BF16), (((0,), (0,)), ((), ())),
                           preferred_element_type=F32)


def _bf16_terms(x, terms):
    out, rest = [], x
    for n in range(terms):
        part = rest.astype(BF16)
        out.append(part)
        if n + 1 < terms:
            rest = rest - part.astype(F32)
    return out


def _cumsum_rows(tri, x):
    return sum(jnp.dot(tri, term, preferred_element_type=F32) for term in _bf16_terms(x, 3))


def _split_dot(x, ones, terms=3):
    return sum(jnp.dot(term, ones, preferred_element_type=F32) for term in _bf16_terms(x, terms))


def _head_ones(width):
    row, col = _tri_masks(width)
    shift = HEAD_DIM.bit_length() - 1
    return ((row >> shift) == (col >> shift)).astype(BF16)


def _gate_spread():
    lanes = MLSTM_GATES * HEAD_DIM
    row = lax.broadcasted_iota(jnp.int32, (MLSTM_GATES, lanes), 0)
    col = lax.broadcasted_iota(jnp.int32, (MLSTM_GATES, lanes), 1)
    return (row == (col >> (HEAD_DIM.bit_length() - 1))).astype(BF16)


def _interleave(*gens):
    results = [None] * len(gens)
    live = dict(enumerate(gens))
    while live:
        for n in list(live):
            try:
                next(live[n])
            except StopIteration as stop:
                results[n] = stop.value
                del live[n]
    return results


def _tri_masks(n):
    row = lax.broadcasted_iota(jnp.int32, (n, n), 0)
    col = lax.broadcasted_iota(jnp.int32, (n, n), 1)
    return row, col


def _chunk_tri(n):
    row, col = _tri_masks(n)
    shift = CHUNK.bit_length() - 1
    return (((row >> shift) == (col >> shift)) & (col <= row)).astype(BF16)


def _macaron_half_step(x, gpre_ref, gpost_ref, win_ref, wout_ref):
    xn = _rms(x, gpre_ref[...]).astype(BF16)
    yield
    splits = list(zip(FF_SPLITS[:-1], FF_SPLITS[1:]))
    gate_up = []
    acc = None
    for n in range(len(splits) + 1):
        if n < len(splits):
            lo, hi = splits[n]
            gate_up.append((jnp.dot(xn, win_ref[:, lo:hi], preferred_element_type=F32),
                            jnp.dot(xn, win_ref[:, D_FF + lo:D_FF + hi], preferred_element_type=F32)))
        if n > 0:
            lo, hi = splits[n - 1]
            gate, up = gate_up[n - 1]
            act = (gate * jax.nn.sigmoid(gate) * up).astype(BF16)
            yield
            part = jnp.dot(act, wout_ref[lo:hi, :], preferred_element_type=F32)
            acc = part if acc is None else acc + part
        yield
    return x + 0.5 * _rms(acc, gpost_ref[...])


def _staggered_row_parts(rows, size, tile):
    if rows < 2 * size:
        return _interleave(tile(0, rows))
    parts = rows // size

    def delayed(gen, phases):
        for _ in range(phases):
            yield
        yield from gen

    return _interleave(*[delayed(tile(n * size, size), n) for n in range(parts)])


def _pre_mix_body(x_ref, gpre_ref, gpost_ref, win_ref, wout_ref, gmix_ref, wproj_ref, conv_ref, mu_ref,
                  x_out_ref, om_ref, or_ref, oa_ref, og_ref, qk_pad_ref, u_pad_ref):
    @pl.when(pl.program_id(0) == 0)
    def _():
        qk_pad_ref[0:SUBLANES, :] = jnp.zeros((SUBLANES, 2 * MLSTM_W), F32)
        u_pad_ref[0:SUBLANES, :] = jnp.zeros((SUBLANES, RWKV_COLS), F32)

    def tile(r0, n):
        rows = slice(r0, r0 + n)
        x = yield from _macaron_half_step(x_ref[rows, :], gpre_ref, gpost_ref, win_ref, wout_ref)
        x_out_ref[rows, :] = x
        h = _rms(x, gmix_ref[...]).astype(BF16)
        yield
        starts = [sum(PROJ_WIDTHS[:g]) for g in range(len(PROJ_WIDTHS))]
        proj = [jnp.dot(h, wproj_ref[:, lo:lo + w], preferred_element_type=F32) for lo, w in zip(starts, PROJ_WIDTHS)]
        p_mlstm, p_rwkv, p_attn, p_gates = proj
        oa_ref[rows, :] = p_attn
        og_ref[rows, :] = p_gates
        yield
        qk_pad_ref[SUBLANES:SUBLANES + n, :] = p_mlstm[:, 0:2 * MLSTM_W]
        conv = None
        for tap in range(MLSTM_CONV):
            shifted = qk_pad_ref[pl.ds(SUBLANES - (MLSTM_CONV - 1) + tap, n), :]
            term = shifted * conv_ref[tap:tap + 1, :]
            conv = term if conv is None else conv + term
        qk_pad_ref[0:SUBLANES, :] = qk_pad_ref[n:n + SUBLANES, :]
        qk = conv * jax.nn.sigmoid(conv)
        om_ref[rows, 0:MLSTM_W] = qk[:, 0:MLSTM_W] * (HEAD_DIM ** -0.5)
        om_ref[rows, MLSTM_W:2 * MLSTM_W] = qk[:, MLSTM_W:2 * MLSTM_W]
        om_ref[rows, 2 * MLSTM_W:] = p_mlstm[:, 2 * MLSTM_W:]
        yield
        u_pad_ref[SUBLANES:SUBLANES + n, :] = p_rwkv
        prev = u_pad_ref[pl.ds(SUBLANES - 1, n), :]
        u_pad_ref[0:SUBLANES, :] = u_pad_ref[n:n + SUBLANES, :]
        or_ref[rows, :] = p_rwkv + (prev - p_rwkv) * mu_ref[...]

    _staggered_row_parts(x_ref.shape[0], PRE_MIX_PART, tile)


PROJ_WIDTHS = (MLSTM_MAIN, RWKV_COLS, ATTN_COLS, MLSTM_GATES)


def _pre_mix(x, gpre, gpost, w_ffn_in, w_ffn_out, gmix, w_proj, conv_w, rwkv_mu, layer, tm):
    s = x.shape[0]
    row = lambda i: (i, 0)
    lay3 = lambda i: (layer, 0, 0)
    widths = PROJ_WIDTHS
    part = tm if tm < 2 * PRE_MIX_PART else PRE_MIX_PART
    return pl.pallas_call(
        _pre_mix_body,
        out_shape=[jax.ShapeDtypeStruct((s, D_MODEL), F32)] + [jax.ShapeDtypeStruct((s, w), F32) for w in widths],
        grid=(s // tm,),
        in_specs=[
            pl.BlockSpec((tm, D_MODEL), row),
            _layer_vec(D_MODEL, layer),
            _layer_vec(D_MODEL, layer),
            _resident((None, D_MODEL, 2 * D_FF), lay3),
            _resident((None, D_FF, D_MODEL), lay3),
            _layer_vec(D_MODEL, layer),
            _resident((None, D_MODEL, sum(widths)), lay3),
            pl.BlockSpec((None, MLSTM_CONV, 2 * MLSTM_W), lay3),
            _layer_vec(RWKV_COLS, layer),
        ],
        out_specs=[pl.BlockSpec((tm, D_MODEL), row)] + [pl.BlockSpec((tm, w), row) for w in widths],
        scratch_shapes=[
            pltpu.VMEM((part + SUBLANES, 2 * MLSTM_W), F32),
            pltpu.VMEM((part + SUBLANES, RWKV_COLS), F32),
        ],
        compiler_params=_params(("arbitrary",)),
        name="pre_mix",
    )(x, gpre, gpost, w_ffn_in, w_ffn_out, gmix, w_proj, conv_w, rwkv_mu)


def _post_mix_body(ya_ref, ym_ref, yr_ref, x_ref, p_ref, wo_ref, gmix_ref, gpre_ref, gpost_ref, win_ref, wout_ref,
                   gple_pre_ref, gple_post_ref, wgate_ref, wemb_ref, o_ref):
    def tile(r0, n):
        rows = slice(r0, r0 + n)
        mix = jnp.dot(ya_ref[rows, :], wo_ref[0:ATTN_W, :], preferred_element_type=F32)
        mix += jnp.dot(ym_ref[rows, :], wo_ref[ATTN_W:ATTN_W + MLSTM_W, :], preferred_element_type=F32)
        mix += jnp.dot(yr_ref[rows, :], wo_ref[ATTN_W + MLSTM_W:, :], preferred_element_type=F32)
        emb = jnp.dot(p_ref[rows, :].astype(BF16), wemb_ref[...], preferred_element_type=F32)
        yield
        x = x_ref[rows, :] + _rms(mix, gmix_ref[...])
        x = yield from _macaron_half_step(x, gpre_ref, gpost_ref, win_ref, wout_ref)
        h = _rms(x, gple_pre_ref[...]).astype(BF16)
        yield
        gate = jax.nn.sigmoid(jnp.dot(h, wgate_ref[...], preferred_element_type=F32))
        yield
        o_ref[rows, :] = x + _rms(gate * emb, gple_post_ref[...])

    _staggered_row_parts(x_ref.shape[0], POST_MIX_PART, tile)


def _post_mix(ya, ym, yr, x, p, w_out, gmix, gpre, gpost, w_ffn_in, w_ffn_out, gple_pre, gple_post,
              w_gate, w_emb, layer, tm):
    s = x.shape[0]
    row = lambda i: (i, 0)
    lay3 = lambda i: (layer, 0, 0)
    vec = _layer_vec(D_MODEL, layer)
    return pl.pallas_call(
        _post_mix_body,
        out_shape=jax.ShapeDtypeStruct((s, D_MODEL), F32),
        grid=(s // tm,),
        in_specs=[
            pl.BlockSpec((tm, ATTN_W), row),
            pl.BlockSpec((tm, MLSTM_W), row),
            pl.BlockSpec((tm, RWKV_W), row),
            pl.BlockSpec((tm, D_MODEL), row),
            pl.BlockSpec((None, tm, D_PLE), lambda i: (layer, i, 0)),
            _resident((None, D_MODEL, D_MODEL), lay3),
            vec, vec, vec,
            _resident((None, D_MODEL, 2 * D_FF), lay3),
            _resident((None, D_FF, D_MODEL), lay3),
            vec, vec,
            _resident((None, D_MODEL, D_MODEL), lay3),
            _resident((None, D_PLE, D_MODEL), lay3),
        ],
        out_specs=pl.BlockSpec((tm, D_MODEL), row),
        compiler_params=_params(("parallel",)),
        name="post_mix",
    )(ya, ym, yr, x, p, w_out, gmix, gpre, gpost, w_ffn_in, w_ffn_out, gple_pre, gple_post, w_gate, w_emb)


def _rope_body(pos_ref, invf_ref, cos_ref, sin_ref):
    ang = pos_ref[...].astype(F32) * invf_ref[...]
    sin = jnp.sin(ang)
    dim = lax.broadcasted_iota(jnp.int32, ang.shape, 1) & (HEAD_DIM - 1)
    cos_ref[...] = jnp.cos(ang)
    sin_ref[...] = jnp.where(dim < ROPE_HALF, -sin, jnp.where(dim < ROPE_DIM, sin, 0.0))


def _rope_tables(positions, tm):
    s = positions.shape[0]
    lane = jnp.arange(2 * HEAD_DIM) % HEAD_DIM
    freq = ROPE_THETA ** (-jnp.arange(0, ROPE_DIM, 2, dtype=F32) / ROPE_DIM)
    invf = jnp.where(lane < ROPE_DIM, freq[lane % ROPE_HALF], 0.0).astype(F32)[None, :]
    row = lambda i: (i, 0)
    out = jax.ShapeDtypeStruct((s, 2 * HEAD_DIM), F32)
    return pl.pallas_call(
        _rope_body,
        out_shape=[out, out],
        grid=(s // tm,),
        in_specs=[pl.BlockSpec((tm, 1), row), pl.BlockSpec((1, 2 * HEAD_DIM), lambda i: (0, 0))],
        out_specs=[pl.BlockSpec((tm, 2 * HEAD_DIM), row)] * 2,
        compiler_params=_params(("parallel",)),
        name="rope_tables",
    )(positions, invf)


ATTN_BLOCKS = 8
GROUP_W = ATTN_GROUP * HEAD_DIM
KEYS = 2 * WINDOW


def _rope_partner():
    row, col = _tri_masks(2 * HEAD_DIM)
    dim = col & (HEAD_DIM - 1)
    first = (dim < ROPE_HALF) & (row == col + ROPE_HALF)
    second = (dim >= ROPE_HALF) & (dim < ROPE_DIM) & (row == col - ROPE_HALF)
    return (first | second).astype(BF16)


def _attn_body(sinks_ref, p_ref, cos_ref, sin_ref, o_ref, ktprev_ref, vprev_ref, *, nblk):
    step = pl.program_id(0)

    @pl.when(step == 0)
    def _():
        ktprev_ref[...] = jnp.zeros_like(ktprev_ref)
        vprev_ref[...] = jnp.zeros_like(vprev_ref)

    cos, sin = cos_ref[...], sin_ref[...]
    partner = _rope_partner()
    period = 2 * HEAD_DIM

    def rope(x):
        slabs = [x[:, c:c + period] for c in range(0, x.shape[1], period)]
        return jnp.concatenate(
            [s * cos + jnp.dot(s.astype(BF16), partner, preferred_element_type=F32) * sin for s in slabs], axis=1)

    q = (rope(p_ref[:, 0:ATTN_W]) * (HEAD_DIM ** -0.5)).astype(BF16)
    k_cur = rope(p_ref[:, ATTN_W:ATTN_W + KV_W])
    v_f32 = p_ref[:, ATTN_W + KV_W:ATTN_COLS]
    kt = jnp.concatenate([ktprev_ref[...], k_cur.T.astype(BF16)], axis=1)
    vv = jnp.concatenate([vprev_ref[0], v_f32.astype(BF16)], axis=0)
    vs = jnp.concatenate([vprev_ref[1], pltpu.roll(v_f32, HEAD_DIM, 1).astype(BF16)], axis=0)
    ktprev_ref[...] = kt[:, nblk * WINDOW:]
    vprev_ref[0] = vv[nblk * WINDOW:, :]
    vprev_ref[1] = vs[nblk * WINDOW:, :]

    t = lax.broadcasted_iota(jnp.int32, (WINDOW, ATTN_GROUP * KEYS), 0)
    lane = lax.broadcasted_iota(jnp.int32, (WINDOW, ATTN_GROUP * KEYS), 1)
    c = lane & (KEYS - 1)
    head = lane >> (KEYS.bit_length() - 1)
    cur_ok = (c >= WINDOW) & (c - WINDOW <= t)
    prev_ok = (c < WINDOW) & (c > t)

    def bias(g, has_prev):
        b = jnp.where(cur_ok | (prev_ok & has_prev), 0.0, NEG_INF)
        sink = jnp.zeros_like(b)
        for n in range(ATTN_GROUP):
            sink = jnp.where(head == n, sinks_ref[g * ATTN_GROUP + n], sink)
        return jnp.where(c == 0, sink, b)

    bias_first = [bias(g, step > 0) for g in range(ATTN_KV_HEADS)]
    bias_rest = [bias(g, True) for g in range(ATTN_KV_HEADS)]

    key_lane = lax.broadcasted_iota(jnp.int32, (HEAD_DIM, KEYS), 1)
    zero_kt = jnp.zeros((HEAD_DIM, KEYS), BF16)
    vrow = lax.broadcasted_iota(jnp.int32, (KEYS, KV_W), 0)
    v_low = lax.broadcasted_iota(jnp.int32, (KEYS, KV_W), 1) < HEAD_DIM
    o_low = lax.broadcasted_iota(jnp.int32, (WINDOW, KV_W), 1) < HEAD_DIM
    one = jnp.ones((KEYS, KV_W), BF16)

    def blocks():
        units = [(b, g) for b in range(nblk) for g in range(ATTN_KV_HEADS)]
        kbd, v_even, v_odd = [], [], []
        for b, g in units:
            ktg = kt[g * HEAD_DIM:(g + 1) * HEAD_DIM, b * WINDOW:b * WINDOW + KEYS]
            ktg = jnp.where(key_lane == 0, jnp.zeros_like(ktg), ktg)
            kbd.append(jnp.concatenate(
                [jnp.concatenate([ktg if m == n else zero_kt for m in range(ATTN_GROUP)], axis=1)
                 for n in range(ATTN_GROUP)], axis=0))
            window = slice(b * WINDOW, b * WINDOW + KEYS)
            in_low, in_high = (vv, vs) if g == 0 else (vs, vv)
            v_even.append(jnp.where(v_low, jnp.where(vrow == 0, jnp.zeros_like(one), in_low[window, :]), one))
            v_odd.append(jnp.where(v_low, one, jnp.where(vrow == 0, jnp.zeros_like(one), in_high[window, :])))
        yield
        scores = [jnp.dot(q[b * WINDOW:(b + 1) * WINDOW, g * GROUP_W:(g + 1) * GROUP_W], kb,
                          preferred_element_type=F32) + (bias_first[g] if b == 0 else bias_rest[g])
                  for (b, g), kb in zip(units, kbd)]
        yield
        probs = []
        for s in scores:
            parts = []
            for n in range(ATTN_GROUP):
                sn = s[:, n * KEYS:(n + 1) * KEYS]
                parts.append(jnp.exp(sn - jnp.max(sn, axis=-1, keepdims=True)).astype(BF16))
            probs.append(parts)
        yield
        outs = [[jnp.dot(pn, even if n % 2 == 0 else odd, preferred_element_type=F32) for n, pn in enumerate(parts)]
                for parts, even, odd in zip(probs, v_even, v_odd)]
        yield
        for (b, g), heads in zip(units, outs):
            normed = [od * pltpu.roll(1.0 / od, HEAD_DIM, 1) for od in heads]
            pairs = [jnp.where(o_low, normed[n], normed[n + 1]) for n in range(0, ATTN_GROUP, 2)]
            o_ref[b * WINDOW:(b + 1) * WINDOW, g * GROUP_W:(g + 1) * GROUP_W] = (
                jnp.concatenate(pairs, axis=1).astype(o_ref.dtype))

    _interleave(blocks())


def _attention(sinks, attn_p, cos, sin, layer):
    s = attn_p.shape[0]
    nblk = min(ATTN_BLOCKS, s // WINDOW)
    tq = nblk * WINDOW
    row = lambda i: (i, 0)
    return pl.pallas_call(
        functools.partial(_attn_body, nblk=nblk),
        out_shape=jax.ShapeDtypeStruct((s, ATTN_W), BF16),
        grid=(s // tq,),
        in_specs=[
            pl.BlockSpec(memory_space=pltpu.SMEM),
            pl.BlockSpec((tq, ATTN_COLS), row),
            pl.BlockSpec((tq, 2 * HEAD_DIM), row),
            pl.BlockSpec((tq, 2 * HEAD_DIM), row),
        ],
        out_specs=pl.BlockSpec((tq, ATTN_W), row),
        scratch_shapes=[pltpu.VMEM((KV_W, WINDOW), BF16), pltpu.VMEM((2, WINDOW, KV_W), BF16)],
        compiler_params=_params(("arbitrary",)),
        name="swa_attention",
    )(sinks[layer], attn_p, cos, sin)


def _mlstm_body(p_ref, gates_ref, bias_ref, norm_ref, o_ref, c_ref, n_ref, *, tb):
    step = pl.program_id(0)

    @pl.when(step == 0)
    def _():
        c_ref[...] = jnp.zeros_like(c_ref)
        n_ref[...] = jnp.zeros_like(n_ref)

    def gate_lanes(r0, n):
        pre = GATE_CAP * jnp.tanh((gates_ref[r0:r0 + n, :] + bias_ref[...]) / GATE_CAP)
        yield
        logsig = jnp.minimum(pre, 0.0) - jnp.log(1.0 + jnp.exp(-jnp.abs(pre)))
        yield
        g_cum = _cumsum_rows(_chunk_tri(n), logsig)
        yield
        gate_col = lax.broadcasted_iota(jnp.int32, (n, MLSTM_GATES), 1)
        lanes = _split_dot(jnp.where(gate_col < MLSTM_HEADS, pre, g_cum), _gate_spread())
        return lanes[:, 0:MLSTM_W], lanes[:, MLSTM_W:2 * MLSTM_W]

    head_ones = _head_ones(MLSTM_W)

    hshift = HEAD_DIM.bit_length() - 1
    lane = lax.broadcasted_iota(jnp.int32, (CHUNK, MLSTM_W), 1)
    time = lax.broadcasted_iota(jnp.int32, (CHUNK, MLSTM_W), 0)
    key_of_lane = lane & (HEAD_DIM - 1)
    causal = key_of_lane <= time
    eye = key_of_lane == time
    brow, bcol = _tri_masks(MLSTM_W)
    same_head = (brow >> hshift) == (bcol >> hshift)
    head_block = same_head.astype(BF16)
    stack = lambda x: jnp.where(same_head, jnp.concatenate([x] * MLSTM_HEADS, axis=0), 0.0)

    def to_row(lane_bcast):
        return jnp.sum(jnp.where(eye, lane_bcast, 0.0), axis=0, keepdims=True)

    state = dict(c=c_ref[...], n=n_ref[0:1, :])

    def chunks(r0, n, i_all, g_all):
        chunk_starts = range(0, n, CHUNK)
        local = lambda x: [x[c0:c0 + CHUNK, :] for c0 in chunk_starts]
        block = lambda lo, hi: [p_ref[r0 + c0:r0 + c0 + CHUNK, lo:hi] for c0 in chunk_starts]
        q, k, v = block(0, MLSTM_W), block(MLSTM_W, 2 * MLSTM_W), block(2 * MLSTM_W, 3 * MLSTM_W)
        qk = [_mm_nt(a, stack(b)) for a, b in zip(q, k)]
        yield
        g_chunk, i_chunk = local(g_all), local(i_all)
        g_tot = [g[CHUNK - 1:CHUNK, :] for g in g_chunk]
        w_in = [jnp.exp(t - g + i) for t, g, i in zip(g_tot, g_chunk, i_chunk)]
        e_tot = [jnp.exp(t) for t in g_tot]
        e_g = [jnp.exp(g) for g in g_chunk]
        n_loc = [jnp.sum(w * x, axis=0, keepdims=True) for w, x in zip(w_in, k)]
        yield
        decay = [jnp.exp(jnp.where(causal, g - to_row(g) + to_row(i), NEG_INF)) for g, i in zip(g_chunk, i_chunk)]
        s_mat = [x * d for x, d in zip(qk, decay)]
        yield
        num_den = [_mm(s, jnp.concatenate([stack(x).astype(BF16), head_block], axis=1))
                   for s, x in zip(s_mat, v)]
        yield
        c_loc = [jnp.where(same_head, _mm_tn(w * x, b), 0.0) for w, x, b in zip(w_in, v, k)]
        yield
        c_in, n_in = [], []
        for m in range(len(chunk_starts)):
            c_in.append(state["c"])
            n_in.append(state["n"])
            state["c"] = e_tot[m] * state["c"] + c_loc[m]
            state["n"] = e_tot[m] * state["n"] + n_loc[m]
        inter = [_mm_nt(a, jnp.concatenate([c, stack(jnp.broadcast_to(x, (CHUNK, MLSTM_W)))], axis=0))
                 for a, c, x in zip(q, c_in, n_in)]
        yield
        for c0, nd, e, x in zip(chunk_starts, num_den, e_g, inter):
            hid = ((nd[:, :MLSTM_W] + e * x[:, :MLSTM_W])
                   / jnp.maximum(jnp.abs(nd[:, MLSTM_W:] + e * x[:, MLSTM_W:]), 1.0))
            mean_sq = _split_dot(hid * hid, head_ones, terms=2) * (1.0 / HEAD_DIM)
            hid = hid * lax.rsqrt(mean_sq + NORM_EPS) * norm_ref[...]
            rows = slice(r0 + c0, r0 + c0 + CHUNK)
            o_ref[rows, :] = (jax.nn.sigmoid(p_ref[rows, 3 * MLSTM_W:4 * MLSTM_W]) * hid).astype(o_ref.dtype)
            yield

    size = tb if tb < 2 * MLSTM_PART else MLSTM_PART
    starts = list(range(0, tb, size))
    gates = _interleave(gate_lanes(starts[0], size))[0]
    for n, r0 in enumerate(starts):
        traces = [chunks(r0, size, *gates)]
        if n + 1 < len(starts):
            traces.append(gate_lanes(starts[n + 1], size))
        results = _interleave(*traces)
        gates = results[-1]
    c_ref[...] = state["c"]
    n_ref[0:1, :] = state["n"]


def _mlstm(mlstm_p, gates, bias, norm, layer, tb):
    s = mlstm_p.shape[0]
    row = lambda i: (i, 0)
    return pl.pallas_call(
        functools.partial(_mlstm_body, tb=tb),
        out_shape=jax.ShapeDtypeStruct((s, MLSTM_W), BF16),
        grid=(s // tb,),
        in_specs=[
            pl.BlockSpec((tb, MLSTM_MAIN), row),
            pl.BlockSpec((tb, MLSTM_GATES), row),
            _layer_vec(MLSTM_GATES, layer),
            _layer_vec(MLSTM_W, layer),
        ],
        out_specs=pl.BlockSpec((tb, MLSTM_W), row),
        scratch_shapes=[
            pltpu.VMEM((MLSTM_W, MLSTM_W), F32),
            pltpu.VMEM((SUBLANES, MLSTM_W), F32),
        ],
        compiler_params=_params(("arbitrary",)),
        name="mlstm",
    )(mlstm_p, gates, bias, norm)


def _rwkv_chunk_terms(a_t, b_t, k_t, r_t, b_end, k_end, v):
    row, col = _tri_masks(CHUNK)
    incl = col <= row
    strict = col < row
    eye = (col == row).astype(F32)
    half = HEAD_DIM

    quad = [_mm_nt(jnp.concatenate([a, r], axis=0), jnp.concatenate([b, k], axis=0))
            for a, r, b, k in zip(a_t, r_t, b_t, k_t)]
    yield
    n_mat = [jnp.where(strict, x[:CHUNK, :CHUNK], 0.0) for x in quad]
    a_ak = [jnp.where(strict, x[:CHUNK, CHUNK:], 0.0) for x in quad]
    c_rb = [jnp.where(incl, x[CHUNK:, :CHUNK], 0.0) for x in quad]
    c_rk = [jnp.where(incl, x[CHUNK:, CHUNK:], 0.0) for x in quad]

    inv = [eye + n for n in n_mat]
    power = n_mat
    for _ in range(NEUMANN_STEPS):
        power = [_mm(x, x) for x in power]
        yield
        inv = [m + _mm(m, x) for m, x in zip(inv, power)]
        yield

    z = [_mm(x, y) for x, y in zip(a_ak, v)]
    yield
    w12 = [_mm(m, jnp.concatenate([a, y], axis=1)) for m, a, y in zip(inv, a_t, z)]
    yield
    cw = [_mm(c, w) for c, w in zip(c_rb, w12)]
    yield
    ckv = [_mm(c, y) for c, y in zip(c_rk, v)]
    yield
    gh = [_mm_tn(w, b) for w, b in zip(w12, b_end)]
    yield
    vk = [_mm_tn(y, k) for y, k in zip(v, k_end)]
    yield
    q_eff = [r + x[:, :half] for r, x in zip(r_t, cw)]
    y_loc = [x[:, half:] + y for x, y in zip(cw, ckv)]
    g_mat = [x[:half, :] for x in gh]
    h_mat = [x[half:, :] + y for x, y in zip(gh, vk)]
    return q_eff, y_loc, g_mat, h_mat


def _rwkv_body(p_ref, w0_ref, wup_ref, a0_ref, aup_ref, gup_ref, kk_ref, ka_ref, rk_ref,
               lnw_ref, lnb_ref, o_ref, s_ref, stage_ref, end_ref, terms_ref, fwd_ref, fwd_end_ref, *, tb):
    step = pl.program_id(0)

    @pl.when(step == 0)
    def _():
        s_ref[...] = jnp.zeros_like(s_ref)
        stage_ref[...] = jnp.zeros_like(stage_ref)
        end_ref[...] = jnp.zeros_like(end_ref)
        terms_ref[...] = jnp.zeros_like(terms_ref)
        fwd_ref[...] = jnp.zeros_like(fwd_ref)
        fwd_end_ref[...] = jnp.zeros_like(fwd_end_ref)

    refs = (p_ref, w0_ref, wup_ref, a0_ref, aup_ref, gup_ref, kk_ref, ka_ref, rk_ref,
            lnw_ref, lnb_ref, o_ref, s_ref, stage_ref, end_ref, terms_ref, fwd_ref, fwd_end_ref)
    for cur in (0, 1):
        pl.when(lax.rem(step, 2) == cur)(functools.partial(_rwkv_step, *refs, tb=tb, cur=cur, prv=1 - cur))


def _rwkv_step(p_ref, w0_ref, wup_ref, a0_ref, aup_ref, gup_ref, kk_ref, ka_ref, rk_ref,
               lnw_ref, lnb_ref, o_ref, s_ref, stage_ref, end_ref, terms_ref, fwd_ref, fwd_end_ref,
               *, tb, cur, prv):
    head_ones = _head_ones(RWKV_W)
    head_sum = lambda x: _split_dot(x, head_ones, terms=2)
    head_lanes = lambda h: slice(h * HEAD_DIM, (h + 1) * HEAD_DIM)
    chunk_starts = range(0, tb, CHUNK)
    units = [(n, c0, h) for n, c0 in enumerate(chunk_starts) for h in range(RWKV_HEADS)]

    unit_rows = lambda u: slice(u * CHUNK, (u + 1) * CHUNK)

    def terms_previous():
        staged = {name: stage_ref[prv, n] for n, name in enumerate(RWKV_STAGED)}
        per_unit = {name: [staged[name][c0:c0 + CHUNK, head_lanes(h)] for _, c0, h in units]
                    for name in RWKV_STAGED[:7]}
        terms = yield from _rwkv_chunk_terms(**per_unit)
        for kind, per_unit_values in enumerate(terms):
            for u, value in enumerate(per_unit_values):
                terms_ref[prv, kind, unit_rows(u), :] = value
            yield
        for n, name in enumerate(RWKV_FORWARDED):
            fwd_ref[prv, n] = staged[name]
        fwd_end_ref[prv] = end_ref[prv]

    def finish_older():
        ends = fwd_end_ref[cur]
        state = [s_ref[h] for h in range(RWKV_HEADS)]
        ys = []
        for u, (n, c0, h) in enumerate(units):
            q_eff, y_loc, g_mat, h_mat = (terms_ref[cur, kind, unit_rows(u), :] for kind in range(4))
            ys.append(_mm_nt(q_eff, state[h]) + y_loc)
            state[h] = state[h] * ends[n:n + 1, head_lanes(h)] + _mm(state[h], g_mat) + h_mat
            if h == RWKV_HEADS - 1:
                yield
        for h in range(RWKV_HEADS):
            s_ref[h] = state[h]
        y_all = jnp.concatenate([jnp.concatenate(ys[n:n + RWKV_HEADS], axis=1)
                                 for n in range(0, len(ys), RWKV_HEADS)], axis=0)
        centred = y_all - head_sum(y_all) * (1.0 / HEAD_DIM)
        yield
        var = head_sum(centred * centred) * (1.0 / HEAD_DIM)
        normed = centred * lax.rsqrt(var + RWKV_GN_EPS) * lnw_ref[...] + lnb_ref[...]
        bonus, v_all, gate = (fwd_ref[cur, n] for n in range(len(RWKV_FORWARDED)))
        o_ref[...] = ((normed + bonus * v_all) * gate).astype(o_ref.dtype)

    def stage_current():
        for r0 in range(0, tb, RWKV_SLAB):
            rows = slice(r0, r0 + RWKV_SLAB)

            def stage(name, value):
                stage_ref[cur, RWKV_STAGED.index(name), rows, :] = value

            u = p_ref[rows, :]
            yield
            r_all = u[:, 0:RWKV_W]
            k_raw = u[:, RWKV_W:2 * RWKV_W]
            v_all = u[:, 2 * RWKV_W:3 * RWKV_W]
            x_w = u[:, 3 * RWKV_W:3 * RWKV_W + RWKV_W_RANK]
            x_a = u[:, 3 * RWKV_W + RWKV_W_RANK:3 * RWKV_W + RWKV_W_RANK + RWKV_A_RANK]
            x_g = u[:, 3 * RWKV_W + RWKV_W_RANK + RWKV_A_RANK:RWKV_COLS]
            z = w0_ref[...] + _mm(jnp.tanh(x_w), wup_ref[...])
            ld_all = -jnp.exp(-0.5) * jax.nn.sigmoid(z)
            a_all = jax.nn.sigmoid(a0_ref[...] + _mm(x_a, aup_ref[...]))
            yield
            kk_all = k_raw * kk_ref[...]
            k_all = k_raw * (1.0 + (a_all - 1.0) * ka_ref[...])
            kk_all = kk_all * lax.rsqrt(jnp.maximum(head_sum(kk_all * kk_all), 1e-24))
            a_vec = -kk_all
            b_vec = kk_all * a_all
            yield
            lp_all = _cumsum_rows(_chunk_tri(RWKV_SLAB), ld_all)
            stage("v", v_all)
            stage("bonus", head_sum(r_all * k_all * rk_ref[...]))
            stage("gate", _mm(jax.nn.sigmoid(x_g), gup_ref[...]))
            yield
            starts = range(0, RWKV_SLAB, CHUNK)
            lp_last = jnp.concatenate(
                [jnp.broadcast_to(lp_all[c0 + CHUNK - 1:c0 + CHUNK, :], (CHUNK, RWKV_W)) for c0 in starts], axis=0)
            grow = jnp.exp(-lp_all)
            to_end = jnp.exp(lp_last - lp_all)
            stage("a_t", a_vec * jnp.exp(lp_all - ld_all))
            stage("b_t", b_vec * grow)
            stage("k_t", k_all * grow)
            yield
            stage("r_t", r_all * jnp.exp(lp_all))
            stage("b_end", b_vec * to_end)
            stage("k_end", k_all * to_end)
            for c0 in starts:
                n = (r0 + c0) // CHUNK
                end_ref[cur, n:n + 1, :] = jnp.exp(lp_all[c0 + CHUNK - 1:c0 + CHUNK, :])
            yield

    _interleave(terms_previous(), finish_older(), stage_current())


def _rwkv(rwkv_p, w0, w_up, a0, a_up, g_up, k_k, k_a, r_k, ln_w, ln_b, layer, tb):
    s = rwkv_p.shape[0]
    blocks = s // tb
    lay3 = lambda i: (layer, 0, 0)
    vec = lambda width: _layer_vec(width, layer)
    return pl.pallas_call(
        functools.partial(_rwkv_body, tb=tb),
        out_shape=jax.ShapeDtypeStruct((s, RWKV_W), BF16),
        grid=(blocks + 2,),
        in_specs=[
            pl.BlockSpec((tb, RWKV_COLS), lambda i: (jnp.minimum(i, blocks - 1), 0)),
            vec(RWKV_W),
            pl.BlockSpec((None, RWKV_W_RANK, RWKV_W), lay3),
            vec(RWKV_W),
            pl.BlockSpec((None, RWKV_A_RANK, RWKV_W), lay3),
            pl.BlockSpec((None, RWKV_G_RANK, RWKV_W), lay3),
            vec(RWKV_W), vec(RWKV_W), vec(RWKV_W), vec(RWKV_W), vec(RWKV_W),
        ],
        out_specs=pl.BlockSpec((tb, RWKV_W), lambda i: (jnp.maximum(i - 2, 0), 0)),
        scratch_shapes=[
            pltpu.VMEM((RWKV_HEADS, HEAD_DIM, HEAD_DIM), F32),
            pltpu.VMEM((2, len(RWKV_STAGED), tb, RWKV_W), F32),
            pltpu.VMEM((2, tb // CHUNK, RWKV_W), F32),
            pltpu.VMEM((2, 4, (tb // CHUNK) * RWKV_HEADS * CHUNK, HEAD_DIM), F32),
            pltpu.VMEM((2, len(RWKV_FORWARDED), tb, RWKV_W), F32),
            pltpu.VMEM((2, tb // CHUNK, RWKV_W), F32),
        ],
        compiler_params=_params(("arbitrary",)),
        name="rwkv7",
    )(rwkv_p, w0, w_up, a0, a_up, g_up, k_k, k_a, r_k, ln_w, ln_b)


def _tiles(s):
    return min(512, s), min(512, s), min(2048, s)


def kernel(x, p, positions, ln_ffn1_pre, ln_ffn1_post, w_ffn1_in, w_ffn1_out, ln_mix_pre, w_in, attn_sinks, mlstm_conv, mlstm_i_bias, mlstm_f_bias, mlstm_norm, rwkv_mu, rwkv_w0, rwkv_w_up, rwkv_a0, rwkv_a_up, rwkv_g_up, rwkv_k_k, rwkv_k_a, rwkv_r_k, rwkv_ln_w, rwkv_ln_b, w_out, ln_mix_post, ln_ffn2_pre, ln_ffn2_post, w_ffn2_in, w_ffn2_out, ln_ple_pre, w_ple_gate, w_ple_proj, ln_ple_post):
    batch, seq, _ = x.shape
    assert batch == 1 and seq % WINDOW == 0
    depth = w_in.shape[0]
    tm, tb, tb_mlstm = _tiles(seq)

    bf = lambda w: w.astype(BF16)
    w_ffn1_in, w_ffn1_out, w_ffn2_in, w_ffn2_out = map(bf, (w_ffn1_in, w_ffn1_out, w_ffn2_in, w_ffn2_out))
    w_out, w_ple_gate, w_ple_proj = map(bf, (w_out, w_ple_gate, w_ple_proj))
    rwkv_w_up, rwkv_a_up, rwkv_g_up = map(bf, (rwkv_w_up, rwkv_a_up, rwkv_g_up))
    m0 = ATTN_COLS
    g0 = m0 + MLSTM_MAIN
    r0 = g0 + MLSTM_GATES
    w_proj = bf(jnp.concatenate([w_in[:, :, m0:g0], w_in[:, :, r0:], w_in[:, :, :m0], w_in[:, :, g0:r0]], axis=-1))
    vec = lambda a: a.reshape(depth, 1, -1)
    gate_bias = vec(jnp.concatenate([mlstm_i_bias, mlstm_f_bias], axis=-1))
    (ln_ffn1_pre, ln_ffn1_post, ln_mix_pre, ln_mix_post, ln_ffn2_pre, ln_ffn2_post, ln_ple_pre,
     ln_ple_post, mlstm_norm, rwkv_mu, rwkv_w0, rwkv_a0, rwkv_k_k, rwkv_k_a, rwkv_r_k, rwkv_ln_w,
     rwkv_ln_b) = map(vec, (
         ln_ffn1_pre, ln_ffn1_post, ln_mix_pre, ln_mix_post, ln_ffn2_pre, ln_ffn2_post, ln_ple_pre,
         ln_ple_post, mlstm_norm, rwkv_mu, rwkv_w0, rwkv_a0, rwkv_k_k, rwkv_k_a, rwkv_r_k, rwkv_ln_w,
         rwkv_ln_b))

    cos, sin = _rope_tables(positions.reshape(seq, 1), tm)
    xs = x.reshape(seq, D_MODEL)
    for l in range(depth):
        xs, mlstm_p, rwkv_p, attn_p, gates = _pre_mix(
            xs, ln_ffn1_pre, ln_ffn1_post, w_ffn1_in, w_ffn1_out, ln_mix_pre, w_proj, mlstm_conv, rwkv_mu, l, tm)
        y_attn = _attention(attn_sinks, attn_p, cos, sin, l)
        y_mlstm = _mlstm(mlstm_p, gates, gate_bias, mlstm_norm, l, tb_mlstm)
        y_rwkv = _rwkv(rwkv_p, rwkv_w0, rwkv_w_up, rwkv_a0, rwkv_a_up, rwkv_g_up,
                       rwkv_k_k, rwkv_k_a, rwkv_r_k, rwkv_ln_w, rwkv_ln_b, l, tb)
        xs = _post_mix(y_attn, y_mlstm, y_rwkv, xs, p.reshape(depth, seq, D_PLE), w_out, ln_mix_post,
                       ln_ffn2_pre, ln_ffn2_post, w_ffn2_in, w_ffn2_out, ln_ple_pre, ln_ple_post,
                       w_ple_gate, w_ple_proj, l, tm)
    return xs.reshape(batch, seq, D_MODEL)
```

```python
import functools

import jax
import jax.numpy as jnp
from jax import lax
from jax.experimental import pallas as pl
from jax.experimental.pallas import tpu as pltpu

F32 = jnp.float32
BF16 = jnp.bfloat16

D_MODEL = 1024
HEAD_DIM = 64
D_FF = 2816
D_PLE = 256
ATTN_Q_HEADS = 8
ATTN_KV_HEADS = 2
ATTN_GROUP = ATTN_Q_HEADS // ATTN_KV_HEADS
WINDOW = 128
ROPE_THETA = 500000.0
ROPE_DIM = HEAD_DIM // 4
ROPE_HALF = ROPE_DIM // 2
MLSTM_HEADS = 4
MLSTM_CONV = 4
GATE_CAP = 15.0
RWKV_HEADS = 4
RWKV_W_RANK = 64
RWKV_A_RANK = 64
RWKV_G_RANK = 128
RWKV_GN_EPS = 64e-5
NORM_EPS = 1e-6
NEG_INF = -1e30

ATTN_W = ATTN_Q_HEADS * HEAD_DIM
KV_W = ATTN_KV_HEADS * HEAD_DIM
MLSTM_W = MLSTM_HEADS * HEAD_DIM
RWKV_W = RWKV_HEADS * HEAD_DIM
ATTN_COLS = ATTN_W + 2 * KV_W
MLSTM_MAIN = 4 * MLSTM_W
MLSTM_GATES = 2 * MLSTM_HEADS
RWKV_COLS = 3 * RWKV_W + RWKV_W_RANK + RWKV_A_RANK + RWKV_G_RANK

CHUNK = 64
NEUMANN_STEPS = 5
PRE_MIX_PART = 128
POST_MIX_PART = 256
MLSTM_PART = 256
RWKV_SLAB = 128
RWKV_FORWARDED = ("bonus", "v", "gate")
RWKV_STAGED = ("a_t", "b_t", "k_t", "r_t", "b_end", "k_end", "v", "bonus", "gate")
MXU_TILE = 256
FF_SPLITS = (0, 4 * MXU_TILE, 8 * MXU_TILE, D_FF)
SUBLANES = 8
V7X_VMEM_LIMIT = 56 * 1024 * 1024


def _params(semantics):
    return pltpu.CompilerParams(dimension_semantics=semantics, vmem_limit_bytes=V7X_VMEM_LIMIT)


def _resident(block_shape, index_map):
    return pl.BlockSpec(block_shape, index_map, pipeline_mode=pl.Buffered(1))


def _layer_vec(width, layer):
    return pl.BlockSpec((None, 1, width), lambda i: (layer, 0, 0))


def _rms(x, gain):
    return x * lax.rsqrt(jnp.mean(x * x, axis=-1, keepdims=True) + NORM_EPS) * gain


def _mm(a, b):
    return jnp.dot(a.astype(BF16), b.astype(BF16), preferred_element_type=F32)


def _mm_nt(a, b):
    return lax.dot_general(a.astype(BF16), b.astype(BF16), (((1,), (1,)), ((), ())),
                           preferred_element_type=F32)


def _mm_tn(a, b):
    return lax.dot_general(a.astype(BF16), b.astype(BF16), (((0,), (0,)), ((), ())),
                           preferred_element_type=F32)


def _bf16_terms(x, terms):
    out, rest = [], x
    for n in range(terms):
        part = rest.astype(BF16)
        out.append(part)
        if n + 1 < terms:
            rest = rest - part.astype(F32)
    return out


def _cumsum_rows(tri, x):
    return sum(jnp.dot(tri, term, preferred_element_type=F32) for term in _bf16_terms(x, 3))


def _split_dot(x, ones, terms=3):
    return sum(jnp.dot(term, ones, preferred_element_type=F32) for term in _bf16_terms(x, terms))


def _head_ones(width):
    row, col = _tri_masks(width)
    shift = HEAD_DIM.bit_length() - 1
    return ((row >> shift) == (col >> shift)).astype(BF16)


def _gate_spread():
    lanes = MLSTM_GATES * HEAD_DIM
    row = lax.broadcasted_iota(jnp.int32, (MLSTM_GATES, lanes), 0)
    col = lax.broadcasted_iota(jnp.int32, (MLSTM_GATES, lanes), 1)
    return (row == (col >> (HEAD_DIM.bit_length() - 1))).astype(BF16)


def _interleave(*gens):
    results = [None] * len(gens)
    live = dict(enumerate(gens))
    while live:
        for n in list(live):
            try:
                next(live[n])
            except StopIteration as stop:
                results[n] = stop.value
                del live[n]
    return results


def _tri_masks(n):
    row = lax.broadcasted_iota(jnp.int32, (n, n), 0)
    col = lax.broadcasted_iota(jnp.int32, (n, n), 1)
    return row, col


def _chunk_tri(n):
    row, col = _tri_masks(n)
    shift = CHUNK.bit_length() - 1
    return (((row >> shift) == (col >> shift)) & (col <= row)).astype(BF16)


def _macaron_half_step(x, gpre_ref, gpost_ref, win_ref, wout_ref):
    xn = _rms(x, gpre_ref[...]).astype(BF16)
    yield
    splits = list(zip(FF_SPLITS[:-1], FF_SPLITS[1:]))
    gate_up = []
    acc = None
    for n in range(len(splits) + 1):
        if n < len(splits):
            lo, hi = splits[n]
            gate_up.append((jnp.dot(xn, win_ref[:, lo:hi], preferred_element_type=F32),
                            jnp.dot(xn, win_ref[:, D_FF + lo:D_FF + hi], preferred_element_type=F32)))
        if n > 0:
            lo, hi = splits[n - 1]
            gate, up = gate_up[n - 1]
            act = (gate * jax.nn.sigmoid(gate) * up).astype(BF16)
            yield
            part = jnp.dot(act, wout_ref[lo:hi, :], preferred_element_type=F32)
            acc = part if acc is None else acc + part
        yield
    return x + 0.5 * _rms(acc, gpost_ref[...])


def _staggered_row_parts(rows, size, tile):
    if rows < 2 * size:
        return _interleave(tile(0, rows))
    parts = rows // size

    def delayed(gen, phases):
        for _ in range(phases):
            yield
        yield from gen

    return _interleave(*[delayed(tile(n * size, size), n) for n in range(parts)])


def _pre_mix_body(x_ref, gpre_ref, gpost_ref, win_ref, wout_ref, gmix_ref, wproj_ref, conv_ref, mu_ref,
                  x_out_ref, om_ref, or_ref, oa_ref, og_ref, qk_pad_ref, u_pad_ref):
    @pl.when(pl.program_id(0) == 0)
    def _():
        qk_pad_ref[0:SUBLANES, :] = jnp.zeros((SUBLANES, 2 * MLSTM_W), F32)
        u_pad_ref[0:SUBLANES, :] = jnp.zeros((SUBLANES, RWKV_COLS), F32)

    def tile(r0, n):
        rows = slice(r0, r0 + n)
        x = yield from _macaron_half_step(x_ref[rows, :], gpre_ref, gpost_ref, win_ref, wout_ref)
        x_out_ref[rows, :] = x
        h = _rms(x, gmix_ref[...]).astype(BF16)
        yield
        starts = [sum(PROJ_WIDTHS[:g]) for g in range(len(PROJ_WIDTHS))]
        proj = [jnp.dot(h, wproj_ref[:, lo:lo + w], preferred_element_type=F32) for lo, w in zip(starts, PROJ_WIDTHS)]
        p_mlstm, p_rwkv, p_attn, p_gates = proj
        oa_ref[rows, :] = p_attn
        og_ref[rows, :] = p_gates
        yield
        qk_pad_ref[SUBLANES:SUBLANES + n, :] = p_mlstm[:, 0:2 * MLSTM_W]
        conv = None
        for tap in range(MLSTM_CONV):
            shifted = qk_pad_ref[pl.ds(SUBLANES - (MLSTM_CONV - 1) + tap, n), :]
            term = shifted * conv_ref[tap:tap + 1, :]
            conv = term if conv is None else conv + term
        qk_pad_ref[0:SUBLANES, :] = qk_pad_ref[n:n + SUBLANES, :]
        qk = conv * jax.nn.sigmoid(conv)
        om_ref[rows, 0:MLSTM_W] = qk[:, 0:MLSTM_W] * (HEAD_DIM ** -0.5)
        om_ref[rows, MLSTM_W:2 * MLSTM_W] = qk[:, MLSTM_W:2 * MLSTM_W]
        om_ref[rows, 2 * MLSTM_W:] = p_mlstm[:, 2 * MLSTM_W:]
        yield
        u_pad_ref[SUBLANES:SUBLANES + n, :] = p_rwkv
        prev = u_pad_ref[pl.ds(SUBLANES - 1, n), :]
        u_pad_ref[0:SUBLANES, :] = u_pad_ref[n:n + SUBLANES, :]
        or_ref[rows, :] = p_rwkv + (prev - p_rwkv) * mu_ref[...]

    _staggered_row_parts(x_ref.shape[0], PRE_MIX_PART, tile)


PROJ_WIDTHS = (MLSTM_MAIN, RWKV_COLS, ATTN_COLS, MLSTM_GATES)


def _pre_mix(x, gpre, gpost, w_ffn_in, w_ffn_out, gmix, w_proj, conv_w, rwkv_mu, layer, tm):
    s = x.shape[0]
    row = lambda i: (i, 0)
    lay3 = lambda i: (layer, 0, 0)
    widths = PROJ_WIDTHS
    part = tm if tm < 2 * PRE_MIX_PART else PRE_MIX_PART
    return pl.pallas_call(
        _pre_mix_body,
        out_shape=[jax.ShapeDtypeStruct((s, D_MODEL), F32)] + [jax.ShapeDtypeStruct((s, w), F32) for w in widths],
        grid=(s // tm,),
        in_specs=[
            pl.BlockSpec((tm, D_MODEL), row),
            _layer_vec(D_MODEL, layer),
            _layer_vec(D_MODEL, layer),
            _resident((None, D_MODEL, 2 * D_FF), lay3),
            _resident((None, D_FF, D_MODEL), lay3),
            _layer_vec(D_MODEL, layer),
            _resident((None, D_MODEL, sum(widths)), lay3),
            pl.BlockSpec((None, MLSTM_CONV, 2 * MLSTM_W), lay3),
            _layer_vec(RWKV_COLS, layer),
        ],
        out_specs=[pl.BlockSpec((tm, D_MODEL), row)] + [pl.BlockSpec((tm, w), row) for w in widths],
        scratch_shapes=[
            pltpu.VMEM((part + SUBLANES, 2 * MLSTM_W), F32),
            pltpu.VMEM((part + SUBLANES, RWKV_COLS), F32),
        ],
        compiler_params=_params(("arbitrary",)),
        name="pre_mix",
    )(x, gpre, gpost, w_ffn_in, w_ffn_out, gmix, w_proj, conv_w, rwkv_mu)


def _post_mix_body(ya_ref, ym_ref, yr_ref, x_ref, p_ref, wo_ref, gmix_ref, gpre_ref, gpost_ref, win_ref, wout_ref,
                   gple_pre_ref, gple_post_ref, wgate_ref, wemb_ref, o_ref):
    def tile(r0, n):
        rows = slice(r0, r0 + n)
        mix = jnp.dot(ya_ref[rows, :], wo_ref[0:ATTN_W, :], preferred_element_type=F32)
        mix += jnp.dot(ym_ref[rows, :], wo_ref[ATTN_W:ATTN_W + MLSTM_W, :], preferred_element_type=F32)
        mix += jnp.dot(yr_ref[rows, :], wo_ref[ATTN_W + MLSTM_W:, :], preferred_element_type=F32)
        emb = jnp.dot(p_ref[rows, :].astype(BF16), wemb_ref[...], preferred_element_type=F32)
        yield
        x = x_ref[rows, :] + _rms(mix, gmix_ref[...])
        x = yield from _macaron_half_step(x, gpre_ref, gpost_ref, win_ref, wout_ref)
        h = _rms(x, gple_pre_ref[...]).astype(BF16)
        yield
        gate = jax.nn.sigmoid(jnp.dot(h, wgate_ref[...], preferred_element_type=F32))
        yield
        o_ref[rows, :] = x + _rms(gate * emb, gple_post_ref[...])

    _staggered_row_parts(x_ref.shape[0], POST_MIX_PART, tile)


def _post_mix(ya, ym, yr, x, p, w_out, gmix, gpre, gpost, w_ffn_in, w_ffn_out, gple_pre, gple_post,
              w_gate, w_emb, layer, tm):
    s = x.shape[0]
    row = lambda i: (i, 0)
    lay3 = lambda i: (layer, 0, 0)
    vec = _layer_vec(D_MODEL, layer)
    return pl.pallas_call(
        _post_mix_body,
        out_shape=jax.ShapeDtypeStruct((s, D_MODEL), F32),
        grid=(s // tm,),
        in_specs=[
            pl.BlockSpec((tm, ATTN_W), row),
            pl.BlockSpec((tm, MLSTM_W), row),
            pl.BlockSpec((tm, RWKV_W), row),
            pl.BlockSpec((tm, D_MODEL), row),
            pl.BlockSpec((None, tm, D_PLE), lambda i: (layer, i, 0)),
            _resident((None, D_MODEL, D_MODEL), lay3),
            vec, vec, vec,
            _resident((None, D_MODEL, 2 * D_FF), lay3),
            _resident((None, D_FF, D_MODEL), lay3),
            vec, vec,
            _resident((None, D_MODEL, D_MODEL), lay3),
            _resident((None, D_PLE, D_MODEL), lay3),
        ],
        out_specs=pl.BlockSpec((tm, D_MODEL), row),
        compiler_params=_params(("parallel",)),
        name="post_mix",
    )(ya, ym, yr, x, p, w_out, gmix, gpre, gpost, w_ffn_in, w_ffn_out, gple_pre, gple_post, w_gate, w_emb)


def _rope_body(pos_ref, invf_ref, cos_ref, sin_ref):
    ang = pos_ref[...].astype(F32) * invf_ref[...]
    sin = jnp.sin(ang)
    dim = lax.broadcasted_iota(jnp.int32, ang.shape, 1) & (HEAD_DIM - 1)
    cos_ref[...] = jnp.cos(ang)
    sin_ref[...] = jnp.where(dim < ROPE_HALF, -sin, jnp.where(dim < ROPE_DIM, sin, 0.0))


def _rope_tables(positions, tm):
    s = positions.shape[0]
    lane = jnp.arange(2 * HEAD_DIM) % HEAD_DIM
    freq = ROPE_THETA ** (-jnp.arange(0, ROPE_DIM, 2, dtype=F32) / ROPE_DIM)
    invf = jnp.where(lane < ROPE_DIM, freq[lane % ROPE_HALF], 0.0).astype(F32)[None, :]
    row = lambda i: (i, 0)
    out = jax.ShapeDtypeStruct((s, 2 * HEAD_DIM), F32)
    return pl.pallas_call(
        _rope_body,
        out_shape=[out, out],
        grid=(s // tm,),
        in_specs=[pl.BlockSpec((tm, 1), row), pl.BlockSpec((1, 2 * HEAD_DIM), lambda i: (0, 0))],
        out_specs=[pl.BlockSpec((tm, 2 * HEAD_DIM), row)] * 2,
        compiler_params=_params(("parallel",)),
        name="rope_tables",
    )(positions, invf)


ATTN_BLOCKS = 8
GROUP_W = ATTN_GROUP * HEAD_DIM
KEYS = 2 * WINDOW


def _rope_partner():
    row, col = _tri_masks(2 * HEAD_DIM)
    dim = col & (HEAD_DIM - 1)
    first = (dim < ROPE_HALF) & (row == col + ROPE_HALF)
    second = (dim >= ROPE_HALF) & (dim < ROPE_DIM) & (row == col - ROPE_HALF)
    return (first | second).astype(BF16)


def _attn_body(sinks_ref, p_ref, cos_ref, sin_ref, o_ref, ktprev_ref, vprev_ref, *, nblk):
    step = pl.program_id(0)

    @pl.when(step == 0)
    def _():
        ktprev_ref[...] = jnp.zeros_like(ktprev_ref)
        vprev_ref[...] = jnp.zeros_like(vprev_ref)

    cos, sin = cos_ref[...], sin_ref[...]
    partner = _rope_partner()
    period = 2 * HEAD_DIM

    def rope(x):
        slabs = [x[:, c:c + period] for c in range(0, x.shape[1], period)]
        return jnp.concatenate(
            [s * cos + jnp.dot(s.astype(BF16), partner, preferred_element_type=F32) * sin for s in slabs], axis=1)

    q = (rope(p_ref[:, 0:ATTN_W]) * (HEAD_DIM ** -0.5)).astype(BF16)
    k_cur = rope(p_ref[:, ATTN_W:ATTN_W + KV_W])
    v_f32 = p_ref[:, ATTN_W + KV_W:ATTN_COLS]
    kt = jnp.concatenate([ktprev_ref[...], k_cur.T.astype(BF16)], axis=1)
    vv = jnp.concatenate([vprev_ref[0], v_f32.astype(BF16)], axis=0)
    vs = jnp.concatenate([vprev_ref[1], pltpu.roll(v_f32, HEAD_DIM, 1).astype(BF16)], axis=0)
    ktprev_ref[...] = kt[:, nblk * WINDOW:]
    vprev_ref[0] = vv[nblk * WINDOW:, :]
    vprev_ref[1] = vs[nblk * WINDOW:, :]

    t = lax.broadcasted_iota(jnp.int32, (WINDOW, ATTN_GROUP * KEYS), 0)
    lane = lax.broadcasted_iota(jnp.int32, (WINDOW, ATTN_GROUP * KEYS), 1)
    c = lane & (KEYS - 1)
    head = lane >> (KEYS.bit_length() - 1)
    cur_ok = (c >= WINDOW) & (c - WINDOW <= t)
    prev_ok = (c < WINDOW) & (c > t)

    def bias(g, has_prev):
        b = jnp.where(cur_ok | (prev_ok & has_prev), 0.0, NEG_INF)
        sink = jnp.zeros_like(b)
        for n in range(ATTN_GROUP):
            sink = jnp.where(head == n, sinks_ref[g * ATTN_GROUP + n], sink)
        return jnp.where(c == 0, sink, b)

    bias_first = [bias(g, step > 0) for g in range(ATTN_KV_HEADS)]
    bias_rest = [bias(g, True) for g in range(ATTN_KV_HEADS)]

    key_lane = lax.broadcasted_iota(jnp.int32, (HEAD_DIM, KEYS), 1)
    zero_kt = jnp.zeros((HEAD_DIM, KEYS), BF16)
    vrow = lax.broadcasted_iota(jnp.int32, (KEYS, KV_W), 0)
    v_low = lax.broadcasted_iota(jnp.int32, (KEYS, KV_W), 1) < HEAD_DIM
    o_low = lax.broadcasted_iota(jnp.int32, (WINDOW, KV_W), 1) < HEAD_DIM
    one = jnp.ones((KEYS, KV_W), BF16)

    def blocks():
        units = [(b, g) for b in range(nblk) for g in range(ATTN_KV_HEADS)]
        kbd, v_even, v_odd = [], [], []
        for b, g in units:
            ktg = kt[g * HEAD_DIM:(g + 1) * HEAD_DIM, b * WINDOW:b * WINDOW + KEYS]
            ktg = jnp.where(key_lane == 0, jnp.zeros_like(ktg), ktg)
            kbd.append(jnp.concatenate(
                [jnp.concatenate([ktg if m == n else zero_kt for m in range(ATTN_GROUP)], axis=1)
                 for n in range(ATTN_GROUP)], axis=0))
            window = slice(b * WINDOW, b * WINDOW + KEYS)
            in_low, in_high = (vv, vs) if g == 0 else (vs, vv)
            v_even.append(jnp.where(v_low, jnp.where(vrow == 0, jnp.zeros_like(one), in_low[window, :]), one))
            v_odd.append(jnp.where(v_low, one, jnp.where(vrow == 0, jnp.zeros_like(one), in_high[window, :])))
        yield
        scores = [jnp.dot(q[b * WINDOW:(b + 1) * WINDOW, g * GROUP_W:(g + 1) * GROUP_W], kb,
                          preferred_element_type=F32) + (bias_first[g] if b == 0 else bias_rest[g])
                  for (b, g), kb in zip(units, kbd)]
        yield
        probs = []
        for s in scores:
            parts = []
            for n in range(ATTN_GROUP):
                sn = s[:, n * KEYS:(n + 1) * KEYS]
                parts.append(jnp.exp(sn - jnp.max(sn, axis=-1, keepdims=True)).astype(BF16))
            probs.append(parts)
        yield
        outs = [[jnp.dot(pn, even if n % 2 == 0 else odd, preferred_element_type=F32) for n, pn in enumerate(parts)]
                for parts, even, odd in zip(probs, v_even, v_odd)]
        yield
        for (b, g), heads in zip(units, outs):
            normed = [od * pltpu.roll(1.0 / od, HEAD_DIM, 1) for od in heads]
            pairs = [jnp.where(o_low, normed[n], normed[n + 1]) for n in range(0, ATTN_GROUP, 2)]
            o_ref[b * WINDOW:(b + 1) * WINDOW, g * GROUP_W:(g + 1) * GROUP_W] = (
                jnp.concatenate(pairs, axis=1).astype(o_ref.dtype))

    _interleave(blocks())


def _attention(sinks, attn_p, cos, sin, layer):
    s = attn_p.shape[0]
    nblk = min(ATTN_BLOCKS, s // WINDOW)
    tq = nblk * WINDOW
    row = lambda i: (i, 0)
    return pl.pallas_call(
        functools.partial(_attn_body, nblk=nblk),
        out_shape=jax.ShapeDtypeStruct((s, ATTN_W), BF16),
        grid=(s // tq,),
        in_specs=[
            pl.BlockSpec(memory_space=pltpu.SMEM),
            pl.BlockSpec((tq, ATTN_COLS), row),
            pl.BlockSpec((tq, 2 * HEAD_DIM), row),
            pl.BlockSpec((tq, 2 * HEAD_DIM), row),
        ],
        out_specs=pl.BlockSpec((tq, ATTN_W), row),
        scratch_shapes=[pltpu.VMEM((KV_W, WINDOW), BF16), pltpu.VMEM((2, WINDOW, KV_W), BF16)],
        compiler_params=_params(("arbitrary",)),
        name="swa_attention",
    )(sinks[layer], attn_p, cos, sin)


def _mlstm_body(p_ref, gates_ref, bias_ref, norm_ref, o_ref, c_ref, n_ref, *, tb):
    step = pl.program_id(0)

    @pl.when(step == 0)
    def _():
        c_ref[...] = jnp.zeros_like(c_ref)
        n_ref[...] = jnp.zeros_like(n_ref)

    def gate_lanes(r0, n):
        pre = GATE_CAP * jnp.tanh((gates_ref[r0:r0 + n, :] + bias_ref[...]) / GATE_CAP)
        yield
        logsig = jnp.minimum(pre, 0.0) - jnp.log(1.0 + jnp.exp(-jnp.abs(pre)))
        yield
        g_cum = _cumsum_rows(_chunk_tri(n), logsig)
        yield
        gate_col = lax.broadcasted_iota(jnp.int32, (n, MLSTM_GATES), 1)
        lanes = _split_dot(jnp.where(gate_col < MLSTM_HEADS, pre, g_cum), _gate_spread())
        return lanes[:, 0:MLSTM_W], lanes[:, MLSTM_W:2 * MLSTM_W]

    head_ones = _head_ones(MLSTM_W)

    hshift = HEAD_DIM.bit_length() - 1
    lane = lax.broadcasted_iota(jnp.int32, (CHUNK, MLSTM_W), 1)
    time = lax.broadcasted_iota(jnp.int32, (CHUNK, MLSTM_W), 0)
    key_of_lane = lane & (HEAD_DIM - 1)
    causal = key_of_lane <= time
    eye = key_of_lane == time
    brow, bcol = _tri_masks(MLSTM_W)
    same_head = (brow >> hshift) == (bcol >> hshift)
    head_block = same_head.astype(BF16)
    stack = lambda x: jnp.where(same_head, jnp.concatenate([x] * MLSTM_HEADS, axis=0), 0.0)

    def to_row(lane_bcast):
        return jnp.sum(jnp.where(eye, lane_bcast, 0.0), axis=0, keepdims=True)

    state = dict(c=c_ref[...], n=n_ref[0:1, :])

    def chunks(r0, n, i_all, g_all):
        chunk_starts = range(0, n, CHUNK)
        local = lambda x: [x[c0:c0 + CHUNK, :] for c0 in chunk_starts]
        block = lambda lo, hi: [p_ref[r0 + c0:r0 + c0 + CHUNK, lo:hi] for c0 in chunk_starts]
        q, k, v = block(0, MLSTM_W), block(MLSTM_W, 2 * MLSTM_W), block(2 * MLSTM_W, 3 * MLSTM_W)
        qk = [_mm_nt(a, stack(b)) for a, b in zip(q, k)]
        yield
        g_chunk, i_chunk = local(g_all), local(i_all)
        g_tot = [g[CHUNK - 1:CHUNK, :] for g in g_chunk]
        w_in = [jnp.exp(t - g + i) for t, g, i in zip(g_tot, g_chunk, i_chunk)]
        e_tot = [jnp.exp(t) for t in g_tot]
        e_g = [jnp.exp(g) for g in g_chunk]
        n_loc = [jnp.sum(w * x, axis=0, keepdims=True) for w, x in zip(w_in, k)]
        yield
        decay = [jnp.exp(jnp.where(causal, g - to_row(g) + to_row(i), NEG_INF)) for g, i in zip(g_chunk, i_chunk)]
        s_mat = [x * d for x, d in zip(qk, decay)]
        yield
        num_den = [_mm(s, jnp.concatenate([stack(x).astype(BF16), head_block], axis=1))
                   for s, x in zip(s_mat, v)]
        yield
        c_loc = [jnp.where(same_head, _mm_tn(w * x, b), 0.0) for w, x, b in zip(w_in, v, k)]
        yield
        c_in, n_in = [], []
        for m in range(len(chunk_starts)):
            c_in.append(state["c"])
            n_in.append(state["n"])
            state["c"] = e_tot[m] * state["c"] + c_loc[m]
            state["n"] = e_tot[m] * state["n"] + n_loc[m]
        inter = [_mm_nt(a, jnp.concatenate([c, stack(jnp.broadcast_to(x, (CHUNK, MLSTM_W)))], axis=0))
                 for a, c, x in zip(q, c_in, n_in)]
        yield
        for c0, nd, e, x in zip(chunk_starts, num_den, e_g, inter):
            hid = ((nd[:, :MLSTM_W] + e * x[:, :MLSTM_W])
                   / jnp.maximum(jnp.abs(nd[:, MLSTM_W:] + e * x[:, MLSTM_W:]), 1.0))
            mean_sq = _split_dot(hid * hid, head_ones, terms=2) * (1.0 / HEAD_DIM)
            hid = hid * lax.rsqrt(mean_sq + NORM_EPS) * norm_ref[...]
            rows = slice(r0 + c0, r0 + c0 + CHUNK)
            o_ref[rows, :] = (jax.nn.sigmoid(p_ref[rows, 3 * MLSTM_W:4 * MLSTM_W]) * hid).astype(o_ref.dtype)
            yield

    size = tb if tb < 2 * MLSTM_PART else MLSTM_PART
    starts = list(range(0, tb, size))
    gates = _interleave(gate_lanes(starts[0], size))[0]
    for n, r0 in enumerate(starts):
        traces = [chunks(r0, size, *gates)]
        if n + 1 < len(starts):
            traces.append(gate_lanes(starts[n + 1], size))
        results = _interleave(*traces)
        gates = results[-1]
    c_ref[...] = state["c"]
    n_ref[0:1, :] = state["n"]


def _mlstm(mlstm_p, gates, bias, norm, layer, tb):
    s = mlstm_p.shape[0]
    row = lambda i: (i, 0)
    return pl.pallas_call(
        functools.partial(_mlstm_body, tb=tb),
        out_shape=jax.ShapeDtypeStruct((s, MLSTM_W), BF16),
        grid=(s // tb,),
        in_specs=[
            pl.BlockSpec((tb, MLSTM_MAIN), row),
            pl.BlockSpec((tb, MLSTM_GATES), row),
            _layer_vec(MLSTM_GATES, layer),
            _layer_vec(MLSTM_W, layer),
        ],
        out_specs=pl.BlockSpec((tb, MLSTM_W), row),
        scratch_shapes=[
            pltpu.VMEM((MLSTM_W, MLSTM_W), F32),
            pltpu.VMEM((SUBLANES, MLSTM_W), F32),
        ],
        compiler_params=_params(("arbitrary",)),
        name="mlstm",
    )(mlstm_p, gates, bias, norm)


def _rwkv_chunk_terms(a_t, b_t, k_t, r_t, b_end, k_end, v):
    row, col = _tri_masks(CHUNK)
    incl = col <= row
    strict = col < row
    eye = (col == row).astype(F32)
    half = HEAD_DIM

    quad = [_mm_nt(jnp.concatenate([a, r], axis=0), jnp.concatenate([b, k], axis=0))
            for a, r, b, k in zip(a_t, r_t, b_t, k_t)]
    yield
    n_mat = [jnp.where(strict, x[:CHUNK, :CHUNK], 0.0) for x in quad]
    a_ak = [jnp.where(strict, x[:CHUNK, CHUNK:], 0.0) for x in quad]
    c_rb = [jnp.where(incl, x[CHUNK:, :CHUNK], 0.0) for x in quad]
    c_rk = [jnp.where(incl, x[CHUNK:, CHUNK:], 0.0) for x in quad]

    inv = [eye + n for n in n_mat]
    power = n_mat
    for _ in range(NEUMANN_STEPS):
        power = [_mm(x, x) for x in power]
        yield
        inv = [m + _mm(m, x) for m, x in zip(inv, power)]
        yield

    z = [_mm(x, y) for x, y in zip(a_ak, v)]
    yield
    w12 = [_mm(m, jnp.concatenate([a, y], axis=1)) for m, a, y in zip(inv, a_t, z)]
    yield
    cw = [_mm(c, w) for c, w in zip(c_rb, w12)]
    yield
    ckv = [_mm(c, y) for c, y in zip(c_rk, v)]
    yield
    gh = [_mm_tn(w, b) for w, b in zip(w12, b_end)]
    yield
    vk = [_mm_tn(y, k) for y, k in zip(v, k_end)]
    yield
    q_eff = [r + x[:, :half] for r, x in zip(r_t, cw)]
    y_loc = [x[:, half:] + y for x, y in zip(cw, ckv)]
    g_mat = [x[:half, :] for x in gh]
    h_mat = [x[half:, :] + y for x, y in zip(gh, vk)]
    return q_eff, y_loc, g_mat, h_mat


def _rwkv_body(p_ref, w0_ref, wup_ref, a0_ref, aup_ref, gup_ref, kk_ref, ka_ref, rk_ref,
               lnw_ref, lnb_ref, o_ref, s_ref, stage_ref, end_ref, terms_ref, fwd_ref, fwd_end_ref, *, tb):
    step = pl.program_id(0)

    @pl.when(step == 0)
    def _():
        s_ref[...] = jnp.zeros_like(s_ref)
        stage_ref[...] = jnp.zeros_like(stage_ref)
        end_ref[...] = jnp.zeros_like(end_ref)
        terms_ref[...] = jnp.zeros_like(terms_ref)
        fwd_ref[...] = jnp.zeros_like(fwd_ref)
        fwd_end_ref[...] = jnp.zeros_like(fwd_end_ref)

    refs = (p_ref, w0_ref, wup_ref, a0_ref, aup_ref, gup_ref, kk_ref, ka_ref, rk_ref,
            lnw_ref, lnb_ref, o_ref, s_ref, stage_ref, end_ref, terms_ref, fwd_ref, fwd_end_ref)
    for cur in (0, 1):
        pl.when(lax.rem(step, 2) == cur)(functools.partial(_rwkv_step, *refs, tb=tb, cur=cur, prv=1 - cur))


def _rwkv_step(p_ref, w0_ref, wup_ref, a0_ref, aup_ref, gup_ref, kk_ref, ka_ref, rk_ref,
               lnw_ref, lnb_ref, o_ref, s_ref, stage_ref, end_ref, terms_ref, fwd_ref, fwd_end_ref,
               *, tb, cur, prv):
    head_ones = _head_ones(RWKV_W)
    head_sum = lambda x: _split_dot(x, head_ones, terms=2)
    head_lanes = lambda h: slice(h * HEAD_DIM, (h + 1) * HEAD_DIM)
    chunk_starts = range(0, tb, CHUNK)
    units = [(n, c0, h) for n, c0 in enumerate(chunk_starts) for h in range(RWKV_HEADS)]

    unit_rows = lambda u: slice(u * CHUNK, (u + 1) * CHUNK)

    def terms_previous():
        staged = {name: stage_ref[prv, n] for n, name in enumerate(RWKV_STAGED)}
        per_unit = {name: [staged[name][c0:c0 + CHUNK, head_lanes(h)] for _, c0, h in units]
                    for name in RWKV_STAGED[:7]}
        terms = yield from _rwkv_chunk_terms(**per_unit)
        for kind, per_unit_values in enumerate(terms):
            for u, value in enumerate(per_unit_values):
                terms_ref[prv, kind, unit_rows(u), :] = value
            yield
        for n, name in enumerate(RWKV_FORWARDED):
            fwd_ref[prv, n] = staged[name]
        fwd_end_ref[prv] = end_ref[prv]

    def finish_older():
        ends = fwd_end_ref[cur]
        state = [s_ref[h] for h in range(RWKV_HEADS)]
        ys = []
        for u, (n, c0, h) in enumerate(units):
            q_eff, y_loc, g_mat, h_mat = (terms_ref[cur, kind, unit_rows(u), :] for kind in range(4))
            ys.append(_mm_nt(q_eff, state[h]) + y_loc)
            state[h] = state[h] * ends[n:n + 1, head_lanes(h)] + _mm(state[h], g_mat) + h_mat
            if h == RWKV_HEADS - 1:
                yield
        for h in range(RWKV_HEADS):
            s_ref[h] = state[h]
        y_all = jnp.concatenate([jnp.concatenate(ys[n:n + RWKV_HEADS], axis=1)
                                 for n in range(0, len(ys), RWKV_HEADS)], axis=0)
        centred = y_all - head_sum(y_all) * (1.0 / HEAD_DIM)
        yield
        var = head_sum(centred * centred) * (1.0 / HEAD_DIM)
        normed = centred * lax.rsqrt(var + RWKV_GN_EPS) * lnw_ref[...] + lnb_ref[...]
        bonus, v_all, gate = (fwd_ref[cur, n] for n in range(len(RWKV_FORWARDED)))
        o_ref[...] = ((normed + bonus * v_all) * gate).astype(o_ref.dtype)

    def stage_current():
        for r0 in range(0, tb, RWKV_SLAB):
            rows = slice(r0, r0 + RWKV_SLAB)

            def stage(name, value):
                stage_ref[cur, RWKV_STAGED.index(name), rows, :] = value

            u = p_ref[rows, :]
            yield
            r_all = u[:, 0:RWKV_W]
            k_raw = u[:, RWKV_W:2 * RWKV_W]
            v_all = u[:, 2 * RWKV_W:3 * RWKV_W]
            x_w = u[:, 3 * RWKV_W:3 * RWKV_W + RWKV_W_RANK]
            x_a = u[:, 3 * RWKV_W + RWKV_W_RANK:3 * RWKV_W + RWKV_W_RANK + RWKV_A_RANK]
            x_g = u[:, 3 * RWKV_W + RWKV_W_RANK + RWKV_A_RANK:RWKV_COLS]
            z = w0_ref[...] + _mm(jnp.tanh(x_w), wup_ref[...])
            ld_all = -jnp.exp(-0.5) * jax.nn.sigmoid(z)
            a_all = jax.nn.sigmoid(a0_ref[...] + _mm(x_a, aup_ref[...]))
            yield
            kk_all = k_raw * kk_ref[...]
            k_all = k_raw * (1.0 + (a_all - 1.0) * ka_ref[...])
            kk_all = kk_all * lax.rsqrt(jnp.maximum(head_sum(kk_all * kk_all), 1e-24))
            a_vec = -kk_all
            b_vec = kk_all * a_all
            yield
            lp_all = _cumsum_rows(_chunk_tri(RWKV_SLAB), ld_all)
            stage("v", v_all)
            stage("bonus", head_sum(r_all * k_all * rk_ref[...]))
            stage("gate", _mm(jax.nn.sigmoid(x_g), gup_ref[...]))
            yield
            starts = range(0, RWKV_SLAB, CHUNK)
            lp_last = jnp.concatenate(
                [jnp.broadcast_to(lp_all[c0 + CHUNK - 1:c0 + CHUNK, :], (CHUNK, RWKV_W)) for c0 in starts], axis=0)
            grow = jnp.exp(-lp_all)
            to_end = jnp.exp(lp_last - lp_all)
            stage("a_t", a_vec * jnp.exp(lp_all - ld_all))
            stage("b_t", b_vec * grow)
            stage("k_t", k_all * grow)
            yield
            stage("r_t", r_all * jnp.exp(lp_all))
            stage("b_end", b_vec * to_end)
            stage("k_end", k_all * to_end)
            for c0 in starts:
                n = (r0 + c0) // CHUNK
                end_ref[cur, n:n + 1, :] = jnp.exp(lp_all[c0 + CHUNK - 1:c0 + CHUNK, :])
            yield

    _interleave(terms_previous(), finish_older(), stage_current())


def _rwkv(rwkv_p, w0, w_up, a0, a_up, g_up, k_k, k_a, r_k, ln_w, ln_b, layer, tb):
    s = rwkv_p.shape[0]
    blocks = s // tb
    lay3 = lambda i: (layer, 0, 0)
    vec = lambda width: _layer_vec(width, layer)
    return pl.pallas_call(
        functools.partial(_rwkv_body, tb=tb),
        out_shape=jax.ShapeDtypeStruct((s, RWKV_W), BF16),
        grid=(blocks + 2,),
        in_specs=[
            pl.BlockSpec((tb, RWKV_COLS), lambda i: (jnp.minimum(i, blocks - 1), 0)),
            vec(RWKV_W),
            pl.BlockSpec((None, RWKV_W_RANK, RWKV_W), lay3),
            vec(RWKV_W),
            pl.BlockSpec((None, RWKV_A_RANK, RWKV_W), lay3),
            pl.BlockSpec((None, RWKV_G_RANK, RWKV_W), lay3),
            vec(RWKV_W), vec(RWKV_W), vec(RWKV_W), vec(RWKV_W), vec(RWKV_W),
        ],
        out_specs=pl.BlockSpec((tb, RWKV_W), lambda i: (jnp.maximum(i - 2, 0), 0)),
        scratch_shapes=[
            pltpu.VMEM((RWKV_HEADS, HEAD_DIM, HEAD_DIM), F32),
            pltpu.VMEM((2, len(RWKV_STAGED), tb, RWKV_W), F32),
            pltpu.VMEM((2, tb // CHUNK, RWKV_W), F32),
            pltpu.VMEM((2, 4, (tb // CHUNK) * RWKV_HEADS * CHUNK, HEAD_DIM), F32),
            pltpu.VMEM((2, len(RWKV_FORWARDED), tb, RWKV_W), F32),
            pltpu.VMEM((2, tb // CHUNK, RWKV_W), F32),
        ],
        compiler_params=_params(("arbitrary",)),
        name="rwkv7",
    )(rwkv_p, w0, w_up, a0, a_up, g_up, k_k, k_a, r_k, ln_w, ln_b)


def _tiles(s):
    return min(512, s), min(512, s), min(2048, s)


def kernel(x, p, positions, ln_ffn1_pre, ln_ffn1_post, w_ffn1_in, w_ffn1_out, ln_mix_pre, w_in, attn_sinks, mlstm_conv, mlstm_i_bias, mlstm_f_bias, mlstm_norm, rwkv_mu, rwkv_w0, rwkv_w_up, rwkv_a0, rwkv_a_up, rwkv_g_up, rwkv_k_k, rwkv_k_a, rwkv_r_k, rwkv_ln_w, rwkv_ln_b, w_out, ln_mix_post, ln_ffn2_pre, ln_ffn2_post, w_ffn2_in, w_ffn2_out, ln_ple_pre, w_ple_gate, w_ple_proj, ln_ple_post):
    batch, seq, _ = x.shape
    assert batch == 1 and seq % WINDOW == 0
    depth = w_in.shape[0]
    tm, tb, tb_mlstm = _tiles(seq)

    bf = lambda w: w.astype(BF16)
    w_ffn1_in, w_ffn1_out, w_ffn2_in, w_ffn2_out = map(bf, (w_ffn1_in, w_ffn1_out, w_ffn2_in, w_ffn2_out))
    w_out, w_ple_gate, w_ple_proj = map(bf, (w_out, w_ple_gate, w_ple_proj))
    rwkv_w_up, rwkv_a_up, rwkv_g_up = map(bf, (rwkv_w_up, rwkv_a_up, rwkv_g_up))
    m0 = ATTN_COLS
    g0 = m0 + MLSTM_MAIN
    r0 = g0 + MLSTM_GATES
    w_proj = bf(jnp.concatenate([w_in[:, :, m0:g0], w_in[:, :, r0:], w_in[:, :, :m0], w_in[:, :, g0:r0]], axis=-1))
    vec = lambda a: a.reshape(depth, 1, -1)
    gate_bias = vec(jnp.concatenate([mlstm_i_bias, mlstm_f_bias], axis=-1))
    (ln_ffn1_pre, ln_ffn1_post, ln_mix_pre, ln_mix_post, ln_ffn2_pre, ln_ffn2_post, ln_ple_pre,
     ln_ple_post, mlstm_norm, rwkv_mu, rwkv_w0, rwkv_a0, rwkv_k_k, rwkv_k_a, rwkv_r_k, rwkv_ln_w,
     rwkv_ln_b) = map(vec, (
         ln_ffn1_pre, ln_ffn1_post, ln_mix_pre, ln_mix_post, ln_ffn2_pre, ln_ffn2_post, ln_ple_pre,
         ln_ple_post, mlstm_norm, rwkv_mu, rwkv_w0, rwkv_a0, rwkv_k_k, rwkv_k_a, rwkv_r_k, rwkv_ln_w,
         rwkv_ln_b))

    cos, sin = _rope_tables(positions.reshape(seq, 1), tm)
    xs = x.reshape(seq, D_MODEL)
    for l in range(depth):
        xs, mlstm_p, rwkv_p, attn_p, gates = _pre_mix(
            xs, ln_ffn1_pre, ln_ffn1_post, w_ffn1_in, w_ffn1_out, ln_mix_pre, w_proj, mlstm_conv, rwkv_mu, l, tm)
        y_attn = _attention(attn_sinks, attn_p, cos, sin, l)
        y_mlstm = _mlstm(mlstm_p, gates, gate_bias, mlstm_norm, l, tb_mlstm)
        y_rwkv = _rwkv(rwkv_p, rwkv_w0, rwkv_w_up, rwkv_a0, rwkv_a_up, rwkv_g_up,
                       rwkv_k_k, rwkv_k_a, rwkv_r_k, rwkv_ln_w, rwkv_ln_b, l, tb)
        xs = _post_mix(y_attn, y_mlstm, y_rwkv, xs, p.reshape(depth, seq, D_PLE), w_out, ln_mix_post,
                       ln_ffn2_pre, ln_ffn2_post, w_ffn2_in, w_ffn2_out, ln_ple_pre, ln_ple_post,
                       w_ple_gate, w_ple_proj, l, tm)
    return xs.reshape(batch, seq, D_MODEL)
```

```python
import functools

import jax
import jax.numpy as jnp
from jax import lax
from jax.experimental import pallas as pl
from jax.experimental.pallas import tpu as pltpu

F32 = jnp.float32
BF16 = jnp.bfloat16

D_MODEL = 1024
HEAD_DIM = 64
D_FF = 2816
D_PLE = 256
ATTN_Q_HEADS = 8
ATTN_KV_HEADS = 2
ATTN_GROUP = ATTN_Q_HEADS // ATTN_KV_HEADS
WINDOW = 128
ROPE_THETA = 500000.0
ROPE_DIM = HEAD_DIM // 4
ROPE_HALF = ROPE_DIM // 2
MLSTM_HEADS = 4
MLSTM_CONV = 4
GATE_CAP = 15.0
RWKV_HEADS = 4
RWKV_W_RANK = 64
RWKV_A_RANK = 64
RWKV_G_RANK = 128
RWKV_GN_EPS = 64e-5
NORM_EPS = 1e-6
NEG_INF = -1e30

ATTN_W = ATTN_Q_HEADS * HEAD_DIM
KV_W = ATTN_KV_HEADS * HEAD_DIM
MLSTM_W = MLSTM_HEADS * HEAD_DIM
RWKV_W = RWKV_HEADS * HEAD_DIM
ATTN_COLS = ATTN_W + 2 * KV_W
MLSTM_MAIN = 4 * MLSTM_W
MLSTM_GATES = 2 * MLSTM_HEADS
RWKV_COLS = 3 * RWKV_W + RWKV_W_RANK + RWKV_A_RANK + RWKV_G_RANK

CHUNK = 64
NEUMANN_STEPS = 5
PRE_MIX_PART = 128
POST_MIX_PART = 256
MLSTM_PART = 256
RWKV_SLAB = 128
RWKV_FORWARDED = ("bonus", "v", "gate")
RWKV_STAGED = ("a_t", "b_t", "k_t", "r_t", "b_end", "k_end", "v", "bonus", "gate")
MXU_TILE = 256
FF_SPLITS = (0, D_FF)
SUBLANES = 8
V7X_VMEM_LIMIT = 56 * 1024 * 1024


def _params(semantics):
    return pltpu.CompilerParams(dimension_semantics=semantics, vmem_limit_bytes=V7X_VMEM_LIMIT)


def _resident(block_shape, index_map):
    return pl.BlockSpec(block_shape, index_map, pipeline_mode=pl.Buffered(1))


def _layer_vec(width, layer):
    return pl.BlockSpec((None, 1, width), lambda i: (layer, 0, 0))


def _rms(x, gain):
    return x * lax.rsqrt(jnp.mean(x * x, axis=-1, keepdims=True) + NORM_EPS) * gain


def _mm(a, b):
    return jnp.dot(a.astype(BF16), b.astype(BF16), preferred_element_type=F32)


def _mm_nt(a, b):
    return lax.dot_general(a.astype(BF16), b.astype(BF16), (((1,), (1,)), ((), ())),
                           preferred_element_type=F32)


def _mm_tn(a, b):
    return lax.dot_general(a.astype(BF16), b.astype(BF16), (((0,), (0,)), ((), ())),
                           preferred_element_type=F32)


def _bf16_terms(x, terms):
    out, rest = [], x
    for n in range(terms):
        part = rest.astype(BF16)
        out.append(part)
        if n + 1 < terms:
            rest = rest - part.astype(F32)
    return out


def _cumsum_rows(tri, x):
    return sum(jnp.dot(tri, term, preferred_element_type=F32) for term in _bf16_terms(x, 3))


def _split_dot(x, ones, terms=3):
    return sum(jnp.dot(term, ones, preferred_element_type=F32) for term in _bf16_terms(x, terms))


def _head_ones(width):
    row, col = _tri_masks(width)
    shift = HEAD_DIM.bit_length() - 1
    return ((row >> shift) == (col >> shift)).astype(BF16)


def _gate_spread():
    lanes = MLSTM_GATES * HEAD_DIM
    row = lax.broadcasted_iota(jnp.int32, (MLSTM_GATES, lanes), 0)
    col = lax.broadcasted_iota(jnp.int32, (MLSTM_GATES, lanes), 1)
    return (row == (col >> (HEAD_DIM.bit_length() - 1))).astype(BF16)


def _interleave(*gens):
    results = [None] * len(gens)
    live = dict(enumerate(gens))
    while live:
        for n in list(live):
            try:
                next(live[n])
            except StopIteration as stop:
                results[n] = stop.value
                del live[n]
    return results


def _tri_masks(n):
    row = lax.broadcasted_iota(jnp.int32, (n, n), 0)
    col = lax.broadcasted_iota(jnp.int32, (n, n), 1)
    return row, col


def _chunk_tri(n):
    row, col = _tri_masks(n)
    shift = CHUNK.bit_length() - 1
    return (((row >> shift) == (col >> shift)) & (col <= row)).astype(BF16)


def _macaron_half_step(x, gpre_ref, gpost_ref, win_ref, wout_ref):
    xn = _rms(x, gpre_ref[...]).astype(BF16)
    yield
    splits = list(zip(FF_SPLITS[:-1], FF_SPLITS[1:]))
    gate_up = []
    acc = None
    for n in range(len(splits) + 1):
        if n < len(splits):
            lo, hi = splits[n]
            gate_up.append((jnp.dot(xn, win_ref[:, lo:hi], preferred_element_type=F32),
                            jnp.dot(xn, win_ref[:, D_FF + lo:D_FF + hi], preferred_element_type=F32)))
        if n > 0:
            lo, hi = splits[n - 1]
            gate, up = gate_up[n - 1]
            act = (gate * jax.nn.sigmoid(gate) * up).astype(BF16)
            yield
            part = jnp.dot(act, wout_ref[lo:hi, :], preferred_element_type=F32)
            acc = part if acc is None else acc + part
        yield
    return x + 0.5 * _rms(acc, gpost_ref[...])


def _staggered_row_parts(rows, size, tile):
    if rows < 2 * size:
        return _interleave(tile(0, rows))
    parts = rows // size

    def delayed(gen, phases):
        for _ in range(phases):
            yield
        yield from gen

    return _interleave(*[delayed(tile(n * size, size), n) for n in range(parts)])


def _pre_mix_body(x_ref, gpre_ref, gpost_ref, win_ref, wout_ref, gmix_ref, wproj_ref, conv_ref, mu_ref,
                  x_out_ref, om_ref, or_ref, oa_ref, og_ref, qk_pad_ref, u_pad_ref):
    @pl.when(pl.program_id(0) == 0)
    def _():
        qk_pad_ref[0:SUBLANES, :] = jnp.zeros((SUBLANES, 2 * MLSTM_W), F32)
        u_pad_ref[0:SUBLANES, :] = jnp.zeros((SUBLANES, RWKV_COLS), F32)

    def tile(r0, n):
        rows = slice(r0, r0 + n)
        x = yield from _macaron_half_step(x_ref[rows, :], gpre_ref, gpost_ref, win_ref, wout_ref)
        x_out_ref[rows, :] = x
        h = _rms(x, gmix_ref[...]).astype(BF16)
        yield
        starts = [sum(PROJ_WIDTHS[:g]) for g in range(len(PROJ_WIDTHS))]
        proj = [jnp.dot(h, wproj_ref[:, lo:lo + w], preferred_element_type=F32) for lo, w in zip(starts, PROJ_WIDTHS)]
        p_mlstm, p_rwkv, p_attn, p_gates = proj
        oa_ref[rows, :] = p_attn
        og_ref[rows, :] = p_gates
        yield
        qk_pad_ref[SUBLANES:SUBLANES + n, :] = p_mlstm[:, 0:2 * MLSTM_W]
        conv = None
        for tap in range(MLSTM_CONV):
            shifted = qk_pad_ref[pl.ds(SUBLANES - (MLSTM_CONV - 1) + tap, n), :]
            term = shifted * conv_ref[tap:tap + 1, :]
            conv = term if conv is None else conv + term
        qk_pad_ref[0:SUBLANES, :] = qk_pad_ref[n:n + SUBLANES, :]
        qk = conv * jax.nn.sigmoid(conv)
        om_ref[rows, 0:MLSTM_W] = qk[:, 0:MLSTM_W] * (HEAD_DIM ** -0.5)
        om_ref[rows, MLSTM_W:2 * MLSTM_W] = qk[:, MLSTM_W:2 * MLSTM_W]
        om_ref[rows, 2 * MLSTM_W:] = p_mlstm[:, 2 * MLSTM_W:]
        yield
        u_pad_ref[SUBLANES:SUBLANES + n, :] = p_rwkv
        prev = u_pad_ref[pl.ds(SUBLANES - 1, n), :]
        u_pad_ref[0:SUBLANES, :] = u_pad_ref[n:n + SUBLANES, :]
        or_ref[rows, :] = p_rwkv + (prev - p_rwkv) * mu_ref[...]

    _staggered_row_parts(x_ref.shape[0], PRE_MIX_PART, tile)


PROJ_WIDTHS = (MLSTM_MAIN, RWKV_COLS, ATTN_COLS, MLSTM_GATES)


def _pre_mix(x, gpre, gpost, w_ffn_in, w_ffn_out, gmix, w_proj, conv_w, rwkv_mu, layer, tm):
    s = x.shape[0]
    row = lambda i: (i, 0)
    lay3 = lambda i: (layer, 0, 0)
    widths = PROJ_WIDTHS
    part = tm if tm < 2 * PRE_MIX_PART else PRE_MIX_PART
    return pl.pallas_call(
        _pre_mix_body,
        out_shape=[jax.ShapeDtypeStruct((s, D_MODEL), F32)] + [jax.ShapeDtypeStruct((s, w), F32) for w in widths],
        grid=(s // tm,),
        in_specs=[
            pl.BlockSpec((tm, D_MODEL), row),
            _layer_vec(D_MODEL, layer),
            _layer_vec(D_MODEL, layer),
            _resident((None, D_MODEL, 2 * D_FF), lay3),
            _resident((None, D_FF, D_MODEL), lay3),
            _layer_vec(D_MODEL, layer),
            _resident((None, D_MODEL, sum(widths)), lay3),
            pl.BlockSpec((None, MLSTM_CONV, 2 * MLSTM_W), lay3),
            _layer_vec(RWKV_COLS, layer),
        ],
        out_specs=[pl.BlockSpec((tm, D_MODEL), row)] + [pl.BlockSpec((tm, w), row) for w in widths],
        scratch_shapes=[
            pltpu.VMEM((part + SUBLANES, 2 * MLSTM_W), F32),
            pltpu.VMEM((part + SUBLANES, RWKV_COLS), F32),
        ],
        compiler_params=_params(("arbitrary",)),
        name="pre_mix",
    )(x, gpre, gpost, w_ffn_in, w_ffn_out, gmix, w_proj, conv_w, rwkv_mu)


def _post_mix_body(ya_ref, ym_ref, yr_ref, x_ref, p_ref, wo_ref, gmix_ref, gpre_ref, gpost_ref, win_ref, wout_ref,
                   gple_pre_ref, gple_post_ref, wgate_ref, wemb_ref, o_ref):
    def tile(r0, n):
        rows = slice(r0, r0 + n)
        mix = jnp.dot(ya_ref[rows, :], wo_ref[0:ATTN_W, :], preferred_element_type=F32)
        mix += jnp.dot(ym_ref[rows, :], wo_ref[ATTN_W:ATTN_W + MLSTM_W, :], preferred_element_type=F32)
        mix += jnp.dot(yr_ref[rows, :], wo_ref[ATTN_W + MLSTM_W:, :], preferred_element_type=F32)
        emb = jnp.dot(p_ref[rows, :].astype(BF16), wemb_ref[...], preferred_element_type=F32)
        yield
        x = x_ref[rows, :] + _rms(mix, gmix_ref[...])
        x = yield from _macaron_half_step(x, gpre_ref, gpost_ref, win_ref, wout_ref)
        h = _rms(x, gple_pre_ref[...]).astype(BF16)
        yield
        gate = jax.nn.sigmoid(jnp.dot(h, wgate_ref[...], preferred_element_type=F32))
        yield
        o_ref[rows, :] = x + _rms(gate * emb, gple_post_ref[...])

    _staggered_row_parts(x_ref.shape[0], POST_MIX_PART, tile)


def _post_mix(ya, ym, yr, x, p, w_out, gmix, gpre, gpost, w_ffn_in, w_ffn_out, gple_pre, gple_post,
              w_gate, w_emb, layer, tm):
    s = x.shape[0]
    row = lambda i: (i, 0)
    lay3 = lambda i: (layer, 0, 0)
    vec = _layer_vec(D_MODEL, layer)
    return pl.pallas_call(
        _post_mix_body,
        out_shape=jax.ShapeDtypeStruct((s, D_MODEL), F32),
        grid=(s // tm,),
        in_specs=[
            pl.BlockSpec((tm, ATTN_W), row),
            pl.BlockSpec((tm, MLSTM_W), row),
            pl.BlockSpec((tm, RWKV_W), row),
            pl.BlockSpec((tm, D_MODEL), row),
            pl.BlockSpec((None, tm, D_PLE), lambda i: (layer, i, 0)),
            _resident((None, D_MODEL, D_MODEL), lay3),
            vec, vec, vec,
            _resident((None, D_MODEL, 2 * D_FF), lay3),
            _resident((None, D_FF, D_MODEL), lay3),
            vec, vec,
            _resident((None, D_MODEL, D_MODEL), lay3),
            _resident((None, D_PLE, D_MODEL), lay3),
        ],
        out_specs=pl.BlockSpec((tm, D_MODEL), row),
        compiler_params=_params(("parallel",)),
        name="post_mix",
    )(ya, ym, yr, x, p, w_out, gmix, gpre, gpost, w_ffn_in, w_ffn_out, gple_pre, gple_post, w_gate, w_emb)


def _rope_body(pos_ref, invf_ref, cos_ref, sin_ref):
    ang = pos_ref[...].astype(F32) * invf_ref[...]
    sin = jnp.sin(ang)
    dim = lax.broadcasted_iota(jnp.int32, ang.shape, 1) & (HEAD_DIM - 1)
    cos_ref[...] = jnp.cos(ang)
    sin_ref[...] = jnp.where(dim < ROPE_HALF, -sin, jnp.where(dim < ROPE_DIM, sin, 0.0))


def _rope_tables(positions, tm):
    s = positions.shape[0]
    lane = jnp.arange(2 * HEAD_DIM) % HEAD_DIM
    freq = ROPE_THETA ** (-jnp.arange(0, ROPE_DIM, 2, dtype=F32) / ROPE_DIM)
    invf = jnp.where(lane < ROPE_DIM, freq[lane % ROPE_HALF], 0.0).astype(F32)[None, :]
    row = lambda i: (i, 0)
    out = jax.ShapeDtypeStruct((s, 2 * HEAD_DIM), F32)
    return pl.pallas_call(
        _rope_body,
        out_shape=[out, out],
        grid=(s // tm,),
        in_specs=[pl.BlockSpec((tm, 1), row), pl.BlockSpec((1, 2 * HEAD_DIM), lambda i: (0, 0))],
        out_specs=[pl.BlockSpec((tm, 2 * HEAD_DIM), row)] * 2,
        compiler_params=_params(("parallel",)),
        name="rope_tables",
    )(positions, invf)


ATTN_BLOCKS = 8
GROUP_W = ATTN_GROUP * HEAD_DIM
KEYS = 2 * WINDOW


def _rope_partner():
    row, col = _tri_masks(2 * HEAD_DIM)
    dim = col & (HEAD_DIM - 1)
    first = (dim < ROPE_HALF) & (row == col + ROPE_HALF)
    second = (dim >= ROPE_HALF) & (dim < ROPE_DIM) & (row == col - ROPE_HALF)
    return (first | second).astype(BF16)


def _attn_body(sinks_ref, p_ref, cos_ref, sin_ref, o_ref, ktprev_ref, vprev_ref, *, nblk):
    step = pl.program_id(0)

    @pl.when(step == 0)
    def _():
        ktprev_ref[...] = jnp.zeros_like(ktprev_ref)
        vprev_ref[...] = jnp.zeros_like(vprev_ref)

    cos, sin = cos_ref[...], sin_ref[...]
    partner = _rope_partner()
    period = 2 * HEAD_DIM

    def rope(x):
        slabs = [x[:, c:c + period] for c in range(0, x.shape[1], period)]
        return jnp.concatenate(
            [s * cos + jnp.dot(s.astype(BF16), partner, preferred_element_type=F32) * sin for s in slabs], axis=1)

    q = (rope(p_ref[:, 0:ATTN_W]) * (HEAD_DIM ** -0.5)).astype(BF16)
    k_cur = rope(p_ref[:, ATTN_W:ATTN_W + KV_W])
    v_f32 = p_ref[:, ATTN_W + KV_W:ATTN_COLS]
    kt = jnp.concatenate([ktprev_ref[...], k_cur.T.astype(BF16)], axis=1)
    vv = jnp.concatenate([vprev_ref[0], v_f32.astype(BF16)], axis=0)
    vs = jnp.concatenate([vprev_ref[1], pltpu.roll(v_f32, HEAD_DIM, 1).astype(BF16)], axis=0)
    ktprev_ref[...] = kt[:, nblk * WINDOW:]
    vprev_ref[0] = vv[nblk * WINDOW:, :]
    vprev_ref[1] = vs[nblk * WINDOW:, :]

    t = lax.broadcasted_iota(jnp.int32, (WINDOW, ATTN_GROUP * KEYS), 0)
    lane = lax.broadcasted_iota(jnp.int32, (WINDOW, ATTN_GROUP * KEYS), 1)
    c = lane & (KEYS - 1)
    head = lane >> (KEYS.bit_length() - 1)
    cur_ok = (c >= WINDOW) & (c - WINDOW <= t)
    prev_ok = (c < WINDOW) & (c > t)

    def bias(g, has_prev):
        b = jnp.where(cur_ok | (prev_ok & has_prev), 0.0, NEG_INF)
        sink = jnp.zeros_like(b)
        for n in range(ATTN_GROUP):
            sink = jnp.where(head == n, sinks_ref[g * ATTN_GROUP + n], sink)
        return jnp.where(c == 0, sink, b)

    bias_first = [bias(g, step > 0) for g in range(ATTN_KV_HEADS)]
    bias_rest = [bias(g, True) for g in range(ATTN_KV_HEADS)]

    key_lane = lax.broadcasted_iota(jnp.int32, (HEAD_DIM, KEYS), 1)
    zero_kt = jnp.zeros((HEAD_DIM, KEYS), BF16)
    vrow = lax.broadcasted_iota(jnp.int32, (KEYS, KV_W), 0)
    v_low = lax.broadcasted_iota(jnp.int32, (KEYS, KV_W), 1) < HEAD_DIM
    o_low = lax.broadcasted_iota(jnp.int32, (WINDOW, KV_W), 1) < HEAD_DIM
    one = jnp.ones((KEYS, KV_W), BF16)

    def blocks():
        units = [(b, g) for b in range(nblk) for g in range(ATTN_KV_HEADS)]
        kbd, v_even, v_odd = [], [], []
        for b, g in units:
            ktg = kt[g * HEAD_DIM:(g + 1) * HEAD_DIM, b * WINDOW:b * WINDOW + KEYS]
            ktg = jnp.where(key_lane == 0, jnp.zeros_like(ktg), ktg)
            kbd.append(jnp.concatenate(
                [jnp.concatenate([ktg if m == n else zero_kt for m in range(ATTN_GROUP)], axis=1)
                 for n in range(ATTN_GROUP)], axis=0))
            window = slice(b * WINDOW, b * WINDOW + KEYS)
            in_low, in_high = (vv, vs) if g == 0 else (vs, vv)
            v_even.append(jnp.where(v_low, jnp.where(vrow == 0, jnp.zeros_like(one), in_low[window, :]), one))
            v_odd.append(jnp.where(v_low, one, jnp.where(vrow == 0, jnp.zeros_like(one), in_high[window, :])))
        yield
        scores = [jnp.dot(q[b * WINDOW:(b + 1) * WINDOW, g * GROUP_W:(g + 1) * GROUP_W], kb,
                          preferred_element_type=F32) + (bias_first[g] if b == 0 else bias_rest[g])
                  for (b, g), kb in zip(units, kbd)]
        yield
        probs = []
        for s in scores:
            parts = []
            for n in range(ATTN_GROUP):
                sn = s[:, n * KEYS:(n + 1) * KEYS]
                parts.append(jnp.exp(sn - jnp.max(sn, axis=-1, keepdims=True)).astype(BF16))
            probs.append(parts)
        yield
        outs = [[jnp.dot(pn, even if n % 2 == 0 else odd, preferred_element_type=F32) for n, pn in enumerate(parts)]
                for parts, even, odd in zip(probs, v_even, v_odd)]
        yield
        for (b, g), heads in zip(units, outs):
            normed = [od * pltpu.roll(1.0 / od, HEAD_DIM, 1) for od in heads]
            pairs = [jnp.where(o_low, normed[n], normed[n + 1]) for n in range(0, ATTN_GROUP, 2)]
            o_ref[b * WINDOW:(b + 1) * WINDOW, g * GROUP_W:(g + 1) * GROUP_W] = (
                jnp.concatenate(pairs, axis=1).astype(o_ref.dtype))

    _interleave(blocks())


def _attention(sinks, attn_p, cos, sin, layer):
    s = attn_p.shape[0]
    nblk = min(ATTN_BLOCKS, s // WINDOW)
    tq = nblk * WINDOW
    row = lambda i: (i, 0)
    return pl.pallas_call(
        functools.partial(_attn_body, nblk=nblk),
        out_shape=jax.ShapeDtypeStruct((s, ATTN_W), BF16),
        grid=(s // tq,),
        in_specs=[
            pl.BlockSpec(memory_space=pltpu.SMEM),
            pl.BlockSpec((tq, ATTN_COLS), row),
            pl.BlockSpec((tq, 2 * HEAD_DIM), row),
            pl.BlockSpec((tq, 2 * HEAD_DIM), row),
        ],
        out_specs=pl.BlockSpec((tq, ATTN_W), row),
        scratch_shapes=[pltpu.VMEM((KV_W, WINDOW), BF16), pltpu.VMEM((2, WINDOW, KV_W), BF16)],
        compiler_params=_params(("arbitrary",)),
        name="swa_attention",
    )(sinks[layer], attn_p, cos, sin)


def _mlstm_body(p_ref, gates_ref, bias_ref, norm_ref, o_ref, c_ref, n_ref, *, tb):
    step = pl.program_id(0)

    @pl.when(step == 0)
    def _():
        c_ref[...] = jnp.zeros_like(c_ref)
        n_ref[...] = jnp.zeros_like(n_ref)

    def gate_lanes(r0, n):
        pre = GATE_CAP * jnp.tanh((gates_ref[r0:r0 + n, :] + bias_ref[...]) / GATE_CAP)
        yield
        logsig = jnp.minimum(pre, 0.0) - jnp.log(1.0 + jnp.exp(-jnp.abs(pre)))
        yield
        g_cum = _cumsum_rows(_chunk_tri(n), logsig)
        yield
        gate_col = lax.broadcasted_iota(jnp.int32, (n, MLSTM_GATES), 1)
        lanes = _split_dot(jnp.where(gate_col < MLSTM_HEADS, pre, g_cum), _gate_spread())
        return lanes[:, 0:MLSTM_W], lanes[:, MLSTM_W:2 * MLSTM_W]

    head_ones = _head_ones(MLSTM_W)

    hshift = HEAD_DIM.bit_length() - 1
    lane = lax.broadcasted_iota(jnp.int32, (CHUNK, MLSTM_W), 1)
    time = lax.broadcasted_iota(jnp.int32, (CHUNK, MLSTM_W), 0)
    key_of_lane = lane & (HEAD_DIM - 1)
    causal = key_of_lane <= time
    eye = key_of_lane == time
    brow, bcol = _tri_masks(MLSTM_W)
    same_head = (brow >> hshift) == (bcol >> hshift)
    head_block = same_head.astype(BF16)
    stack = lambda x: jnp.where(same_head, jnp.concatenate([x] * MLSTM_HEADS, axis=0), 0.0)

    def to_row(lane_bcast):
        return jnp.sum(jnp.where(eye, lane_bcast, 0.0), axis=0, keepdims=True)

    state = dict(c=c_ref[...], n=n_ref[0:1, :])

    def chunks(r0, n, i_all, g_all):
        chunk_starts = range(0, n, CHUNK)
        local = lambda x: [x[c0:c0 + CHUNK, :] for c0 in chunk_starts]
        block = lambda lo, hi: [p_ref[r0 + c0:r0 + c0 + CHUNK, lo:hi] for c0 in chunk_starts]
        q, k, v = block(0, MLSTM_W), block(MLSTM_W, 2 * MLSTM_W), block(2 * MLSTM_W, 3 * MLSTM_W)
        qk = [_mm_nt(a, stack(b)) for a, b in zip(q, k)]
        yield
        g_chunk, i_chunk = local(g_all), local(i_all)
        g_tot = [g[CHUNK - 1:CHUNK, :] for g in g_chunk]
        w_in = [jnp.exp(t - g + i) for t, g, i in zip(g_tot, g_chunk, i_chunk)]
        e_tot = [jnp.exp(t) for t in g_tot]
        e_g = [jnp.exp(g) for g in g_chunk]
        n_loc = [jnp.sum(w * x, axis=0, keepdims=True) for w, x in zip(w_in, k)]
        yield
        decay = [jnp.exp(jnp.where(causal, g - to_row(g) + to_row(i), NEG_INF)) for g, i in zip(g_chunk, i_chunk)]
        s_mat = [x * d for x, d in zip(qk, decay)]
        yield
        num_den = [_mm(s, jnp.concatenate([stack(x).astype(BF16), head_block], axis=1))
                   for s, x in zip(s_mat, v)]
        yield
        c_loc = [jnp.where(same_head, _mm_tn(w * x, b), 0.0) for w, x, b in zip(w_in, v, k)]
        yield
        c_in, n_in = [], []
        for m in range(len(chunk_starts)):
            c_in.append(state["c"])
            n_in.append(state["n"])
            state["c"] = e_tot[m] * state["c"] + c_loc[m]
            state["n"] = e_tot[m] * state["n"] + n_loc[m]
        inter = [_mm_nt(a, jnp.concatenate([c, stack(jnp.broadcast_to(x, (CHUNK, MLSTM_W)))], axis=0))
                 for a, c, x in zip(q, c_in, n_in)]
        yield
        for c0, nd, e, x in zip(chunk_starts, num_den, e_g, inter):
            hid = ((nd[:, :MLSTM_W] + e * x[:, :MLSTM_W])
                   / jnp.maximum(jnp.abs(nd[:, MLSTM_W:] + e * x[:, MLSTM_W:]), 1.0))
            mean_sq = _split_dot(hid * hid, head_ones, terms=2) * (1.0 / HEAD_DIM)
            hid = hid * lax.rsqrt(mean_sq + NORM_EPS) * norm_ref[...]
            rows = slice(r0 + c0, r0 + c0 + CHUNK)
            o_ref[rows, :] = (jax.nn.sigmoid(p_ref[rows, 3 * MLSTM_W:4 * MLSTM_W]) * hid).astype(o_ref.dtype)
            yield

    size = tb if tb < 2 * MLSTM_PART else MLSTM_PART
    starts = list(range(0, tb, size))
    gates = _interleave(gate_lanes(starts[0], size))[0]
    for n, r0 in enumerate(starts):
        traces = [chunks(r0, size, *gates)]
        if n + 1 < len(starts):
            traces.append(gate_lanes(starts[n + 1], size))
        results = _interleave(*traces)
        gates = results[-1]
    c_ref[...] = state["c"]
    n_ref[0:1, :] = state["n"]


def _mlstm(mlstm_p, gates, bias, norm, layer, tb):
    s = mlstm_p.shape[0]
    row = lambda i: (i, 0)
    return pl.pallas_call(
        functools.partial(_mlstm_body, tb=tb),
        out_shape=jax.ShapeDtypeStruct((s, MLSTM_W), BF16),
        grid=(s // tb,),
        in_specs=[
            pl.BlockSpec((tb, MLSTM_MAIN), row),
            pl.BlockSpec((tb, MLSTM_GATES), row),
            _layer_vec(MLSTM_GATES, layer),
            _layer_vec(MLSTM_W, layer),
        ],
        out_specs=pl.BlockSpec((tb, MLSTM_W), row),
        scratch_shapes=[
            pltpu.VMEM((MLSTM_W, MLSTM_W), F32),
            pltpu.VMEM((SUBLANES, MLSTM_W), F32),
        ],
        compiler_params=_params(("arbitrary",)),
        name="mlstm",
    )(mlstm_p, gates, bias, norm)


def _rwkv_chunk_terms(a_t, b_t, k_t, r_t, b_end, k_end, v):
    row, col = _tri_masks(CHUNK)
    incl = col <= row
    strict = col < row
    eye = (col == row).astype(F32)
    half = HEAD_DIM

    quad = [_mm_nt(jnp.concatenate([a, r], axis=0), jnp.concatenate([b, k], axis=0))
            for a, r, b, k in zip(a_t, r_t, b_t, k_t)]
    yield
    n_mat = [jnp.where(strict, x[:CHUNK, :CHUNK], 0.0) for x in quad]
    a_ak = [jnp.where(strict, x[:CHUNK, CHUNK:], 0.0) for x in quad]
    c_rb = [jnp.where(incl, x[CHUNK:, :CHUNK], 0.0) for x in quad]
    c_rk = [jnp.where(incl, x[CHUNK:, CHUNK:], 0.0) for x in quad]

    inv = [eye + n for n in n_mat]
    power = n_mat
    for _ in range(NEUMANN_STEPS):
        power = [_mm(x, x) for x in power]
        yield
        inv = [m + _mm(m, x) for m, x in zip(inv, power)]
        yield

    z = [_mm(x, y) for x, y in zip(a_ak, v)]
    yield
    w12 = [_mm(m, jnp.concatenate([a, y], axis=1)) for m, a, y in zip(inv, a_t, z)]
    yield
    cw = [_mm(c, w) for c, w in zip(c_rb, w12)]
    yield
    ckv = [_mm(c, y) for c, y in zip(c_rk, v)]
    yield
    gh = [_mm_tn(w, b) for w, b in zip(w12, b_end)]
    yield
    vk = [_mm_tn(y, k) for y, k in zip(v, k_end)]
    yield
    q_eff = [r + x[:, :half] for r, x in zip(r_t, cw)]
    y_loc = [x[:, half:] + y for x, y in zip(cw, ckv)]
    g_mat = [x[:half, :] for x in gh]
    h_mat = [x[half:, :] + y for x, y in zip(gh, vk)]
    return q_eff, y_loc, g_mat, h_mat


def _rwkv_body(p_ref, w0_ref, wup_ref, a0_ref, aup_ref, gup_ref, kk_ref, ka_ref, rk_ref,
               lnw_ref, lnb_ref, o_ref, s_ref, stage_ref, end_ref, terms_ref, fwd_ref, fwd_end_ref, *, tb):
    step = pl.program_id(0)

    @pl.when(step == 0)
    def _():
        s_ref[...] = jnp.zeros_like(s_ref)
        stage_ref[...] = jnp.zeros_like(stage_ref)
        end_ref[...] = jnp.zeros_like(end_ref)
        terms_ref[...] = jnp.zeros_like(terms_ref)
        fwd_ref[...] = jnp.zeros_like(fwd_ref)
        fwd_end_ref[...] = jnp.zeros_like(fwd_end_ref)

    refs = (p_ref, w0_ref, wup_ref, a0_ref, aup_ref, gup_ref, kk_ref, ka_ref, rk_ref,
            lnw_ref, lnb_ref, o_ref, s_ref, stage_ref, end_ref, terms_ref, fwd_ref, fwd_end_ref)
    for cur in (0, 1):
        pl.when(lax.rem(step, 2) == cur)(functools.partial(_rwkv_step, *refs, tb=tb, cur=cur, prv=1 - cur))


def _rwkv_step(p_ref, w0_ref, wup_ref, a0_ref, aup_ref, gup_ref, kk_ref, ka_ref, rk_ref,
               lnw_ref, lnb_ref, o_ref, s_ref, stage_ref, end_ref, terms_ref, fwd_ref, fwd_end_ref,
               *, tb, cur, prv):
    head_ones = _head_ones(RWKV_W)
    head_sum = lambda x: _split_dot(x, head_ones, terms=2)
    head_lanes = lambda h: slice(h * HEAD_DIM, (h + 1) * HEAD_DIM)
    chunk_starts = range(0, tb, CHUNK)
    units = [(n, c0, h) for n, c0 in enumerate(chunk_starts) for h in range(RWKV_HEADS)]

    unit_rows = lambda u: slice(u * CHUNK, (u + 1) * CHUNK)

    def terms_previous():
        staged = {name: stage_ref[prv, n] for n, name in enumerate(RWKV_STAGED)}
        per_unit = {name: [staged[name][c0:c0 + CHUNK, head_lanes(h)] for _, c0, h in units]
                    for name in RWKV_STAGED[:7]}
        terms = yield from _rwkv_chunk_terms(**per_unit)
        for kind, per_unit_values in enumerate(terms):
            for u, value in enumerate(per_unit_values):
                terms_ref[prv, kind, unit_rows(u), :] = value
            yield
        for n, name in enumerate(RWKV_FORWARDED):
            fwd_ref[prv, n] = staged[name]
        fwd_end_ref[prv] = end_ref[prv]

    def finish_older():
        ends = fwd_end_ref[cur]
        state = [s_ref[h] for h in range(RWKV_HEADS)]
        ys = []
        for u, (n, c0, h) in enumerate(units):
            q_eff, y_loc, g_mat, h_mat = (terms_ref[cur, kind, unit_rows(u), :] for kind in range(4))
            ys.append(_mm_nt(q_eff, state[h]) + y_loc)
            state[h] = state[h] * ends[n:n + 1, head_lanes(h)] + _mm(state[h], g_mat) + h_mat
            if h == RWKV_HEADS - 1:
                yield
        for h in range(RWKV_HEADS):
            s_ref[h] = state[h]
        y_all = jnp.concatenate([jnp.concatenate(ys[n:n + RWKV_HEADS], axis=1)
                                 for n in range(0, len(ys), RWKV_HEADS)], axis=0)
        centred = y_all - head_sum(y_all) * (1.0 / HEAD_DIM)
        yield
        var = head_sum(centred * centred) * (1.0 / HEAD_DIM)
        normed = centred * lax.rsqrt(var + RWKV_GN_EPS) * lnw_ref[...] + lnb_ref[...]
        bonus, v_all, gate = (fwd_ref[cur, n] for n in range(len(RWKV_FORWARDED)))
        o_ref[...] = ((normed + bonus * v_all) * gate).astype(o_ref.dtype)

    def stage_current():
        for r0 in range(0, tb, RWKV_SLAB):
            rows = slice(r0, r0 + RWKV_SLAB)

            def stage(name, value):
                stage_ref[cur, RWKV_STAGED.index(name), rows, :] = value

            u = p_ref[rows, :]
            yield
            r_all = u[:, 0:RWKV_W]
            k_raw = u[:, RWKV_W:2 * RWKV_W]
            v_all = u[:, 2 * RWKV_W:3 * RWKV_W]
            x_w = u[:, 3 * RWKV_W:3 * RWKV_W + RWKV_W_RANK]
            x_a = u[:, 3 * RWKV_W + RWKV_W_RANK:3 * RWKV_W + RWKV_W_RANK + RWKV_A_RANK]
            x_g = u[:, 3 * RWKV_W + RWKV_W_RANK + RWKV_A_RANK:RWKV_COLS]
            z = w0_ref[...] + _mm(jnp.tanh(x_w), wup_ref[...])
            ld_all = -jnp.exp(-0.5) * jax.nn.sigmoid(z)
            a_all = jax.nn.sigmoid(a0_ref[...] + _mm(x_a, aup_ref[...]))
            yield
            kk_all = k_raw * kk_ref[...]
            k_all = k_raw * (1.0 + (a_all - 1.0) * ka_ref[...])
            kk_all = kk_all * lax.rsqrt(jnp.maximum(head_sum(kk_all * kk_all), 1e-24))
            a_vec = -kk_all
            b_vec = kk_all * a_all
            yield
            lp_all = _cumsum_rows(_chunk_tri(RWKV_SLAB), ld_all)
            stage("v", v_all)
            stage("bonus", head_sum(r_all * k_all * rk_ref[...]))
            stage("gate", _mm(jax.nn.sigmoid(x_g), gup_ref[...]))
            yield
            starts = range(0, RWKV_SLAB, CHUNK)
            lp_last = jnp.concatenate(
                [jnp.broadcast_to(lp_all[c0 + CHUNK - 1:c0 + CHUNK, :], (CHUNK, RWKV_W)) for c0 in starts], axis=0)
            grow = jnp.exp(-lp_all)
            to_end = jnp.exp(lp_last - lp_all)
            stage("a_t", a_vec * jnp.exp(lp_all - ld_all))
            stage("b_t", b_vec * grow)
            stage("k_t", k_all * grow)
            yield
            stage("r_t", r_all * jnp.exp(lp_all))
            stage("b_end", b_vec * to_end)
            stage("k_end", k_all * to_end)
            for c0 in starts:
                n = (r0 + c0) // CHUNK
                end_ref[cur, n:n + 1, :] = jnp.exp(lp_all[c0 + CHUNK - 1:c0 + CHUNK, :])
            yield

    _interleave(terms_previous(), finish_older(), stage_current())


def _rwkv(rwkv_p, w0, w_up, a0, a_up, g_up, k_k, k_a, r_k, ln_w, ln_b, layer, tb):
    s = rwkv_p.shape[0]
    blocks = s // tb
    lay3 = lambda i: (layer, 0, 0)
    vec = lambda width: _layer_vec(width, layer)
    return pl.pallas_call(
        functools.partial(_rwkv_body, tb=tb),
        out_shape=jax.ShapeDtypeStruct((s, RWKV_W), BF16),
        grid=(blocks + 2,),
        in_specs=[
            pl.BlockSpec((tb, RWKV_COLS), lambda i: (jnp.minimum(i, blocks - 1), 0)),
            vec(RWKV_W),
            pl.BlockSpec((None, RWKV_W_RANK, RWKV_W), lay3),
            vec(RWKV_W),
            pl.BlockSpec((None, RWKV_A_RANK, RWKV_W), lay3),
            pl.BlockSpec((None, RWKV_G_RANK, RWKV_W), lay3),
            vec(RWKV_W), vec(RWKV_W), vec(RWKV_W), vec(RWKV_W), vec(RWKV_W),
        ],
        out_specs=pl.BlockSpec((tb, RWKV_W), lambda i: (jnp.maximum(i - 2, 0), 0)),
        scratch_shapes=[
            pltpu.VMEM((RWKV_HEADS, HEAD_DIM, HEAD_DIM), F32),
            pltpu.VMEM((2, len(RWKV_STAGED), tb, RWKV_W), F32),
            pltpu.VMEM((2, tb // CHUNK, RWKV_W), F32),
            pltpu.VMEM((2, 4, (tb // CHUNK) * RWKV_HEADS * CHUNK, HEAD_DIM), F32),
            pltpu.VMEM((2, len(RWKV_FORWARDED), tb, RWKV_W), F32),
            pltpu.VMEM((2, tb // CHUNK, RWKV_W), F32),
        ],
        compiler_params=_params(("arbitrary",)),
        name="rwkv7",
    )(rwkv_p, w0, w_up, a0, a_up, g_up, k_k, k_a, r_k, ln_w, ln_b)


def _tiles(s):
    return min(512, s), min(512, s), min(2048, s)


def kernel(x, p, positions, ln_ffn1_pre, ln_ffn1_post, w_ffn1_in, w_ffn1_out, ln_mix_pre, w_in, attn_sinks, mlstm_conv, mlstm_i_bias, mlstm_f_bias, mlstm_norm, rwkv_mu, rwkv_w0, rwkv_w_up, rwkv_a0, rwkv_a_up, rwkv_g_up, rwkv_k_k, rwkv_k_a, rwkv_r_k, rwkv_ln_w, rwkv_ln_b, w_out, ln_mix_post, ln_ffn2_pre, ln_ffn2_post, w_ffn2_in, w_ffn2_out, ln_ple_pre, w_ple_gate, w_ple_proj, ln_ple_post):
    batch, seq, _ = x.shape
    assert batch == 1 and seq % WINDOW == 0
    depth = w_in.shape[0]
    tm, tb, tb_mlstm = _tiles(seq)

    bf = lambda w: w.astype(BF16)
    w_ffn1_in, w_ffn1_out, w_ffn2_in, w_ffn2_out = map(bf, (w_ffn1_in, w_ffn1_out, w_ffn2_in, w_ffn2_out))
    w_out, w_ple_gate, w_ple_proj = map(bf, (w_out, w_ple_gate, w_ple_proj))
    rwkv_w_up, rwkv_a_up, rwkv_g_up = map(bf, (rwkv_w_up, rwkv_a_up, rwkv_g_up))
    m0 = ATTN_COLS
    g0 = m0 + MLSTM_MAIN
    r0 = g0 + MLSTM_GATES
    w_proj = bf(jnp.concatenate([w_in[:, :, m0:g0], w_in[:, :, r0:], w_in[:, :, :m0], w_in[:, :, g0:r0]], axis=-1))
    vec = lambda a: a.reshape(depth, 1, -1)
    gate_bias = vec(jnp.concatenate([mlstm_i_bias, mlstm_f_bias], axis=-1))
    (ln_ffn1_pre, ln_ffn1_post, ln_mix_pre, ln_mix_post, ln_ffn2_pre, ln_ffn2_post, ln_ple_pre,
     ln_ple_post, mlstm_norm, rwkv_mu, rwkv_w0, rwkv_a0, rwkv_k_k, rwkv_k_a, rwkv_r_k, rwkv_ln_w,
     rwkv_ln_b) = map(vec, (
         ln_ffn1_pre, ln_ffn1_post, ln_mix_pre, ln_mix_post, ln_ffn2_pre, ln_ffn2_post, ln_ple_pre,
         ln_ple_post, mlstm_norm, rwkv_mu, rwkv_w0, rwkv_a0, rwkv_k_k, rwkv_k_a, rwkv_r_k, rwkv_ln_w,
         rwkv_ln_b))

    cos, sin = _rope_tables(positions.reshape(seq, 1), tm)
    xs = x.reshape(seq, D_MODEL)
    for l in range(depth):
        xs, mlstm_p, rwkv_p, attn_p, gates = _pre_mix(
            xs, ln_ffn1_pre, ln_ffn1_post, w_ffn1_in, w_ffn1_out, ln_mix_pre, w_proj, mlstm_conv, rwkv_mu, l, tm)
        y_attn = _attention(attn_sinks, attn_p, cos, sin, l)
        y_mlstm = _mlstm(mlstm_p, gates, gate_bias, mlstm_norm, l, tb_mlstm)
        y_rwkv = _rwkv(rwkv_p, rwkv_w0, rwkv_w_up, rwkv_a0, rwkv_a_up, rwkv_g_up,
                       rwkv_k_k, rwkv_k_a, rwkv_r_k, rwkv_ln_w, rwkv_ln_b, l, tb)
        xs = _post_mix(y_attn, y_mlstm, y_rwkv, xs, p.reshape(depth, seq, D_PLE), w_out, ln_mix_post,
                       ln_ffn2_pre, ln_ffn2_post, w_ffn2_in, w_ffn2_out, ln_ple_pre, ln_ple_post,
                       w_ple_gate, w_ple_proj, l, tm)
    return xs.reshape(batch, seq, D_MODEL)
```
